```python
import jax, jax.numpy as jnp
from jax import lax
import numpy as np

D_MODEL = 1024
BATCH = 4
SEQ = 8192
DEPTH = 2

CHUNK = 64
N_META = 16
Q_BLOCK = 128
EPS = 1e-6

D_MIX = D_MODEL
N_GROUPS = 4
GROUP_WIDTH = D_MIX // N_GROUPS

LRU_WIDTH = GROUP_WIDTH
LRU_BLOCKS = 4
LRU_BLOCK = LRU_WIDTH // LRU_BLOCKS
CONV_WIDTH = 4
LRU_C = 8.0
HG_HEADS = 4
HG_DV = GROUP_WIDTH // HG_HEADS
HG_DK = 64
HG_CHUNK = 16
MLA_HEADS = 4
MLA_NOPE = 64
MLA_ROPE = 32
MLA_V = GROUP_WIDTH // MLA_HEADS
MLA_Q_RANK = 192
MLA_KV_RANK = 128
ROPE_THETA = 10000.0
FOX_HEADS = 4
FOX_HD = GROUP_WIDTH // FOX_HEADS
D_FF = 2816
N_EXPERTS = 8
TOP_K = 2
D_FF_EXPERT = 3584
N_DENSE = (DEPTH + 1) // 2
N_MOE = DEPTH // 2

IN_SIZES = (LRU_WIDTH, LRU_WIDTH,
            HG_HEADS * HG_DK, HG_HEADS * HG_DK, HG_HEADS * HG_DV, HG_HEADS * HG_DV,
            MLA_Q_RANK, MLA_KV_RANK, MLA_ROPE,
            FOX_HEADS * FOX_HD, FOX_HEADS * FOX_HD, FOX_HEADS * FOX_HD, FOX_HEADS)
IN_COLS = sum(IN_SIZES)

kernel_name = "hymba_hybrid_rglru_hgrn2_mla_fox_moe"

F32 = jnp.float32


def rmsnorm(x, g):
    xf = x.astype(F32)
    y = xf * lax.rsqrt(jnp.mean(xf * xf, axis=-1, keepdims=True) + EPS)
    return (y * g.astype(F32)).astype(x.dtype)


def group_rmsnorm(y, g):
    B, Lp, _ = y.shape
    yg = y.reshape(B, Lp, N_GROUPS, GROUP_WIDTH)
    return rmsnorm(yg, g.reshape(N_GROUPS, GROUP_WIDTH)).reshape(B, Lp, D_MIX)


def chunk_ids(pos):
    return jnp.where(pos < N_META, 0, 1 + (pos - N_META) // CHUNK)


def rope(t, pos):
    half = t.shape[-1] // 2
    inv = ROPE_THETA ** (-jnp.arange(half, dtype=F32) / half)
    ang = pos.astype(F32)[:, None] * inv[None, :]
    cos = jnp.cos(ang)[:, None, :]
    sin = jnp.sin(ang)[:, None, :]
    t1 = t[..., :half].astype(F32)
    t2 = t[..., half:].astype(F32)
    return jnp.concatenate([t1 * cos - t2 * sin, t1 * sin + t2 * cos], axis=-1).astype(t.dtype)


def blocked_attention(q, k, v, score_mod):
    B, H, Lp, dk = q.shape
    nb = Lp // Q_BLOCK
    scale = dk ** -0.5
    qb = q.reshape(B, H, nb, Q_BLOCK, dk).transpose(2, 0, 1, 3, 4)

    def one_block(args):
        q_blk, blk = args
        s = jnp.einsum('bhqd,bhkd->bhqk', q_blk, k).astype(F32) * scale
        p = jax.nn.softmax(score_mod(s, blk * Q_BLOCK), axis=-1)
        return jnp.einsum('bhqk,bhkd->bhqd', p.astype(v.dtype), v)

    o = lax.map(one_block, (qb, jnp.arange(nb)))
    return o.transpose(1, 2, 0, 3, 4).reshape(B, H, Lp, v.shape[-1])


def _linear_combine(e1, e2):
    a1, b1 = e1
    a2, b2 = e2
    return a1 * a2, a2 * b1 + b2


def rglru_mixer(xb, gb, conv_w, conv_b, wa, ba, wx, bx, lam):
    B, Lp, W = xb.shape
    xp = jnp.pad(xb, ((0, 0), (CONV_WIDTH - 1, 0), (0, 0)))
    u = conv_b + xp[:, 0:Lp] * conv_w[0]
    for j in range(1, CONV_WIDTH):
        u = u + xp[:, j:j + Lp] * conv_w[j]
    ub = u.reshape(B, Lp, LRU_BLOCKS, LRU_BLOCK)
    r = jax.nn.sigmoid(jnp.einsum('blnc,ncd->blnd', ub, wa) + ba).reshape(B, Lp, W)
    i = jax.nn.sigmoid(jnp.einsum('blnc,ncd->blnd', ub, wx) + bx).reshape(B, Lp, W)
    log_a = -LRU_C * r.astype(F32) * jax.nn.softplus(-lam.astype(F32))
    a = jnp.exp(log_a)
    b = jnp.sqrt(-jnp.expm1(2.0 * log_a)) * (i * u).astype(F32)
    _, h = lax.associative_scan(_linear_combine, (a, b), axis=1)
    return h.astype(xb.dtype) * jax.nn.gelu(gb)


def hgrn2_mixer(q, fz, v, g, lb):
    B, Lp, _ = q.shape
    nc = Lp // HG_CHUNK
    logf = jnp.logaddexp(jnp.log(lb), jnp.log1p(-lb) + jax.nn.log_sigmoid(fz.astype(F32)))
    k = -jnp.expm1(logf)

    def heads(t, d):
        return t.reshape(B, nc, HG_CHUNK, HG_HEADS, d).transpose(0, 3, 1, 2, 4)

    qh = heads(q.astype(F32), HG_DK)
    kh = heads(k, HG_DK)
    lfh = heads(logf, HG_DK)
    vh = heads(v.astype(F32), HG_DV)
    b = jnp.cumsum(lfh, axis=3)
    b_end = b[:, :, :, -1:, :]
    q_dec = qh * jnp.exp(b)
    k_dec = kh * jnp.exp(-b)
    tri = jnp.tril(jnp.ones((HG_CHUNK, HG_CHUNK), dtype=bool))
    att = jnp.where(tri, jnp.einsum('bhntd,bhnsd->bhnts', q_dec, k_dec), 0.0)
    o_intra = jnp.einsum('bhnts,bhnsv->bhntv', att, vh)
    u_chunk = jnp.einsum('bhnsd,bhnsv->bhndv', kh * jnp.exp(b_end - b), vh)
    dec_chunk = jnp.exp(b_end[:, :, :, 0, :])

    def step(S, inp):
        dec, uc = inp
        return dec[..., None] * S + uc, S

    S0 = jnp.zeros((B, HG_HEADS, HG_DK, HG_DV), F32)
    _, S_prev = lax.scan(step, S0, (jnp.moveaxis(dec_chunk, 2, 0), jnp.moveaxis(u_chunk, 2, 0)))
    o_inter = jnp.einsum('bhntd,nbhdv->bhntv', q_dec, S_prev)
    o = (o_intra + o_inter).transpose(0, 2, 3, 1, 4).reshape(B, Lp, HG_HEADS * HG_DV)
    return o.astype(g.dtype) * jax.nn.silu(g)


def mla_mixer(cq, ckv, kr, pos, cid, gq, w_uq, gkv, w_ukv, gqn, gkn):
    B, Lp, _ = cq.shape
    q = (rmsnorm(cq, gq) @ w_uq).reshape(B, Lp, MLA_HEADS, MLA_NOPE + MLA_ROPE)
    kv = (rmsnorm(ckv, gkv) @ w_ukv).reshape(B, Lp, MLA_HEADS, MLA_NOPE + MLA_V)
    k_nope, v = kv[..., :MLA_NOPE], kv[..., MLA_NOPE:]
    k = jnp.concatenate([k_nope, jnp.broadcast_to(kr[:, :, None, :], (B, Lp, MLA_HEADS, MLA_ROPE))], axis=-1)
    q = rmsnorm(q, gqn)
    k = rmsnorm(k, gkn)
    q = jnp.concatenate([q[..., :MLA_NOPE], rope(q[..., MLA_NOPE:], pos)], axis=-1)
    k = jnp.concatenate([k[..., :MLA_NOPE], rope(k[..., MLA_NOPE:], pos)], axis=-1)

    def chunk_mask(s, q0):
        qc = lax.dynamic_slice_in_dim(cid, q0, Q_BLOCK)
        return jnp.where(cid[None, :] <= qc[:, None], s, -jnp.inf)

    o = blocked_attention(q.transpose(0, 2, 1, 3), k.transpose(0, 2, 1, 3),
                          v.transpose(0, 2, 1, 3), chunk_mask)
    return o.transpose(0, 2, 1, 3).reshape(B, Lp, MLA_HEADS * MLA_V)


def fox_mixer(q, k, v, fz, pos, bf, gqn, gkn):
    B, Lp, _ = q.shape

    def heads(t):
        return t.reshape(B, Lp, FOX_HEADS, FOX_HD)

    qh = rmsnorm(heads(q), gqn).transpose(0, 2, 1, 3)
    kh = rmsnorm(heads(k), gkn).transpose(0, 2, 1, 3)
    vh = heads(v).transpose(0, 2, 1, 3)
    logf = jax.nn.log_sigmoid((fz + bf).astype(F32))
    c = jnp.cumsum(logf, axis=1).transpose(0, 2, 1)

    def decay_mask(s, q0):
        cq = lax.dynamic_slice_in_dim(c, q0, Q_BLOCK, axis=2)
        qp = q0 + jnp.arange(Q_BLOCK)
        bias = cq[..., :, None] - c[..., None, :]
        return jnp.where(pos[None, :] <= qp[:, None], s + bias, -jnp.inf)

    o = blocked_attention(qh, kh, vh, decay_mask)
    return o.transpose(0, 2, 1, 3).reshape(B, Lp, FOX_HEADS * FOX_HD)


def swiglu(u, wg, wu, wd):
    return (jax.nn.silu(u @ wg) * (u @ wu)) @ wd


def moe_swiglu(u, w_router, w_gate, w_up, w_down):
    logits = jnp.einsum('bld,de->ble', u, w_router).astype(F32)
    top_v, top_i = lax.top_k(logits, TOP_K)
    gates = jax.nn.softmax(top_v, axis=-1)
    dense_gate = jnp.sum(jax.nn.one_hot(top_i, N_EXPERTS, dtype=F32) * gates[..., None], axis=-2)
    out = jnp.zeros_like(u)
    for e in range(N_EXPERTS):
        ye = swiglu(u, w_gate[e], w_up[e], w_down[e])
        out = out + dense_gate[..., e:e + 1].astype(u.dtype) * ye
    return out


def setup_inputs(seed: int = 0) -> dict:
    key = jax.random.key(seed)
    it = iter(jax.random.split(key, 40))

    def nrm(shape, scale):
        return jax.random.normal(next(it), shape, F32) * scale

    def gain(shape):
        return 1.0 + nrm(shape, 0.05)

    u = jax.random.uniform(next(it), (DEPTH, LRU_WIDTH), F32, minval=0.9, maxval=0.999)
    a0 = u ** (1.0 / LRU_C)
    lru_lambda = jnp.log(a0) - jnp.log1p(-a0)
    return {
        "x": nrm((BATCH, SEQ, D_MODEL), 1.0),
        "meta": nrm((N_META, D_MODEL), 1.0),
        "norm1_g": gain((DEPTH, D_MODEL)),
        "norm2_g": gain((DEPTH, D_MODEL)),
        "w_in": nrm((DEPTH, D_MODEL, IN_COLS), D_MODEL ** -0.5),
        "w_out": nrm((DEPTH, D_MIX, D_MODEL), D_MIX ** -0.5),
        "out_norm_g": gain((DEPTH, D_MIX)),
        "lru_conv_w": nrm((DEPTH, CONV_WIDTH, LRU_WIDTH), CONV_WIDTH ** -0.5),
        "lru_conv_b": nrm((DEPTH, LRU_WIDTH), 0.02),
        "lru_wa": nrm((DEPTH, LRU_BLOCKS, LRU_BLOCK, LRU_BLOCK), LRU_BLOCK ** -0.5),
        "lru_ba": nrm((DEPTH, LRU_BLOCKS, LRU_BLOCK), 0.02),
        "lru_wx": nrm((DEPTH, LRU_BLOCKS, LRU_BLOCK, LRU_BLOCK), LRU_BLOCK ** -0.5),
        "lru_bx": nrm((DEPTH, LRU_BLOCKS, LRU_BLOCK), 0.02),
        "lru_lambda": lru_lambda,
        "hg_lb_logits": nrm((DEPTH, HG_HEADS * HG_DK), 1.0),
        "mla_gq": gain((DEPTH, MLA_Q_RANK)),
        "mla_w_uq": nrm((DEPTH, MLA_Q_RANK, MLA_HEADS * (MLA_NOPE + MLA_ROPE)), MLA_Q_RANK ** -0.5),
        "mla_gkv": gain((DEPTH, MLA_KV_RANK)),
        "mla_w_ukv": nrm((DEPTH, MLA_KV_RANK, MLA_HEADS * (MLA_NOPE + MLA_V)), MLA_KV_RANK ** -0.5),
        "mla_gqn": gain((DEPTH, MLA_NOPE + MLA_ROPE)),
        "mla_gkn": gain((DEPTH, MLA_NOPE + MLA_ROPE)),
        "fox_gqn": gain((DEPTH, FOX_HD)),
        "fox_gkn": gain((DEPTH, FOX_HD)),
        "fox_bf": 2.0 + nrm((DEPTH, FOX_HEADS), 0.5),
        "ffn_w_gate": nrm((N_DENSE, D_MODEL, D_FF), D_MODEL ** -0.5),
        "ffn_w_up": nrm((N_DENSE, D_MODEL, D_FF), D_MODEL ** -0.5),
        "ffn_w_down": nrm((N_DENSE, D_FF, D_MODEL), D_FF ** -0.5),
        "moe_w_router": nrm((N_MOE, D_MODEL, N_EXPERTS), D_MODEL ** -0.5),
        "moe_w_gate": nrm((N_MOE, N_EXPERTS, D_MODEL, D_FF_EXPERT), D_MODEL ** -0.5),
        "moe_w_up": nrm((N_MOE, N_EXPERTS, D_MODEL, D_FF_EXPERT), D_MODEL ** -0.5),
        "moe_w_down": nrm((N_MOE, N_EXPERTS, D_FF_EXPERT, D_MODEL), D_FF_EXPERT ** -0.5),
    }


def reference(x, meta, norm1_g, norm2_g, w_in, w_out, out_norm_g,
              lru_conv_w, lru_conv_b, lru_wa, lru_ba, lru_wx, lru_bx, lru_lambda,
              hg_lb_logits,
              mla_gq, mla_w_uq, mla_gkv, mla_w_ukv, mla_gqn, mla_gkn,
              fox_gqn, fox_gkn, fox_bf,
              ffn_w_gate, ffn_w_up, ffn_w_down,
              moe_w_router, moe_w_gate, moe_w_up, moe_w_down):
    B = x.shape[0]
    L = N_META + x.shape[1]
    Lp = -(-L // Q_BLOCK) * Q_BLOCK
    h = jnp.concatenate([jnp.broadcast_to(meta[None].astype(x.dtype), (B, N_META, D_MODEL)), x], axis=1)
    h = jnp.pad(h, ((0, 0), (0, Lp - L), (0, 0)))
    pos = jnp.arange(Lp, dtype=jnp.int32)
    cid = chunk_ids(pos)
    lb_cum = jnp.cumsum(jax.nn.softmax(hg_lb_logits.astype(F32), axis=0), axis=0)
    split_at = np.cumsum(IN_SIZES)[:-1].tolist()
    for l in range(DEPTH):
        u = rmsnorm(h, norm1_g[l])
        z = u @ w_in[l]
        (xa, ga, hq, hf, hi, hg, cq, ckv, kr, fq, fk, fv, ff) = jnp.split(z, split_at, axis=-1)
        ya = rglru_mixer(xa, ga, lru_conv_w[l], lru_conv_b[l], lru_wa[l], lru_ba[l],
                         lru_wx[l], lru_bx[l], lru_lambda[l])
        yb = hgrn2_mixer(hq, hf, hi, hg, lb_cum[l] - lb_cum[0])
        yc = mla_mixer(cq, ckv, kr, pos, cid, mla_gq[l], mla_w_uq[l], mla_gkv[l], mla_w_ukv[l],
                       mla_gqn[l], mla_gkn[l])
        yd = fox_mixer(fq, fk, fv, ff, pos, fox_bf[l], fox_gqn[l], fox_gkn[l])
        y = group_rmsnorm(jnp.concatenate([ya, yb, yc, yd], axis=-1), out_norm_g[l])
        h = h + y @ w_out[l]
        u = rmsnorm(h, norm2_g[l])
        if l % 2 == 0:
            h = h + swiglu(u, ffn_w_gate[l // 2], ffn_w_up[l // 2], ffn_w_down[l // 2])
        else:
            h = h + moe_swiglu(u, moe_w_router[l // 2], moe_w_gate[l // 2],
                               moe_w_up[l // 2], moe_w_down[l // 2])
    return h[:, N_META:L]
```

```python
import functools
import math

import jax
import jax.numpy as jnp
from jax import lax
from jax.experimental import pallas as pl
from jax.experimental.pallas import tpu as pltpu

F32 = jnp.float32
BF16 = jnp.bfloat16

D_MODEL = 1024
N_META = 16
CHUNK = 64
SEQ_ALIGN = 128
EPS = 1e-6
GROUP_WIDTH = 256
N_HEADS = 4
HEAD_PAD = 128
LRU_C = 8.0
HG_CHUNK = 16
HG_BLOCK = 128
MLA_NOPE, MLA_ROPE, MLA_V = 64, 32, 64
MLA_QK = MLA_NOPE + MLA_ROPE
MLA_Q_RANK, MLA_KV_RANK = 192, 128
ROPE_THETA = 10000.0
FOX_HD = 64
N_EXPERTS = 8
LOG2E = 1.4426950408889634
ONES_LANE = 64
VMEM_LIMIT = 56 * 1024 * 1024

NT_DIMS = (((1,), (1,)), ((), ()))
TN_DIMS = (((0,), (0,)), ((), ()))


def _cparams(*sem):
    return pltpu.CompilerParams(dimension_semantics=sem, vmem_limit_bytes=VMEM_LIMIT)


def _tile(n, pref, align=SEQ_ALIGN):
    best = None
    for t in range(align, min(n, pref) + 1, align):
        if n % t == 0:
            best = t
    assert best is not None, (n, pref, align)
    return best


def _rms(x, width):
    return lax.rsqrt(jnp.sum(x * x, axis=-1, keepdims=True) * (1.0 / width) + EPS)


def _sigmoid(x):
    return 1.0 / (1.0 + jnp.exp(-x))


def _log_sigmoid(x):
    return jnp.minimum(x, 0.0) - jnp.log(1.0 + jnp.exp(-jnp.abs(x)))


def _full(shape):
    return pl.BlockSpec(shape, lambda *_: (0,) * len(shape))


def _inproj_kernel(h_ref, g_ref, wl_ref, wh_ref, wm_ref, wf_ref, zl_ref, zh_ref, zm_ref, zf_ref):
    x = h_ref[...]
    u = (x * _rms(x, D_MODEL) * g_ref[...]).astype(BF16)
    zl_ref[...] = jnp.dot(u, wl_ref[...], preferred_element_type=F32)
    zh_ref[...] = jnp.dot(u, wh_ref[...], preferred_element_type=F32)
    zm_ref[...] = jnp.dot(u, wm_ref[...], preferred_element_type=F32)
    zf_ref[...] = jnp.dot(u, wf_ref[...], preferred_element_type=F32)


def _inproj(h, g, wl, wh, wm, wf, tm):
    R = h.shape[0]
    row = lambda n: pl.BlockSpec((tm, n), lambda i: (i, 0))
    return pl.pallas_call(
        _inproj_kernel,
        grid=(R // tm,),
        in_specs=[row(D_MODEL), _full((1, D_MODEL)), _full(wl.shape), _full(wh.shape),
                  _full(wm.shape), _full(wf.shape)],
        out_specs=[row(wl.shape[1]), row(wh.shape[1]), row(wm.shape[1]), row(wf.shape[1])],
        out_shape=[jax.ShapeDtypeStruct((R, w.shape[1]), F32) for w in (wl, wh, wm, wf)],
        compiler_params=_cparams("parallel"),
        name="inproj",
    )(h, g, wl, wh, wm, wf)


def _rglru_kernel(z_ref, cw_ref, cb_ref, wa_ref, ba_ref, wx_ref, bx_ref, lam_ref, gn_ref, y_ref,
                  xbuf, a_s, b_s, h_s, hst):
    W = GROUP_WIDTH
    tt = y_ref.shape[0]

    @pl.when(pl.program_id(1) == 0)
    def _():
        xbuf[0:8, :] = jnp.zeros((8, W), F32)
        hst[...] = jnp.zeros((1, W), F32)

    xa = z_ref[:, 0:W]
    ga = z_ref[:, W:2 * W]
    xbuf[8:8 + tt, :] = xa
    u = (cb_ref[...] + xbuf[5:5 + tt, :] * cw_ref[0:1, :] + xbuf[6:6 + tt, :] * cw_ref[1:2, :]
         + xbuf[7:7 + tt, :] * cw_ref[2:3, :] + xa * cw_ref[3:4, :])
    xbuf[0:8, :] = xbuf[tt:tt + 8, :]

    ub = u.astype(BF16)
    r = _sigmoid(jnp.dot(ub, wa_ref[...], preferred_element_type=F32) + ba_ref[...])
    i = _sigmoid(jnp.dot(ub, wx_ref[...], preferred_element_type=F32) + bx_ref[...])
    lam = lam_ref[...]
    softplus_neg_lam = jnp.maximum(-lam, 0.0) + jnp.log(1.0 + jnp.exp(-jnp.abs(lam)))
    a = jnp.exp((-LRU_C) * r * softplus_neg_lam)
    b = jnp.sqrt(1.0 - a * a) * (i * u)

    row = lax.broadcasted_iota(jnp.int32, (tt, W), 0) & 7
    for s in (1, 2, 4):
        ok = row >= s
        b = jnp.where(ok, a * pltpu.roll(b, s, 0) + b, b)
        a = jnp.where(ok, a * pltpu.roll(a, s, 0), a)
    a_s[...] = a
    b_s[...] = b

    def group(gi, hprev):
        sl = pl.ds(pl.multiple_of(gi * 8, 8), 8)
        hg = b_s[sl, :] + a_s[sl, :] * hprev
        h_s[sl, :] = hg
        return hg[7:8, :]

    hst[...] = lax.fori_loop(0, tt // 8, group, hst[...], unroll=8)

    c0 = math.sqrt(2.0 / math.pi)
    gelu = 0.5 * ga * (1.0 + jnp.tanh(c0 * (ga + 0.044715 * (ga * ga * ga))))
    y = h_s[...] * gelu
    y_ref[...] = (y * _rms(y, W) * gn_ref[...]).astype(BF16)


def _rglru(zl, cw, cb, wa, ba, wx, bx, lam, gn, B, tt):
    R = zl.shape[0]
    nt = R // B // tt
    W = GROUP_WIDTH
    return pl.pallas_call(
        _rglru_kernel,
        grid=(B, nt),
        in_specs=[pl.BlockSpec((tt, 2 * W), lambda b, t: (b * nt + t, 0)),
                  _full((4, W)), _full((1, W)), _full((W, W)), _full((1, W)), _full((W, W)),
                  _full((1, W)), _full((1, W)), _full((1, W))],
        out_specs=pl.BlockSpec((tt, W), lambda b, t: (b * nt + t, 0)),
        out_shape=jax.ShapeDtypeStruct((R, W), BF16),
        scratch_shapes=[pltpu.VMEM((tt + 8, W), F32), pltpu.VMEM((tt, W), F32),
                        pltpu.VMEM((tt, W), F32), pltpu.VMEM((tt, W), F32), pltpu.VMEM((1, W), F32)],
        compiler_params=_cparams("parallel", "arbitrary"),
        name="rglru",
    )(zl, cw, cb, wa, ba, wx, bx, lam, gn)


def _hgrn2_kernel(z_ref, lb_ref, gn_ref, y_ref, qd_s, ke_s, v_s, dec_s, o_s, st_s):
    W = GROUP_WIDTH
    tt = y_ref.shape[0]
    C = HG_CHUNK

    @pl.when(pl.program_id(1) == 0)
    def _():
        st_s[...] = jnp.zeros((W, W), F32)

    q = z_ref[:, 0:W]
    fz = z_ref[:, W:2 * W]
    v = z_ref[:, 2 * W:3 * W]
    g = z_ref[:, 3 * W:4 * W]
    lb = lb_ref[...]
    la = jnp.log(lb)
    lq = jnp.log(1.0 - lb) + _log_sigmoid(fz)
    logf = jnp.maximum(la, lq) + jnp.log(1.0 + jnp.exp(-jnp.abs(la - lq)))
    kin = 1.0 - jnp.exp(logf)

    rowc = lax.broadcasted_iota(jnp.int32, (tt, W), 0) & (C - 1)
    b = logf
    rev = logf
    s = 1
    while s < C:
        b = b + jnp.where(rowc >= s, pltpu.roll(b, s, 0), 0.0)
        rev = rev + jnp.where(rowc + s < C, pltpu.roll(rev, tt - s, 0), 0.0)
        s *= 2
    tail = rev - logf
    qd = q * jnp.exp(b)
    kd = (kin * jnp.exp(-b)).astype(BF16)
    vb = v.astype(BF16)
    qd_s[...] = qd.astype(BF16)
    ke_s[...] = (kin * jnp.exp(tail)).astype(BF16)
    v_s[...] = vb
    dec_s[...] = jnp.exp(b + tail)

    lane_head = lax.broadcasted_iota(jnp.int32, (1, W), 1) // (W // N_HEADS)

    rr = lax.broadcasted_iota(jnp.int32, (HG_BLOCK, HG_BLOCK), 0)
    cc = lax.broadcasted_iota(jnp.int32, (HG_BLOCK, HG_BLOCK), 1)
    amask = (rr // C == cc // C) & (cc <= rr)
    for jb in range(tt // HG_BLOCK):
        sl = slice(jb * HG_BLOCK, (jb + 1) * HG_BLOCK)
        qb, kb, vv = qd[sl], kd[sl], vb[sl]
        acc = jnp.zeros((HG_BLOCK, W), F32)
        for hh in range(N_HEADS):
            hm = lane_head == hh
            qh = jnp.where(hm, qb, 0.0).astype(BF16)
            att = lax.dot_general(qh, kb, NT_DIMS, preferred_element_type=F32)
            att = jnp.where(amask, att, 0.0).astype(BF16)
            acc = jnp.where(hm, jnp.dot(att, vv, preferred_element_type=F32), acc)
        o_s[sl, :] = acc

    def chunk(c, carry):
        sl = pl.ds(pl.multiple_of(c * C, C), C)
        qc = qd_s[sl, :]
        st = st_s[...]
        qbd = jnp.concatenate([jnp.where(lane_head == hh, qc, jnp.zeros_like(qc))
                               for hh in range(N_HEADS)], axis=0)
        res = lax.dot_general(qbd, st.astype(BF16), NT_DIMS, preferred_element_type=F32)
        oi = jnp.zeros((C, W), F32)
        for hh in range(N_HEADS):
            oi = jnp.where(lane_head == hh, res[hh * C:(hh + 1) * C, :], oi)
        o_s[sl, :] = o_s[sl, :] + oi
        upd = lax.dot_general(v_s[sl, :], ke_s[sl, :], TN_DIMS, preferred_element_type=F32)
        st_s[...] = st * dec_s[pl.ds(c * C, 1), :] + upd
        return carry

    lax.fori_loop(0, tt // C, chunk, 0)

    y = o_s[...] * (g * _sigmoid(g))
    y_ref[...] = (y * _rms(y, W) * gn_ref[...]).astype(BF16)


def _hgrn2(zh, lb, gn, B, tt):
    R = zh.shape[0]
    nt = R // B // tt
    W = GROUP_WIDTH
    return pl.pallas_call(
        _hgrn2_kernel,
        grid=(B, nt),
        in_specs=[pl.BlockSpec((tt, 4 * W), lambda b, t: (b * nt + t, 0)), _full((1, W)), _full((1, W))],
        out_specs=pl.BlockSpec((tt, W), lambda b, t: (b * nt + t, 0)),
        out_shape=jax.ShapeDtypeStruct((R, W), BF16),
        scratch_shapes=[pltpu.VMEM((tt, W), BF16), pltpu.VMEM((tt, W), BF16), pltpu.VMEM((tt, W), BF16),
                        pltpu.VMEM((tt, W), F32), pltpu.VMEM((tt, W), F32), pltpu.VMEM((W, W), F32)],
        compiler_params=_cparams("parallel", "arbitrary"),
        name="hgrn2",
    )(zh, lb, gn)


def _rope(x, cos, s_lo, s_hi):
    return x * cos + pltpu.roll(x, 16, 1) * s_hi + pltpu.roll(x, HEAD_PAD - 16, 1) * s_lo


def _mla_prep_kernel(z_ref, gq_ref, wuq_ref, gkv_ref, wuk_ref, wuv_ref, gqn_ref, gkn_ref,
                     cos_ref, slo_ref, shi_ref, q_ref, k_ref, v_ref):
    P = HEAD_PAD
    ckv = z_ref[:, 0:P]
    krb = z_ref[:, P:2 * P]
    cq = z_ref[:, 2 * P:4 * P]
    qn = (cq * _rms(cq, MLA_Q_RANK) * gq_ref[...]).astype(BF16)
    kvn = (ckv * _rms(ckv, MLA_KV_RANK) * gkv_ref[...]).astype(BF16)
    q = jnp.dot(qn, wuq_ref[...], preferred_element_type=F32)
    kn = jnp.dot(kvn, wuk_ref[...], preferred_element_type=F32)
    vv = jnp.dot(kvn, wuv_ref[...], preferred_element_type=F32)
    cos, slo, shi = cos_ref[...], slo_ref[...], shi_ref[...]
    lane = lax.broadcasted_iota(jnp.int32, (1, P), 1)
    qscale = (MLA_QK ** -0.5) * LOG2E
    for hh in range(N_HEADS):
        sl = slice(hh * P, (hh + 1) * P)
        qh = q[:, sl]
        qh = qh * _rms(qh, MLA_QK) * gqn_ref[...]
        q_ref[:, sl] = (_rope(qh, cos, slo, shi) * qscale).astype(BF16)
        kh = kn[:, sl] + krb
        kh = kh * _rms(kh, MLA_QK) * gkn_ref[...]
        k_ref[:, sl] = _rope(kh, cos, slo, shi).astype(BF16)
        v_ref[:, sl] = jnp.where(lane == ONES_LANE, 1.0, vv[:, sl]).astype(BF16)


def _mla_prep(zm, gq, wuq, gkv, wuk, wuv, gqn, gkn, cos, slo, shi, tm):
    R = zm.shape[0]
    P = HEAD_PAD
    nt = cos.shape[0] // tm
    row = lambda n: pl.BlockSpec((tm, n), lambda i: (i, 0))
    tab = pl.BlockSpec((tm, P), lambda i: (i % nt, 0))
    out = jax.ShapeDtypeStruct((R, N_HEADS * P), BF16)
    return pl.pallas_call(
        _mla_prep_kernel,
        grid=(R // tm,),
        in_specs=[row(4 * P), _full((1, 2 * P)), _full(wuq.shape), _full((1, P)), _full(wuk.shape),
                  _full(wuv.shape), _full((1, P)), _full((1, P)), tab, tab, tab],
        out_specs=[row(N_HEADS * P)] * 3,
        out_shape=[out, out, out],
        compiler_params=_cparams("parallel"),
        name="mla_prep",
    )(zm, gq, wuq, gkv, wuk, wuv, gqn, gkn, cos, slo, shi)


def _fox_prep_kernel(z_ref, bf_ref, gqn_ref, gkn_ref, q_ref, k_ref, v_ref, carry):
    P = HEAD_PAD
    tt = q_ref.shape[0]

    @pl.when(pl.program_id(1) == 0)
    def _():
        carry[...] = jnp.zeros((1, P), F32)

    c = _log_sigmoid(z_ref[:, 3 * N_HEADS * P:] + bf_ref[...])
    row = lax.broadcasted_iota(jnp.int32, (tt, P), 0)
    s = 1
    while s < tt:
        c = c + jnp.where(row >= s, pltpu.roll(c, s, 0), 0.0)
        s *= 2
    c = c + carry[...]
    carry[...] = c[tt - 1:tt, :]
    c = c * LOG2E
    c1 = c.astype(BF16).astype(F32)
    c2 = (c - c1).astype(BF16).astype(F32)
    c3 = c - c1 - c2

    lane = lax.broadcasted_iota(jnp.int32, (1, P), 1)
    qscale = (FOX_HD ** -0.5) * LOG2E
    for hh in range(N_HEADS):
        p1 = c1[:, hh:hh + 1]
        p2 = c2[:, hh:hh + 1]
        p3 = c3[:, hh:hh + 1]
        qh = z_ref[:, hh * P:(hh + 1) * P]
        qh = qh * _rms(qh, FOX_HD) * gqn_ref[...] * qscale
        qh = jnp.where(lane == 64, p1, jnp.where(lane == 65, p2, jnp.where(lane == 66, p3, qh)))
        qh = jnp.where((lane >= 67) & (lane < 70), 1.0, qh)
        q_ref[:, hh * P:(hh + 1) * P] = qh.astype(BF16)
        kh = z_ref[:, (N_HEADS + hh) * P:(N_HEADS + hh + 1) * P]
        kh = kh * _rms(kh, FOX_HD) * gkn_ref[...]
        kh = jnp.where(lane == 67, -p1, jnp.where(lane == 68, -p2, jnp.where(lane == 69, -p3, kh)))
        kh = jnp.where((lane >= 64) & (lane < 67), 1.0, kh)
        k_ref[:, hh * P:(hh + 1) * P] = kh.astype(BF16)
        vh = z_ref[:, (2 * N_HEADS + hh) * P:(2 * N_HEADS + hh + 1) * P]
        v_ref[:, hh * P:(hh + 1) * P] = jnp.where(lane == ONES_LANE, 1.0, vh).astype(BF16)


def _fox_prep(zf, bf, gqn, gkn, B, tt):
    R = zf.shape[0]
    nt = R // B // tt
    P = HEAD_PAD
    blk = lambda n: pl.BlockSpec((tt, n), lambda b, t: (b * nt + t, 0))
    out = jax.ShapeDtypeStruct((R, N_HEADS * P), BF16)
    return pl.pallas_call(
        _fox_prep_kernel,
        grid=(B, nt),
        in_specs=[blk(zf.shape[1]), _full((1, P)), _full((1, P)), _full((1, P))],
        out_specs=[blk(N_HEADS * P)] * 3,
        out_shape=[out, out, out],
        scratch_shapes=[pltpu.VMEM((1, P), F32)],
        compiler_params=_cparams("parallel", "arbitrary"),
        name="fox_prep",
    )(zf, bf, gqn, gkn)


def _attn_kernel(q_ref, k_ref, v_ref, gn_ref, y_ref, m_s, acc_s, o_s, *, chunk_causal, tk):
    P = HEAD_PAD
    tq = q_ref.shape[0]
    lp = k_ref.shape[0]
    q0 = pl.program_id(1) * tq
    m_s[...] = jnp.full(m_s.shape, -jnp.inf, F32)
    acc_s[...] = jnp.zeros(acc_s.shape, F32)

    qpos = q0 + lax.broadcasted_iota(jnp.int32, (tq, 1), 0)
    if chunk_causal:
        qlim = N_META + CHUNK * ((qpos + (CHUNK - N_META)) // CHUNK)
        reach = N_META
    else:
        qlim = qpos + 1
        reach = 0
    n_full = (q0 + reach) // tk
    n_all = (jnp.minimum(q0 + tq + reach, lp) + tk - 1) // tk

    def chunk(j, masked):
        ks = pl.ds(pl.multiple_of(j * tk, tk), tk)
        if masked:
            vis = (j * tk + lax.broadcasted_iota(jnp.int32, (1, tk), 1)) < qlim
        for hh in range(N_HEADS):
            hs = slice(hh * P, (hh + 1) * P)
            s = lax.dot_general(q_ref[:, hs], k_ref[ks, hs], NT_DIMS, preferred_element_type=F32)
            if masked:
                s = jnp.where(vis, s, -jnp.inf)
            m_old = m_s[hh]
            m_new = jnp.maximum(m_old, jnp.max(s, axis=-1, keepdims=True))
            p = jnp.exp2(s - m_new).astype(BF16)
            acc_s[hh] = jnp.exp2(m_old - m_new) * acc_s[hh] + jnp.dot(
                p, v_ref[ks, hs], preferred_element_type=F32)
            m_s[hh] = m_new

    def full_body(j, carry):
        chunk(j, False)
        return carry

    def masked_body(j, carry):
        chunk(j, True)
        return carry

    lax.fori_loop(0, n_full, full_body, 0)
    lax.fori_loop(n_full, n_all, masked_body, 0)

    hd = GROUP_WIDTH // N_HEADS
    for hh in range(N_HEADS):
        acc = acc_s[hh]
        o_s[:, hh * hd:(hh + 1) * hd] = acc[:, 0:hd] / acc[:, ONES_LANE:ONES_LANE + 1]
    y = o_s[...]
    y_ref[...] = (y * _rms(y, GROUP_WIDTH) * gn_ref[...]).astype(BF16)


def _attention(q, k, v, gn, B, tq, tk, chunk_causal):
    R = q.shape[0]
    lp = R // B
    nq = lp // tq
    P = HEAD_PAD
    W = GROUP_WIDTH
    kv_spec = pl.BlockSpec((lp, N_HEADS * P), lambda b, i: (b, 0))
    return pl.pallas_call(
        functools.partial(_attn_kernel, chunk_causal=chunk_causal, tk=tk),
        grid=(B, nq),
        in_specs=[pl.BlockSpec((tq, N_HEADS * P), lambda b, i: (b * nq + i, 0)), kv_spec, kv_spec,
                  _full((1, W))],
        out_specs=pl.BlockSpec((tq, W), lambda b, i: (b * nq + i, 0)),
        out_shape=jax.ShapeDtypeStruct((R, W), BF16),
        scratch_shapes=[pltpu.VMEM((N_HEADS, tq, 1), F32), pltpu.VMEM((N_HEADS, tq, P), F32),
                        pltpu.VMEM((tq, W), F32)],
        compiler_params=_cparams("parallel", "arbitrary"),
        name="mla_attn" if chunk_causal else "fox_attn",
    )(q, k, v, gn)


def _outproj_kernel(h_ref, ya_ref, yb_ref, yc_ref, yd_ref, w_ref, g_ref, *rest, with_router):
    W = GROUP_WIDTH
    if with_router:
        wr_ref, hn_ref, u_ref, dg_ref = rest
    else:
        hn_ref, u_ref = rest
    acc = h_ref[...]
    for gi, y_ref in enumerate((ya_ref, yb_ref, yc_ref, yd_ref)):
        acc = acc + jnp.dot(y_ref[...], w_ref[gi * W:(gi + 1) * W, :], preferred_element_type=F32)
    hn_ref[...] = acc
    u = acc * _rms(acc, D_MODEL) * g_ref[...]
    u_ref[...] = u.astype(BF16)
    if with_router:
        logits = jnp.dot(u, wr_ref[...], preferred_element_type=F32, precision=lax.Precision.HIGHEST)
        lane = lax.broadcasted_iota(jnp.int32, logits.shape, 1).astype(F32)
        lg = jnp.where(lane < N_EXPERTS, logits, -jnp.inf)
        m1 = jnp.max(lg, axis=-1, keepdims=True)
        i1 = jnp.min(jnp.where(lg == m1, lane, 1e9), axis=-1, keepdims=True)
        lg2 = jnp.where(lane == i1, -jnp.inf, lg)
        m2 = jnp.max(lg2, axis=-1, keepdims=True)
        i2 = jnp.min(jnp.where(lg2 == m2, lane, 1e9), axis=-1, keepdims=True)
        e = jnp.exp(m2 - m1)
        g1 = 1.0 / (1.0 + e)
        dg_ref[...] = jnp.where(lane == i1, g1, 0.0) + jnp.where(lane == i2, e * g1, 0.0)


def _outproj(h, ya, yb, yc, yd, w, g, wr, tm):
    R = h.shape[0]
    W = GROUP_WIDTH
    with_router = wr is not None
    row = lambda n: pl.BlockSpec((tm, n), lambda i: (i, 0))
    in_specs = [row(D_MODEL), row(W), row(W), row(W), row(W), _full(w.shape), _full((1, D_MODEL))]
    out_specs = [row(D_MODEL), row(D_MODEL)]
    out_shape = [jax.ShapeDtypeStruct((R, D_MODEL), F32), jax.ShapeDtypeStruct((R, D_MODEL), BF16)]
    args = [h, ya, yb, yc, yd, w, g]
    if with_router:
        in_specs.append(_full(wr.shape))
        out_specs.append(row(wr.shape[1]))
        out_shape.append(jax.ShapeDtypeStruct((R, wr.shape[1]), F32))
        args.append(wr)
    return pl.pallas_call(
        functools.partial(_outproj_kernel, with_router=with_router),
        grid=(R // tm,),
        in_specs=in_specs, out_specs=out_specs, out_shape=out_shape,
        compiler_params=_cparams("parallel"),
        name="outproj_router" if with_router else "outproj",
    )(*args)


def _ffn_kernel(u_ref, h_ref, wg_ref, wu_ref, wd_ref, o_ref):
    @pl.when(pl.program_id(1) == 0)
    def _():
        o_ref[...] = h_ref[...]

    u = u_ref[...]
    a = jnp.dot(u, wg_ref[...], preferred_element_type=F32)
    b = jnp.dot(u, wu_ref[...], preferred_element_type=F32)
    hid = (a * _sigmoid(a) * b).astype(BF16)
    o_ref[...] += jnp.dot(hid, wd_ref[...], preferred_element_type=F32)


def _ffn(u, h, wg, wu, wd, tm, tf):
    R = h.shape[0]
    dff = wg.shape[1]
    return pl.pallas_call(
        _ffn_kernel,
        grid=(R // tm, dff // tf),
        in_specs=[pl.BlockSpec((tm, D_MODEL), lambda i, f: (i, 0)),
                  pl.BlockSpec((tm, D_MODEL), lambda i, f: (i, 0)),
                  pl.BlockSpec((D_MODEL, tf), lambda i, f: (0, f)),
                  pl.BlockSpec((D_MODEL, tf), lambda i, f: (0, f)),
                  pl.BlockSpec((tf, D_MODEL), lambda i, f: (f, 0))],
        out_specs=pl.BlockSpec((tm, D_MODEL), lambda i, f: (i, 0)),
        out_shape=jax.ShapeDtypeStruct((R, D_MODEL), F32),
        compiler_params=_cparams("parallel", "arbitrary"),
        name="ffn",
    )(u, h, wg, wu, wd)


def _moe_kernel(u_ref, h_ref, dg_ref, wg_ref, wu_ref, wd_ref, o_ref):
    e = pl.program_id(1)

    @pl.when((e == 0) & (pl.program_id(2) == 0))
    def _():
        o_ref[...] = h_ref[...]

    dg = dg_ref[...]
    lane = lax.broadcasted_iota(jnp.int32, dg.shape, 1)
    gate = jnp.sum(jnp.where(lane == e, dg, 0.0), axis=-1, keepdims=True)
    u = u_ref[...]
    a = jnp.dot(u, wg_ref[...], preferred_element_type=F32)
    b = jnp.dot(u, wu_ref[...], preferred_element_type=F32)
    hid = (a * _sigmoid(a) * b * gate).astype(BF16)
    o_ref[...] += jnp.dot(hid, wd_ref[...], preferred_element_type=F32)


def _moe(u, h, dg, wg, wu, wd, tm, tf):
    R = h.shape[0]
    ne, _, dff = wg.shape
    return pl.pallas_call(
        _moe_kernel,
        grid=(R // tm, ne, dff // tf),
        in_specs=[pl.BlockSpec((tm, D_MODEL), lambda i, e, f: (i, 0)),
                  pl.BlockSpec((tm, D_MODEL), lambda i, e, f: (i, 0)),
                  pl.BlockSpec((tm, dg.shape[1]), lambda i, e, f: (i, 0)),
                  pl.BlockSpec((None, D_MODEL, tf), lambda i, e, f: (e, 0, f)),
                  pl.BlockSpec((None, D_MODEL, tf), lambda i, e, f: (e, 0, f)),
                  pl.BlockSpec((None, tf, D_MODEL), lambda i, e, f: (e, f, 0))],
        out_specs=pl.BlockSpec((tm, D_MODEL), lambda i, e, f: (i, 0)),
        out_shape=jax.ShapeDtypeStruct((R, D_MODEL), F32),
        compiler_params=_cparams("parallel", "arbitrary", "arbitrary"),
        name="moe",
    )(u, h, dg, wg, wu, wd)


def _pad_heads(w, real):
    rows = w.shape[0]
    w = w.reshape(rows, N_HEADS, real)
    return jnp.pad(w, ((0, 0), (0, 0), (0, HEAD_PAD - real))).reshape(rows, N_HEADS * HEAD_PAD)


def _pad_cols(w, n):
    return jnp.pad(w, ((0, 0), (0, n - w.shape[1])))


def _row(v, n=None):
    v = v.reshape(1, -1).astype(F32)
    return v if n is None else _pad_cols(v, n)


def _block_diag(w):
    nb, c, d = w.shape
    eye = jnp.eye(nb, dtype=w.dtype)
    return (eye[:, None, :, None] * w[:, :, None, :]).reshape(nb * c, nb * d)


def _rope_tables(lp):
    half = MLA_ROPE // 2
    inv = ROPE_THETA ** (-jnp.arange(half, dtype=F32) / half)
    ang = jnp.arange(lp, dtype=jnp.int32).astype(F32)[:, None] * inv[None, :]
    cos, sin = jnp.cos(ang), jnp.sin(ang)
    one = jnp.ones((lp, MLA_NOPE), F32)
    zero = jnp.zeros((lp, MLA_NOPE), F32)
    zh = jnp.zeros((lp, half), F32)
    tail1 = jnp.ones((lp, HEAD_PAD - MLA_QK), F32)
    tail0 = jnp.zeros((lp, HEAD_PAD - MLA_QK), F32)
    cos_t = jnp.concatenate([one, cos, cos, tail1], axis=1)
    s_lo = jnp.concatenate([zero, -sin, zh, tail0], axis=1)
    s_hi = jnp.concatenate([zero, zh, sin, tail0], axis=1)
    return cos_t, s_lo, s_hi


def kernel(x, meta, norm1_g, norm2_g, w_in, w_out, out_norm_g, lru_conv_w, lru_conv_b, lru_wa, lru_ba, lru_wx, lru_bx, lru_lambda, hg_lb_logits, mla_gq, mla_w_uq, mla_gkv, mla_w_ukv, mla_gqn, mla_gkn, fox_gqn, fox_gkn, fox_bf, ffn_w_gate, ffn_w_up, ffn_w_down, moe_w_router, moe_w_gate, moe_w_up, moe_w_down):
    B, S, _ = x.shape
    depth = w_in.shape[0]
    L = N_META + S
    lp = -(-L // SEQ_ALIGN) * SEQ_ALIGN
    R = B * lp
    W = GROUP_WIDTH
    P = HEAD_PAD

    tt = _tile(lp, 640)
    tq = _tile(lp, 640)
    tk = _tile(lp, 640)
    tm_ffn = _tile(R, 1280)

    h = jnp.concatenate([jnp.broadcast_to(meta[None].astype(x.dtype), (B, N_META, D_MODEL)), x,
                         jnp.zeros((B, lp - L, D_MODEL), x.dtype)], axis=1).reshape(R, D_MODEL)
    cos_t, s_lo, s_hi = _rope_tables(lp)
    lb_cum = jnp.cumsum(jax.nn.softmax(hg_lb_logits.astype(F32), axis=0), axis=0)

    o = 0
    offs = []
    for n in (W, W, W, W, W, W, MLA_Q_RANK, MLA_KV_RANK, MLA_ROPE, W, W, W, N_HEADS):
        offs.append(o)
        o += n
    (o_xa, _, o_hq, _, _, _, o_cq, o_ckv, o_kr, o_fq, o_fk, o_fv, o_ff) = offs

    for l in range(depth):
        w = w_in[l]
        wl = w[:, o_xa:o_xa + 2 * W].astype(BF16)
        wh = w[:, o_hq:o_hq + 4 * W].astype(BF16)
        wm = jnp.concatenate([
            w[:, o_ckv:o_ckv + MLA_KV_RANK],
            jnp.zeros((D_MODEL, MLA_NOPE), F32), w[:, o_kr:o_kr + MLA_ROPE],
            jnp.zeros((D_MODEL, P - MLA_QK), F32),
            _pad_cols(w[:, o_cq:o_cq + MLA_Q_RANK], 2 * P)], axis=1).astype(BF16)
        wf = jnp.concatenate([
            _pad_heads(w[:, o_fq:o_fq + W], FOX_HD), _pad_heads(w[:, o_fk:o_fk + W], FOX_HD),
            _pad_heads(w[:, o_fv:o_fv + W], FOX_HD), _pad_cols(w[:, o_ff:o_ff + N_HEADS], P)],
            axis=1).astype(BF16)
        zl, zh, zm, zf = _inproj(h, _row(norm1_g[l]), wl, wh, wm, wf, tt)

        gn = out_norm_g[l].astype(F32)
        ya = _rglru(zl, lru_conv_w[l].astype(F32), _row(lru_conv_b[l]),
                    _block_diag(lru_wa[l]).astype(BF16), _row(lru_ba[l]),
                    _block_diag(lru_wx[l]).astype(BF16), _row(lru_bx[l]),
                    _row(lru_lambda[l]), _row(gn[0:W]), B, tt)
        yb = _hgrn2(zh, _row(lb_cum[l] - lb_cum[0]), _row(gn[W:2 * W]), B, tt)

        wuq = jnp.pad(_pad_heads(mla_w_uq[l], MLA_QK), ((0, 2 * P - MLA_Q_RANK), (0, 0))).astype(BF16)
        wukv = mla_w_ukv[l].reshape(MLA_KV_RANK, N_HEADS, MLA_NOPE + MLA_V)
        wuk = _pad_heads(wukv[:, :, :MLA_NOPE].reshape(MLA_KV_RANK, -1), MLA_NOPE).astype(BF16)
        wuv = _pad_heads(wukv[:, :, MLA_NOPE:].reshape(MLA_KV_RANK, -1), MLA_V).astype(BF16)
        q, k, v = _mla_prep(zm, _row(mla_gq[l], 2 * P), wuq, _row(mla_gkv[l]), wuk, wuv,
                            _row(mla_gqn[l], P), _row(mla_gkn[l], P), cos_t, s_lo, s_hi, tt)
        yc = _attention(q, k, v, _row(gn[2 * W:3 * W]), B, tq, tk, True)

        q, k, v = _fox_prep(zf, _row(fox_bf[l], P), _row(fox_gqn[l], P), _row(fox_gkn[l], P), B, tt)
        yd = _attention(q, k, v, _row(gn[3 * W:4 * W]), B, tq, tk, False)

        wo = w_out[l].astype(BF16)
        if l % 2 == 0:
            hn, u2 = _outproj(h, ya, yb, yc, yd, wo, _row(norm2_g[l]), None, tt)
            j = l // 2
            h = _ffn(u2, hn, ffn_w_gate[j].astype(BF16), ffn_w_up[j].astype(BF16),
                     ffn_w_down[j].astype(BF16), tm_ffn, 256)
        else:
            j = l // 2
            wr = _pad_cols(moe_w_router[j].astype(F32), P)
            hn, u2, dg = _outproj(h, ya, yb, yc, yd, wo, _row(norm2_g[l]), wr, tt)
            h = _moe(u2, hn, dg, moe_w_gate[j].astype(BF16), moe_w_up[j].astype(BF16),
                     moe_w_down[j].astype(BF16), tm_ffn, 512)
    return h.reshape(B, lp, D_MODEL)[:, N_META:L]
```

```python
import functools
import math

import jax
import jax.numpy as jnp
from jax import lax
from jax.experimental import pallas as pl
from jax.experimental.pallas import tpu as pltpu

F32 = jnp.float32
BF16 = jnp.bfloat16

D_MODEL = 1024
N_META = 16
CHUNK = 64
SEQ_ALIGN = 128
EPS = 1e-6
GROUP_WIDTH = 256
N_HEADS = 4
HEAD_PAD = 128
LRU_C = 8.0
HG_CHUNK = 16
HG_BLOCK = 128
MLA_NOPE, MLA_ROPE, MLA_V = 64, 32, 64
MLA_QK = MLA_NOPE + MLA_ROPE
MLA_Q_RANK, MLA_KV_RANK = 192, 128
ROPE_THETA = 10000.0
FOX_HD = 64
N_EXPERTS = 8
LOG2E = 1.4426950408889634
ONES_LANE = 64
VMEM_LIMIT = 56 * 1024 * 1024

NT_DIMS = (((1,), (1,)), ((), ()))
TN_DIMS = (((0,), (0,)), ((), ()))


def _cparams(*sem):
    return pltpu.CompilerParams(dimension_semantics=sem, vmem_limit_bytes=VMEM_LIMIT)


def _tile(n, pref, align=SEQ_ALIGN):
    best = None
    for t in range(align, min(n, pref) + 1, align):
        if n % t == 0:
            best = t
    assert best is not None, (n, pref, align)
    return best


def _rms(x, width):
    return lax.rsqrt(jnp.sum(x * x, axis=-1, keepdims=True) * (1.0 / width) + EPS)


def _sigmoid(x):
    return 1.0 / (1.0 + jnp.exp(-x))


def _log_sigmoid(x):
    return jnp.minimum(x, 0.0) - jnp.log(1.0 + jnp.exp(-jnp.abs(x)))


def _full(shape):
    return pl.BlockSpec(shape, lambda *_: (0,) * len(shape))


def _inproj_kernel(h_ref, g_ref, wl_ref, wh_ref, wm_ref, wf_ref, zl_ref, zh_ref, zm_ref, zf_ref):
    x = h_ref[...]
    u = (x * _rms(x, D_MODEL) * g_ref[...]).astype(BF16)
    zl_ref[...] = jnp.dot(u, wl_ref[...], preferred_element_type=F32)
    zh_ref[...] = jnp.dot(u, wh_ref[...], preferred_element_type=F32)
    zm_ref[...] = jnp.dot(u, wm_ref[...], preferred_element_type=F32)
    zf_ref[...] = jnp.dot(u, wf_ref[...], preferred_element_type=F32)


def _inproj(h, g, wl, wh, wm, wf, tm):
    R = h.shape[0]
    row = lambda n: pl.BlockSpec((tm, n), lambda i: (i, 0))
    return pl.pallas_call(
        _inproj_kernel,
        grid=(R // tm,),
        in_specs=[row(D_MODEL), _full((1, D_MODEL)), _full(wl.shape), _full(wh.shape),
                  _full(wm.shape), _full(wf.shape)],
        out_specs=[row(wl.shape[1]), row(wh.shape[1]), row(wm.shape[1]), row(wf.shape[1])],
        out_shape=[jax.ShapeDtypeStruct((R, w.shape[1]), F32) for w in (wl, wh, wm, wf)],
        compiler_params=_cparams("parallel"),
        name="inproj",
    )(h, g, wl, wh, wm, wf)


def _rglru_kernel(z_ref, cw_ref, cb_ref, wa_ref, ba_ref, wx_ref, bx_ref, lam_ref, gn_ref, y_ref,
                  xbuf, a_s, b_s, h_s, hst):
    W = GROUP_WIDTH
    tt = y_ref.shape[0]

    @pl.when(pl.program_id(1) == 0)
    def _():
        xbuf[0:8, :] = jnp.zeros((8, W), F32)
        hst[...] = jnp.zeros((1, W), F32)

    xa = z_ref[:, 0:W]
    ga = z_ref[:, W:2 * W]
    xbuf[8:8 + tt, :] = xa
    u = (cb_ref[...] + xbuf[5:5 + tt, :] * cw_ref[0:1, :] + xbuf[6:6 + tt, :] * cw_ref[1:2, :]
         + xbuf[7:7 + tt, :] * cw_ref[2:3, :] + xa * cw_ref[3:4, :])
    xbuf[0:8, :] = xbuf[tt:tt + 8, :]

    ub = u.astype(BF16)
    r = _sigmoid(jnp.dot(ub, wa_ref[...], preferred_element_type=F32) + ba_ref[...])
    i = _sigmoid(jnp.dot(ub, wx_ref[...], preferred_element_type=F32) + bx_ref[...])
    lam = lam_ref[...]
    softplus_neg_lam = jnp.maximum(-lam, 0.0) + jnp.log(1.0 + jnp.exp(-jnp.abs(lam)))
    a = jnp.exp((-LRU_C) * r * softplus_neg_lam)
    b = jnp.sqrt(1.0 - a * a) * (i * u)

    row = lax.broadcasted_iota(jnp.int32, (tt, W), 0) & 7
    for s in (1, 2, 4):
        ok = row >= s
        b = jnp.where(ok, a * pltpu.roll(b, s, 0) + b, b)
        a = jnp.where(ok, a * pltpu.roll(a, s, 0), a)
    a_s[...] = a
    b_s[...] = b

    def group(gi, hprev):
        sl = pl.ds(pl.multiple_of(gi * 8, 8), 8)
        hg = b_s[sl, :] + a_s[sl, :] * hprev
        h_s[sl, :] = hg
        return hg[7:8, :]

    hst[...] = lax.fori_loop(0, tt // 8, group, hst[...], unroll=8)

    c0 = math.sqrt(2.0 / math.pi)
    gelu = 0.5 * ga * (1.0 + jnp.tanh(c0 * (ga + 0.044715 * (ga * ga * ga))))
    y = h_s[...] * gelu
    y_ref[...] = (y * _rms(y, W) * gn_ref[...]).astype(BF16)


def _rglru(zl, cw, cb, wa, ba, wx, bx, lam, gn, B, tt):
    R = zl.shape[0]
    nt = R // B // tt
    W = GROUP_WIDTH
    return pl.pallas_call(
        _rglru_kernel,
        grid=(B, nt),
        in_specs=[pl.BlockSpec((tt, 2 * W), lambda b, t: (b * nt + t, 0)),
                  _full((4, W)), _full((1, W)), _full((W, W)), _full((1, W)), _full((W, W)),
                  _full((1, W)), _full((1, W)), _full((1, W))],
        out_specs=pl.BlockSpec((tt, W), lambda b, t: (b * nt + t, 0)),
        out_shape=jax.ShapeDtypeStruct((R, W), BF16),
        scratch_shapes=[pltpu.VMEM((tt + 8, W), F32), pltpu.VMEM((tt, W), F32),
                        pltpu.VMEM((tt, W), F32), pltpu.VMEM((tt, W), F32), pltpu.VMEM((1, W), F32)],
        compiler_params=_cparams("parallel", "arbitrary"),
        name="rglru",
    )(zl, cw, cb, wa, ba, wx, bx, lam, gn)


def _hgrn2_kernel(z_ref, lb_ref, gn_ref, y_ref, qd_s, ke_s, v_s, dec_s, o_s, st_s):
    W = GROUP_WIDTH
    tt = y_ref.shape[0]
    C = HG_CHUNK

    @pl.when(pl.program_id(1) == 0)
    def _():
        st_s[...] = jnp.zeros((W, W), F32)

    q = z_ref[:, 0:W]
    fz = z_ref[:, W:2 * W]
    v = z_ref[:, 2 * W:3 * W]
    g = z_ref[:, 3 * W:4 * W]
    lb = lb_ref[...]
    la = jnp.log(lb)
    lq = jnp.log(1.0 - lb) + _log_sigmoid(fz)
    logf = jnp.maximum(la, lq) + jnp.log(1.0 + jnp.exp(-jnp.abs(la - lq)))
    kin = 1.0 - jnp.exp(logf)

    rowc = lax.broadcasted_iota(jnp.int32, (tt, W), 0) & (C - 1)
    b = logf
    rev = logf
    s = 1
    while s < C:
        b = b + jnp.where(rowc >= s, pltpu.roll(b, s, 0), 0.0)
        rev = rev + jnp.where(rowc + s < C, pltpu.roll(rev, tt - s, 0), 0.0)
        s *= 2
    tail = rev - logf
    qd = q * jnp.exp(b)
    kd = (kin * jnp.exp(-b)).astype(BF16)
    vb = v.astype(BF16)
    qd_s[...] = qd.astype(BF16)
    ke_s[...] = (kin * jnp.exp(tail)).astype(BF16)
    v_s[...] = vb
    dec_s[...] = jnp.exp(b + tail)

    lane_head = lax.broadcasted_iota(jnp.int32, (1, W), 1) // (W // N_HEADS)

    rr = lax.broadcasted_iota(jnp.int32, (HG_BLOCK, HG_BLOCK), 0)
    cc = lax.broadcasted_iota(jnp.int32, (HG_BLOCK, HG_BLOCK), 1)
    amask = (rr // C == cc // C) & (cc <= rr)
    for jb in range(tt // HG_BLOCK):
        sl = slice(jb * HG_BLOCK, (jb + 1) * HG_BLOCK)
        qb, kb, vv = qd[sl], kd[sl], vb[sl]
        acc = jnp.zeros((HG_BLOCK, W), F32)
        for hh in range(N_HEADS):
            hm = lane_head == hh
            qh = jnp.where(hm, qb, 0.0).astype(BF16)
            att = lax.dot_general(qh, kb, NT_DIMS, preferred_element_type=F32)
            att = jnp.where(amask, att, 0.0).astype(BF16)
            acc = jnp.where(hm, jnp.dot(att, vv, preferred_element_type=F32), acc)
        o_s[sl, :] = acc

    def chunk(c, carry):
        sl = pl.ds(pl.multiple_of(c * C, C), C)
        qc = qd_s[sl, :]
        st = st_s[...]
        qbd = jnp.concatenate([jnp.where(lane_head == hh, qc, jnp.zeros_like(qc))
                               for hh in range(N_HEADS)], axis=0)
        res = lax.dot_general(qbd, st.astype(BF16), NT_DIMS, preferred_element_type=F32)
        oi = jnp.zeros((C, W), F32)
        for hh in range(N_HEADS):
            oi = jnp.where(lane_head == hh, res[hh * C:(hh + 1) * C, :], oi)
        o_s[sl, :] = o_s[sl, :] + oi
        upd = lax.dot_general(v_s[sl, :], ke_s[sl, :], TN_DIMS, preferred_element_type=F32)
        st_s[...] = st * dec_s[pl.ds(c * C, 1), :] + upd
        return carry

    lax.fori_loop(0, tt // C, chunk, 0, unroll=math.gcd(tt // C, 8))

    y = o_s[...] * (g * _sigmoid(g))
    y_ref[...] = (y * _rms(y, W) * gn_ref[...]).astype(BF16)


def _hgrn2(zh, lb, gn, B, tt):
    R = zh.shape[0]
    nt = R // B // tt
    W = GROUP_WIDTH
    return pl.pallas_call(
        _hgrn2_kernel,
        grid=(B, nt),
        in_specs=[pl.BlockSpec((tt, 4 * W), lambda b, t: (b * nt + t, 0)), _full((1, W)), _full((1, W))],
        out_specs=pl.BlockSpec((tt, W), lambda b, t: (b * nt + t, 0)),
        out_shape=jax.ShapeDtypeStruct((R, W), BF16),
        scratch_shapes=[pltpu.VMEM((tt, W), BF16), pltpu.VMEM((tt, W), BF16), pltpu.VMEM((tt, W), BF16),
                        pltpu.VMEM((tt, W), F32), pltpu.VMEM((tt, W), F32), pltpu.VMEM((W, W), F32)],
        compiler_params=_cparams("parallel", "arbitrary"),
        name="hgrn2",
    )(zh, lb, gn)


def _rope(x, cos, s_lo, s_hi):
    return x * cos + pltpu.roll(x, 16, 1) * s_hi + pltpu.roll(x, HEAD_PAD - 16, 1) * s_lo


def _mla_prep_kernel(z_ref, gq_ref, wuq_ref, gkv_ref, wuk_ref, wuv_ref, gqn_ref, gkn_ref,
                     cos_ref, slo_ref, shi_ref, q_ref, k_ref, v_ref):
    P = HEAD_PAD
    ckv = z_ref[:, 0:P]
    krb = z_ref[:, P:2 * P]
    cq = z_ref[:, 2 * P:4 * P]
    qn = (cq * _rms(cq, MLA_Q_RANK) * gq_ref[...]).astype(BF16)
    kvn = (ckv * _rms(ckv, MLA_KV_RANK) * gkv_ref[...]).astype(BF16)
    q = jnp.dot(qn, wuq_ref[...], preferred_element_type=F32)
    kn = jnp.dot(kvn, wuk_ref[...], preferred_element_type=F32)
    vv = jnp.dot(kvn, wuv_ref[...], preferred_element_type=F32)
    cos, slo, shi = cos_ref[...], slo_ref[...], shi_ref[...]
    lane = lax.broadcasted_iota(jnp.int32, (1, P), 1)
    qscale = (MLA_QK ** -0.5) * LOG2E
    for hh in range(N_HEADS):
        sl = slice(hh * P, (hh + 1) * P)
        qh = q[:, sl]
        qh = qh * _rms(qh, MLA_QK) * gqn_ref[...]
        q_ref[:, sl] = (_rope(qh, cos, slo, shi) * qscale).astype(BF16)
        kh = kn[:, sl] + krb
        kh = kh * _rms(kh, MLA_QK) * gkn_ref[...]
        k_ref[:, sl] = _rope(kh, cos, slo, shi).astype(BF16)
        v_ref[:, sl] = jnp.where(lane == ONES_LANE, 1.0, vv[:, sl]).astype(BF16)


def _mla_prep(zm, gq, wuq, gkv, wuk, wuv, gqn, gkn, cos, slo, shi, tm):
    R = zm.shape[0]
    P = HEAD_PAD
    nt = cos.shape[0] // tm
    row = lambda n: pl.BlockSpec((tm, n), lambda i: (i, 0))
    tab = pl.BlockSpec((tm, P), lambda i: (i % nt, 0))
    out = jax.ShapeDtypeStruct((R, N_HEADS * P), BF16)
    return pl.pallas_call(
        _mla_prep_kernel,
        grid=(R // tm,),
        in_specs=[row(4 * P), _full((1, 2 * P)), _full(wuq.shape), _full((1, P)), _full(wuk.shape),
                  _full(wuv.shape), _full((1, P)), _full((1, P)), tab, tab, tab],
        out_specs=[row(N_HEADS * P)] * 3,
        out_shape=[out, out, out],
        compiler_params=_cparams("parallel"),
        name="mla_prep",
    )(zm, gq, wuq, gkv, wuk, wuv, gqn, gkn, cos, slo, shi)


STAT_ROWS = 8
STAT_C_FIRST, STAT_C_LAST, STAT_Q2, STAT_K2 = range(4)


def _fox_prep_kernel(z_ref, bf_ref, gqn_ref, gkn_ref, q_ref, k_ref, v_ref, stat_ref, carry):
    P = HEAD_PAD
    tt = q_ref.shape[0]

    @pl.when(pl.program_id(1) == 0)
    def _():
        carry[...] = jnp.zeros((1, P), F32)

    c = _log_sigmoid(z_ref[:, 3 * N_HEADS * P:] + bf_ref[...])
    row = lax.broadcasted_iota(jnp.int32, (tt, P), 0)
    s = 1
    while s < tt:
        c = c + jnp.where(row >= s, pltpu.roll(c, s, 0), 0.0)
        s *= 2
    c = c + carry[...]
    carry[...] = c[tt - 1:tt, :]
    c = c * LOG2E
    c1 = c.astype(BF16).astype(F32)
    c2 = (c - c1).astype(BF16).astype(F32)
    c3 = c - c1 - c2

    lane = lax.broadcasted_iota(jnp.int32, (1, P), 1)
    qscale = (FOX_HD ** -0.5) * LOG2E
    q2 = jnp.zeros((1, P), F32)
    k2 = jnp.zeros((1, P), F32)
    for hh in range(N_HEADS):
        p1 = c1[:, hh:hh + 1]
        p2 = c2[:, hh:hh + 1]
        p3 = c3[:, hh:hh + 1]
        qh = z_ref[:, hh * P:(hh + 1) * P]
        qh = qh * _rms(qh, FOX_HD) * gqn_ref[...] * qscale
        q2 = jnp.where(lane == hh, jnp.max(jnp.sum(qh * qh, axis=-1, keepdims=True), axis=0, keepdims=True), q2)
        qh = jnp.where(lane == 64, p1, jnp.where(lane == 65, p2, jnp.where(lane == 66, p3, qh)))
        qh = jnp.where((lane >= 67) & (lane < 70), 1.0, qh)
        q_ref[:, hh * P:(hh + 1) * P] = qh.astype(BF16)
        kh = z_ref[:, (N_HEADS + hh) * P:(N_HEADS + hh + 1) * P]
        kh = kh * _rms(kh, FOX_HD) * gkn_ref[...]
        k2 = jnp.where(lane == hh, jnp.max(jnp.sum(kh * kh, axis=-1, keepdims=True), axis=0, keepdims=True), k2)
        kh = jnp.where(lane == 67, -p1, jnp.where(lane == 68, -p2, jnp.where(lane == 69, -p3, kh)))
        kh = jnp.where((lane >= 64) & (lane < 67), 1.0, kh)
        k_ref[:, hh * P:(hh + 1) * P] = kh.astype(BF16)
        vh = z_ref[:, (2 * N_HEADS + hh) * P:(2 * N_HEADS + hh + 1) * P]
        v_ref[:, hh * P:(hh + 1) * P] = jnp.where(lane == ONES_LANE, 1.0, vh).astype(BF16)

    srow = lax.broadcasted_iota(jnp.int32, (STAT_ROWS, P), 0)
    stat = jnp.zeros((STAT_ROWS, P), F32)
    for rr, val in ((STAT_C_FIRST, c[0:1, :]), (STAT_C_LAST, c[tt - 1:tt, :]), (STAT_Q2, q2), (STAT_K2, k2)):
        stat = jnp.where(srow == rr, val, stat)
    stat_ref[...] = stat


def _fox_prep(zf, bf, gqn, gkn, B, tt):
    R = zf.shape[0]
    nt = R // B // tt
    P = HEAD_PAD
    blk = lambda n: pl.BlockSpec((tt, n), lambda b, t: (b * nt + t, 0))
    out = jax.ShapeDtypeStruct((R, N_HEADS * P), BF16)
    return pl.pallas_call(
        _fox_prep_kernel,
        grid=(B, nt),
        in_specs=[blk(zf.shape[1]), _full((1, P)), _full((1, P)), _full((1, P))],
        out_specs=[blk(N_HEADS * P)] * 3 + [pl.BlockSpec((STAT_ROWS, P), lambda b, t: (b * nt + t, 0))],
        out_shape=[out, out, out, jax.ShapeDtypeStruct((B * nt * STAT_ROWS, P), F32)],
        scratch_shapes=[pltpu.VMEM((1, P), F32)],
        compiler_params=_cparams("parallel", "arbitrary"),
        name="fox_prep",
    )(zf, bf, gqn, gkn)


SKIP_LOG2_MARGIN = 40.0
NORM_SLACK = 1.02


def _fox_first_chunk(stats, B, nt):
    st = stats.reshape(B, nt, STAT_ROWS, HEAD_PAD)[..., :N_HEADS]
    c_first, c_last = st[:, :, STAT_C_FIRST], st[:, :, STAT_C_LAST]
    bound = jnp.sqrt(jnp.max(st[:, :, STAT_Q2], axis=1) * jnp.max(st[:, :, STAT_K2], axis=1)) * NORM_SLACK
    gap = 2.0 * bound[:, None, None, :] + c_first[:, :, None, :] - c_last[:, None, :, :]
    earlier = jnp.arange(nt)[None, :] < jnp.arange(nt)[:, None]
    skip = (gap < -SKIP_LOG2_MARGIN) & earlier[None, :, :, None]
    return jnp.min(jnp.sum(skip, axis=2), axis=-1).astype(jnp.int32).reshape(-1)


def _attn_kernel(first_ref, q_ref, k_ref, v_ref, gn_ref, y_ref, m_s, acc_s, o_s, *, chunk_causal, tk):
    P = HEAD_PAD
    tq = q_ref.shape[0]
    lp = k_ref.shape[0]
    blk = pl.program_id(0) * pl.num_programs(1) + pl.program_id(1)
    q0 = pl.program_id(1) * tq

    qpos = q0 + lax.broadcasted_iota(jnp.int32, (tq, 1), 0)
    if chunk_causal:
        qlim = N_META + CHUNK * ((qpos + (CHUNK - N_META)) // CHUNK)
        reach = N_META
    else:
        qlim = qpos + 1
        reach = 0
    n_full = (q0 + reach) // tk
    n_diag = (jnp.minimum(q0 + tq, lp) + tk - 1) // tk

    def visit(k0, width, masked):
        ks = pl.ds(pl.multiple_of(k0, SEQ_ALIGN), width)
        if masked:
            vis = (k0 + lax.broadcasted_iota(jnp.int32, (1, width), 1)) < qlim
        for hh in range(N_HEADS):
            hs = slice(hh * P, (hh + 1) * P)
            s = lax.dot_general(q_ref[:, hs], k_ref[ks, hs], NT_DIMS, preferred_element_type=F32)
            if masked:
                s = jnp.where(vis, s, -jnp.inf)
            tiles = [s[:, c * P:(c + 1) * P] for c in range(width // P)]
            mx = tiles[0]
            for t in tiles[1:]:
                mx = jnp.maximum(mx, t)
            m_old = m_s[hh]
            m_new = jnp.maximum(m_old, jnp.max(mx, axis=-1, keepdims=True))
            p = jnp.concatenate([jnp.exp2((t - m_new).astype(BF16)) for t in tiles], axis=1)
            acc_s[hh] = jnp.exp2(m_old - m_new) * acc_s[hh] + jnp.dot(
                p, v_ref[ks, hs], preferred_element_type=F32)
            m_s[hh] = m_new

    m_s[...] = jnp.full(m_s.shape, -jnp.inf, F32)
    acc_s[...] = jnp.zeros(acc_s.shape, F32)

    def full_body(j, carry):
        visit(j * tk, tk, False)
        return carry

    def masked_body(j, carry):
        visit(j * tk, tk, True)
        return carry

    lax.fori_loop(first_ref[blk], n_full, full_body, 0)
    lax.fori_loop(n_full, n_diag, masked_body, 0)
    if chunk_causal:
        @pl.when(q0 + tq < lp)
        def _():
            visit(q0 + tq, SEQ_ALIGN, True)

    hd = GROUP_WIDTH // N_HEADS
    for hh in range(N_HEADS):
        acc = acc_s[hh]
        o_s[:, hh * hd:(hh + 1) * hd] = acc[:, 0:hd] / acc[:, ONES_LANE:ONES_LANE + 1]
    y = o_s[...]
    y_ref[...] = (y * _rms(y, GROUP_WIDTH) * gn_ref[...]).astype(BF16)


def _attention(first_chunk, q, k, v, gn, B, tq, tk, chunk_causal):
    R = q.shape[0]
    lp = R // B
    nq = lp // tq
    P = HEAD_PAD
    W = GROUP_WIDTH
    kv_spec = pl.BlockSpec((lp, N_HEADS * P), lambda b, i, fc: (b, 0))
    return pl.pallas_call(
        functools.partial(_attn_kernel, chunk_causal=chunk_causal, tk=tk),
        grid_spec=pltpu.PrefetchScalarGridSpec(
            num_scalar_prefetch=1,
            grid=(B, nq),
            in_specs=[pl.BlockSpec((tq, N_HEADS * P), lambda b, i, fc: (b * nq + i, 0)), kv_spec, kv_spec,
                      pl.BlockSpec((1, W), lambda b, i, fc: (0, 0))],
            out_specs=pl.BlockSpec((tq, W), lambda b, i, fc: (b * nq + i, 0)),
            scratch_shapes=[pltpu.VMEM((N_HEADS, tq, P), F32), pltpu.VMEM((N_HEADS, tq, P), F32),
                            pltpu.VMEM((tq, W), F32)],
        ),
        out_shape=jax.ShapeDtypeStruct((R, W), BF16),
        compiler_params=_cparams("parallel", "arbitrary"),
        name="mla_attn" if chunk_causal else "fox_attn",
    )(first_chunk, q, k, v, gn)


ROUTE_E1, ROUTE_E2, ROUTE_R1, ROUTE_R2, ROUTE_G1, ROUTE_G2 = range(6)


def _outproj_kernel(h_ref, ya_ref, yb_ref, yc_ref, yd_ref, w_ref, g_ref, *rest, with_router):
    W = GROUP_WIDTH
    if with_router:
        wr_ref, hn_ref, u_ref, route_ref, cnt_ref = rest
    else:
        hn_ref, u_ref = rest
    acc = h_ref[...]
    for gi, y_ref in enumerate((ya_ref, yb_ref, yc_ref, yd_ref)):
        acc = acc + jnp.dot(y_ref[...], w_ref[gi * W:(gi + 1) * W, :], preferred_element_type=F32)
    hn_ref[...] = acc
    u = acc * _rms(acc, D_MODEL) * g_ref[...]
    if not with_router:
        u_ref[...] = u.astype(BF16)
        return
    u_ref[...] = u
    tm = u.shape[0]

    @pl.when(pl.program_id(0) == 0)
    def _():
        cnt_ref[...] = jnp.zeros(cnt_ref.shape, F32)

    logits = jnp.dot(u, wr_ref[...], preferred_element_type=F32, precision=lax.Precision.HIGHEST)
    lane = lax.broadcasted_iota(jnp.int32, logits.shape, 1).astype(F32)
    lg = jnp.where(lane < N_EXPERTS, logits, -jnp.inf)
    m1 = jnp.max(lg, axis=-1, keepdims=True)
    i1 = jnp.min(jnp.where(lg == m1, lane, 1e9), axis=-1, keepdims=True)
    lg2 = jnp.where(lane == i1, -jnp.inf, lg)
    m2 = jnp.max(lg2, axis=-1, keepdims=True)
    i2 = jnp.min(jnp.where(lg2 == m2, lane, 1e9), axis=-1, keepdims=True)
    e = jnp.exp(m2 - m1)
    g1 = 1.0 / (1.0 + e)
    picks = jnp.where(lane == i1, 1.0, 0.0) + jnp.where(lane == i2, 1.0, 0.0)
    earlier = (lax.broadcasted_iota(jnp.int32, (tm, tm), 0) > lax.broadcasted_iota(jnp.int32, (tm, tm), 1))
    base = cnt_ref[...] + jnp.dot(earlier.astype(BF16), picks.astype(BF16), preferred_element_type=F32)
    r1 = jnp.sum(jnp.where(lane == i1, base, 0.0), axis=-1, keepdims=True)
    r2 = jnp.sum(jnp.where(lane == i2, base, 0.0), axis=-1, keepdims=True)
    cnt_ref[...] = cnt_ref[...] + jnp.sum(picks, axis=0, keepdims=True)
    rec = jnp.zeros(logits.shape, F32)
    for ln, val in ((ROUTE_E1, i1), (ROUTE_E2, i2), (ROUTE_R1, r1), (ROUTE_R2, r2),
                    (ROUTE_G1, g1), (ROUTE_G2, e * g1)):
        rec = jnp.where(lane == ln, val, rec)
    route_ref[...] = rec


def _outproj(h, ya, yb, yc, yd, w, g, wr, tm):
    R = h.shape[0]
    W = GROUP_WIDTH
    with_router = wr is not None
    row = lambda n: pl.BlockSpec((tm, n), lambda i: (i, 0))
    in_specs = [row(D_MODEL), row(W), row(W), row(W), row(W), _full(w.shape), _full((1, D_MODEL))]
    out_specs = [row(D_MODEL), row(D_MODEL)]
    out_shape = [jax.ShapeDtypeStruct((R, D_MODEL), F32),
                 jax.ShapeDtypeStruct((R, D_MODEL), F32 if with_router else BF16)]
    args = [h, ya, yb, yc, yd, w, g]
    if with_router:
        in_specs.append(_full(wr.shape))
        out_specs += [row(wr.shape[1]), _full((1, wr.shape[1]))]
        out_shape += [jax.ShapeDtypeStruct((R, wr.shape[1]), F32), jax.ShapeDtypeStruct((1, wr.shape[1]), F32)]
        args.append(wr)
    return pl.pallas_call(
        functools.partial(_outproj_kernel, with_router=with_router),
        grid=(R // tm,),
        in_specs=in_specs, out_specs=out_specs, out_shape=out_shape,
        compiler_params=_cparams("arbitrary" if with_router else "parallel"),
        name="outproj_router" if with_router else "outproj",
    )(*args)


def _ffn_kernel(u_ref, h_ref, wg_ref, wu_ref, wd_ref, o_ref):
    @pl.when(pl.program_id(1) == 0)
    def _():
        o_ref[...] = h_ref[...]

    u = u_ref[...]
    a = jnp.dot(u, wg_ref[...], preferred_element_type=F32)
    b = jnp.dot(u, wu_ref[...], preferred_element_type=F32)
    hid = (a * _sigmoid(a) * b).astype(BF16)
    o_ref[...] += jnp.dot(hid, wd_ref[...], preferred_element_type=F32)


def _ffn(u, h, wg, wu, wd, tm, tf):
    R = h.shape[0]
    dff = wg.shape[1]
    return pl.pallas_call(
        _ffn_kernel,
        grid=(R // tm, dff // tf),
        in_specs=[pl.BlockSpec((tm, D_MODEL), lambda i, f: (i, 0)),
                  pl.BlockSpec((tm, D_MODEL), lambda i, f: (i, 0)),
                  pl.BlockSpec((D_MODEL, tf), lambda i, f: (0, f)),
                  pl.BlockSpec((D_MODEL, tf), lambda i, f: (0, f)),
                  pl.BlockSpec((tf, D_MODEL), lambda i, f: (f, 0))],
        out_specs=pl.BlockSpec((tm, D_MODEL), lambda i, f: (i, 0)),
        out_shape=jax.ShapeDtypeStruct((R, D_MODEL), F32),
        compiler_params=_cparams("parallel", "arbitrary"),
        name="ffn",
    )(u, h, wg, wu, wd)


def _row_copy(src_ref, src_row, dst_ref, dst_row, sem):
    return pltpu.make_async_copy(src_ref.at[pl.ds(src_row, 1)], dst_ref.at[pl.ds(dst_row, 1)], sem)


def _dispatch_kernel(pos1_ref, pos2_ref, u_ref, xs_in_ref, xs_ref, sem):
    del xs_in_ref
    tm = u_ref.shape[0]
    t0 = pl.program_id(0) * tm

    def issue(r, carry):
        _row_copy(u_ref, r, xs_ref, pos1_ref[t0 + r], sem).start()
        _row_copy(u_ref, r, xs_ref, pos2_ref[t0 + r], sem).start()
        return carry

    lax.fori_loop(0, tm, issue, 0, unroll=8)

    def drain(r, carry):
        _row_copy(u_ref, r, xs_ref, 0, sem).wait()
        _row_copy(u_ref, r, xs_ref, 0, sem).wait()
        return carry

    lax.fori_loop(0, tm, drain, 0, unroll=8)


def _dispatch(pos1, pos2, u, n_slots, tm):
    R = u.shape[0]
    xs0 = jnp.zeros((n_slots, D_MODEL), F32)
    return pl.pallas_call(
        _dispatch_kernel,
        grid_spec=pltpu.PrefetchScalarGridSpec(
            num_scalar_prefetch=2,
            grid=(R // tm,),
            in_specs=[pl.BlockSpec((tm, D_MODEL), lambda i, p1, p2: (i, 0)),
                      pl.BlockSpec(memory_space=pl.ANY)],
            out_specs=pl.BlockSpec(memory_space=pl.ANY),
            scratch_shapes=[pltpu.SemaphoreType.DMA],
        ),
        out_shape=jax.ShapeDtypeStruct((n_slots, D_MODEL), F32),
        input_output_aliases={3: 0},
        compiler_params=_cparams("arbitrary"),
        name="moe_dispatch",
    )(pos1, pos2, u, xs0)


def _expert_ffn_kernel(te_ref, nt_ref, x_ref, wg_ref, wu_ref, wd_ref, y_ref, xb_s):
    f = pl.program_id(1)
    used = pl.program_id(0) < nt_ref[0]

    @pl.when(jnp.logical_not(used) & (f == 0))
    def _():
        y_ref[...] = jnp.zeros(y_ref.shape, F32)

    @pl.when(used)
    def _():
        @pl.when(f == 0)
        def _():
            xb_s[...] = x_ref[...].astype(BF16)

        x = xb_s[...]
        a = jnp.dot(x, wg_ref[...], preferred_element_type=F32)
        b = jnp.dot(x, wu_ref[...], preferred_element_type=F32)
        hid = (a * _sigmoid(a) * b).astype(BF16)
        out = jnp.dot(hid, wd_ref[...], preferred_element_type=F32)

        @pl.when(f == 0)
        def _():
            y_ref[...] = out

        @pl.when(f > 0)
        def _():
            y_ref[...] += out


def _expert_ffn(tile_expert, n_tiles, xs, wg, wu, wd, tm, tf):
    n_slots = xs.shape[0]
    dff = wg.shape[2]
    row_map = lambda i, f, te, nt: (jnp.minimum(i, nt[0] - 1), 0)
    out_map = lambda i, f, te, nt: (i, 0)
    return pl.pallas_call(
        _expert_ffn_kernel,
        grid_spec=pltpu.PrefetchScalarGridSpec(
            num_scalar_prefetch=2,
            grid=(n_slots // tm, dff // tf),
            in_specs=[pl.BlockSpec((tm, D_MODEL), row_map),
                      pl.BlockSpec((None, D_MODEL, tf), lambda i, f, te, nt: (te[i], 0, f)),
                      pl.BlockSpec((None, D_MODEL, tf), lambda i, f, te, nt: (te[i], 0, f)),
                      pl.BlockSpec((None, tf, D_MODEL), lambda i, f, te, nt: (te[i], f, 0))],
            out_specs=pl.BlockSpec((tm, D_MODEL), out_map),
            scratch_shapes=[pltpu.VMEM((tm, D_MODEL), BF16)],
        ),
        out_shape=jax.ShapeDtypeStruct((n_slots, D_MODEL), F32),
        compiler_params=_cparams("arbitrary", "arbitrary"),
        name="moe_expert_ffn",
    )(tile_expert, n_tiles, xs, wg, wu, wd)


def _combine_kernel(pos1_ref, pos2_ref, h_ref, route_ref, ys_ref, o_ref, y1_s, y2_s, sem):
    tm = h_ref.shape[0]
    t0 = pl.program_id(0) * tm

    def issue(r, carry):
        _row_copy(ys_ref, pos1_ref[t0 + r], y1_s, r, sem).start()
        _row_copy(ys_ref, pos2_ref[t0 + r], y2_s, r, sem).start()
        return carry

    lax.fori_loop(0, tm, issue, 0, unroll=8)

    def drain(r, carry):
        _row_copy(ys_ref, 0, y1_s, r, sem).wait()
        _row_copy(ys_ref, 0, y2_s, r, sem).wait()
        return carry

    lax.fori_loop(0, tm, drain, 0, unroll=8)
    rec = route_ref[...]
    g1 = rec[:, ROUTE_G1:ROUTE_G1 + 1]
    g2 = rec[:, ROUTE_G2:ROUTE_G2 + 1]
    o_ref[...] = h_ref[...] + g1 * y1_s[...] + g2 * y2_s[...]


def _combine(pos1, pos2, h, route, ys, tm):
    R = h.shape[0]
    row = lambda n: pl.BlockSpec((tm, n), lambda i, p1, p2: (i, 0))
    return pl.pallas_call(
        _combine_kernel,
        grid_spec=pltpu.PrefetchScalarGridSpec(
            num_scalar_prefetch=2,
            grid=(R // tm,),
            in_specs=[row(D_MODEL), row(route.shape[1]), pl.BlockSpec(memory_space=pl.ANY)],
            out_specs=row(D_MODEL),
            scratch_shapes=[pltpu.VMEM((tm, D_MODEL), F32), pltpu.VMEM((tm, D_MODEL), F32),
                            pltpu.SemaphoreType.DMA],
        ),
        out_shape=jax.ShapeDtypeStruct((R, D_MODEL), F32),
        compiler_params=_cparams("arbitrary"),
        name="moe_combine",
    )(pos1, pos2, h, route, ys)


def _moe(u, h, route, counts, wg, wu, wd, tm_rows, tm_expert, tf):
    R = h.shape[0]
    ne = wg.shape[0]
    cnt = counts[0, :ne].astype(jnp.int32)
    padded = -(-cnt // tm_expert) * tm_expert
    ends = jnp.cumsum(padded)
    offs = ends - padded
    col = lambda ln: route[:, ln].astype(jnp.int32)
    pos1 = jnp.take(offs, col(ROUTE_E1)) + col(ROUTE_R1)
    pos2 = jnp.take(offs, col(ROUTE_E2)) + col(ROUTE_R2)
    n_slots = (2 * R // tm_expert + ne) * tm_expert
    n_tiles = (ends[-1] // tm_expert).reshape(1)
    tile_start = jnp.arange(n_slots // tm_expert, dtype=jnp.int32) * tm_expert
    tile_expert = jnp.sum(tile_start[:, None] >= ends[None, :], axis=1).astype(jnp.int32)
    last_expert = jnp.sum((ends[-1] - 1) >= ends).astype(jnp.int32)
    tile_expert = jnp.minimum(tile_expert, last_expert)

    xs = _dispatch(pos1, pos2, u, n_slots, tm_rows)
    ys = _expert_ffn(tile_expert, n_tiles, xs, wg, wu, wd, tm_expert, tf)
    return _combine(pos1, pos2, h, route, ys, tm_rows)


def _pad_heads(w, real):
    rows = w.shape[0]
    w = w.reshape(rows, N_HEADS, real)
    return jnp.pad(w, ((0, 0), (0, 0), (0, HEAD_PAD - real))).reshape(rows, N_HEADS * HEAD_PAD)


def _pad_cols(w, n):
    return jnp.pad(w, ((0, 0), (0, n - w.shape[1])))


def _row(v, n=None):
    v = v.reshape(1, -1).astype(F32)
    return v if n is None else _pad_cols(v, n)


def _block_diag(w):
    nb, c, d = w.shape
    eye = jnp.eye(nb, dtype=w.dtype)
    return (eye[:, None, :, None] * w[:, :, None, :]).reshape(nb * c, nb * d)


def _rope_tables(lp):
    half = MLA_ROPE // 2
    inv = ROPE_THETA ** (-jnp.arange(half, dtype=F32) / half)
    ang = jnp.arange(lp, dtype=jnp.int32).astype(F32)[:, None] * inv[None, :]
    cos, sin = jnp.cos(ang), jnp.sin(ang)
    one = jnp.ones((lp, MLA_NOPE), F32)
    zero = jnp.zeros((lp, MLA_NOPE), F32)
    zh = jnp.zeros((lp, half), F32)
    tail1 = jnp.ones((lp, HEAD_PAD - MLA_QK), F32)
    tail0 = jnp.zeros((lp, HEAD_PAD - MLA_QK), F32)
    cos_t = jnp.concatenate([one, cos, cos, tail1], axis=1)
    s_lo = jnp.concatenate([zero, -sin, zh, tail0], axis=1)
    s_hi = jnp.concatenate([zero, zh, sin, tail0], axis=1)
    return cos_t, s_lo, s_hi


def kernel(x, meta, norm1_g, norm2_g, w_in, w_out, out_norm_g, lru_conv_w, lru_conv_b, lru_wa, lru_ba, lru_wx, lru_bx, lru_lambda, hg_lb_logits, mla_gq, mla_w_uq, mla_gkv, mla_w_ukv, mla_gqn, mla_gkn, fox_gqn, fox_gkn, fox_bf, ffn_w_gate, ffn_w_up, ffn_w_down, moe_w_router, moe_w_gate, moe_w_up, moe_w_down):
    B, S, _ = x.shape
    depth = w_in.shape[0]
    L = N_META + S
    lp = -(-L // SEQ_ALIGN) * SEQ_ALIGN
    R = B * lp
    W = GROUP_WIDTH
    P = HEAD_PAD

    tt = _tile(lp, 640)
    tq = tk = tt
    tm_ffn = _tile(R, 1280)

    h = jnp.concatenate([jnp.broadcast_to(meta[None].astype(x.dtype), (B, N_META, D_MODEL)), x,
                         jnp.zeros((B, lp - L, D_MODEL), x.dtype)], axis=1).reshape(R, D_MODEL)
    cos_t, s_lo, s_hi = _rope_tables(lp)
    lb_cum = jnp.cumsum(jax.nn.softmax(hg_lb_logits.astype(F32), axis=0), axis=0)

    o = 0
    offs = []
    for n in (W, W, W, W, W, W, MLA_Q_RANK, MLA_KV_RANK, MLA_ROPE, W, W, W, N_HEADS):
        offs.append(o)
        o += n
    (o_xa, _, o_hq, _, _, _, o_cq, o_ckv, o_kr, o_fq, o_fk, o_fv, o_ff) = offs

    for l in range(depth):
        w = w_in[l]
        wl = w[:, o_xa:o_xa + 2 * W].astype(BF16)
        wh = w[:, o_hq:o_hq + 4 * W].astype(BF16)
        wm = jnp.concatenate([
            w[:, o_ckv:o_ckv + MLA_KV_RANK],
            jnp.zeros((D_MODEL, MLA_NOPE), F32), w[:, o_kr:o_kr + MLA_ROPE],
            jnp.zeros((D_MODEL, P - MLA_QK), F32),
            _pad_cols(w[:, o_cq:o_cq + MLA_Q_RANK], 2 * P)], axis=1).astype(BF16)
        wf = jnp.concatenate([
            _pad_heads(w[:, o_fq:o_fq + W], FOX_HD), _pad_heads(w[:, o_fk:o_fk + W], FOX_HD),
            _pad_heads(w[:, o_fv:o_fv + W], FOX_HD), _pad_cols(w[:, o_ff:o_ff + N_HEADS], P)],
            axis=1).astype(BF16)
        zl, zh, zm, zf = _inproj(h, _row(norm1_g[l]), wl, wh, wm, wf, tt)

        gn = out_norm_g[l].astype(F32)
        ya = _rglru(zl, lru_conv_w[l].astype(F32), _row(lru_conv_b[l]),
                    _block_diag(lru_wa[l]).astype(BF16), _row(lru_ba[l]),
                    _block_diag(lru_wx[l]).astype(BF16), _row(lru_bx[l]),
                    _row(lru_lambda[l]), _row(gn[0:W]), B, tt)
        yb = _hgrn2(zh, _row(lb_cum[l] - lb_cum[0]), _row(gn[W:2 * W]), B, tt)

        wuq = jnp.pad(_pad_heads(mla_w_uq[l], MLA_QK), ((0, 2 * P - MLA_Q_RANK), (0, 0))).astype(BF16)
        wukv = mla_w_ukv[l].reshape(MLA_KV_RANK, N_HEADS, MLA_NOPE + MLA_V)
        wuk = _pad_heads(wukv[:, :, :MLA_NOPE].reshape(MLA_KV_RANK, -1), MLA_NOPE).astype(BF16)
        wuv = _pad_heads(wukv[:, :, MLA_NOPE:].reshape(MLA_KV_RANK, -1), MLA_V).astype(BF16)
        q, k, v = _mla_prep(zm, _row(mla_gq[l], 2 * P), wuq, _row(mla_gkv[l]), wuk, wuv,
                            _row(mla_gqn[l], P), _row(mla_gkn[l], P), cos_t, s_lo, s_hi, tt)
        visit_all = jnp.zeros((B * (lp // tq),), jnp.int32)
        yc = _attention(visit_all, q, k, v, _row(gn[2 * W:3 * W]), B, tq, tk, True)

        q, k, v, stats = _fox_prep(zf, _row(fox_bf[l], P), _row(fox_gqn[l], P), _row(fox_gkn[l], P), B, tt)
        yd = _attention(_fox_first_chunk(stats, B, lp // tt), q, k, v, _row(gn[3 * W:4 * W]), B, tq, tk, False)

        wo = w_out[l].astype(BF16)
        if l % 2 == 0:
            hn, u2 = _outproj(h, ya, yb, yc, yd, wo, _row(norm2_g[l]), None, tt)
            j = l // 2
            h = _ffn(u2, hn, ffn_w_gate[j].astype(BF16), ffn_w_up[j].astype(BF16),
                     ffn_w_down[j].astype(BF16), tm_ffn, 256)
        else:
            j = l // 2
            wr = _pad_cols(moe_w_router[j].astype(F32), P)
            hn, u2, route, counts = _outproj(h, ya, yb, yc, yd, wo, _row(norm2_g[l]), wr, tt)
            h = _moe(u2, hn, route, counts, moe_w_gate[j].astype(BF16), moe_w_up[j].astype(BF16),
                     moe_w_down[j].astype(BF16), tt, 512, 512)
    return h.reshape(B, lp, D_MODEL)[:, N_META:L]
```

```python
import functools
import math

import jax
import jax.numpy as jnp
from jax import lax
from jax.experimental import pallas as pl
from jax.experimental.pallas import tpu as pltpu

F32 = jnp.float32
BF16 = jnp.bfloat16

D_MODEL = 1024
N_META = 16
CHUNK = 64
SEQ_ALIGN = 128
EPS = 1e-6
GROUP_WIDTH = 256
N_HEADS = 4
HEAD_PAD = 128
LRU_C = 8.0
HG_CHUNK = 16
HG_BLOCK = 128
MLA_NOPE, MLA_ROPE, MLA_V = 64, 32, 64
MLA_QK = MLA_NOPE + MLA_ROPE
MLA_Q_RANK, MLA_KV_RANK = 192, 128
ROPE_THETA = 10000.0
FOX_HD = 64
N_EXPERTS = 8
LOG2E = 1.4426950408889634
ONES_LANE = 64
VMEM_LIMIT = 56 * 1024 * 1024

NT_DIMS = (((1,), (1,)), ((), ()))
TN_DIMS = (((0,), (0,)), ((), ()))


def _cparams(*sem):
    return pltpu.CompilerParams(dimension_semantics=sem, vmem_limit_bytes=VMEM_LIMIT)


def _tile(n, pref, align=SEQ_ALIGN):
    best = None
    for t in range(align, min(n, pref) + 1, align):
        if n % t == 0:
            best = t
    assert best is not None, (n, pref, align)
    return best


def _rms(x, width):
    return lax.rsqrt(jnp.sum(x * x, axis=-1, keepdims=True) * (1.0 / width) + EPS)


def _sigmoid(x):
    return 1.0 / (1.0 + jnp.exp(-x))


def _log_sigmoid(x):
    return jnp.minimum(x, 0.0) - jnp.log(1.0 + jnp.exp(-jnp.abs(x)))


def _full(shape):
    return pl.BlockSpec(shape, lambda *_: (0,) * len(shape))


def _inproj_kernel(h_ref, g_ref, wl_ref, wh_ref, wm_ref, wf_ref, zl_ref, zh_ref, zm_ref, zf_ref):
    x = h_ref[...]
    u = (x * _rms(x, D_MODEL) * g_ref[...]).astype(BF16)
    zl_ref[...] = jnp.dot(u, wl_ref[...], preferred_element_type=F32)
    zh_ref[...] = jnp.dot(u, wh_ref[...], preferred_element_type=F32)
    zm_ref[...] = jnp.dot(u, wm_ref[...], preferred_element_type=F32)
    zf_ref[...] = jnp.dot(u, wf_ref[...], preferred_element_type=F32)


def _inproj(h, g, wl, wh, wm, wf, tm):
    R = h.shape[0]
    row = lambda n: pl.BlockSpec((tm, n), lambda i: (i, 0))
    return pl.pallas_call(
        _inproj_kernel,
        grid=(R // tm,),
        in_specs=[row(D_MODEL), _full((1, D_MODEL)), _full(wl.shape), _full(wh.shape),
                  _full(wm.shape), _full(wf.shape)],
        out_specs=[row(wl.shape[1]), row(wh.shape[1]), row(wm.shape[1]), row(wf.shape[1])],
        out_shape=[jax.ShapeDtypeStruct((R, w.shape[1]), F32) for w in (wl, wh, wm, wf)],
        compiler_params=_cparams("parallel"),
        name="inproj",
    )(h, g, wl, wh, wm, wf)


def _rglru_kernel(z_ref, cw_ref, cb_ref, wa_ref, ba_ref, wx_ref, bx_ref, lam_ref, gn_ref, y_ref,
                  xbuf, a_s, b_s, h_s, hst):
    W = GROUP_WIDTH
    tt = y_ref.shape[0]

    @pl.when(pl.program_id(1) == 0)
    def _():
        xbuf[0:8, :] = jnp.zeros((8, W), F32)
        hst[...] = jnp.zeros((1, W), F32)

    xa = z_ref[:, 0:W]
    ga = z_ref[:, W:2 * W]
    xbuf[8:8 + tt, :] = xa
    u = (cb_ref[...] + xbuf[5:5 + tt, :] * cw_ref[0:1, :] + xbuf[6:6 + tt, :] * cw_ref[1:2, :]
         + xbuf[7:7 + tt, :] * cw_ref[2:3, :] + xa * cw_ref[3:4, :])
    xbuf[0:8, :] = xbuf[tt:tt + 8, :]

    ub = u.astype(BF16)
    r = _sigmoid(jnp.dot(ub, wa_ref[...], preferred_element_type=F32) + ba_ref[...])
    i = _sigmoid(jnp.dot(ub, wx_ref[...], preferred_element_type=F32) + bx_ref[...])
    lam = lam_ref[...]
    softplus_neg_lam = jnp.maximum(-lam, 0.0) + jnp.log(1.0 + jnp.exp(-jnp.abs(lam)))
    a = jnp.exp((-LRU_C) * r * softplus_neg_lam)
    b = jnp.sqrt(1.0 - a * a) * (i * u)

    row = lax.broadcasted_iota(jnp.int32, (tt, W), 0) & 7
    for s in (1, 2, 4):
        ok = row >= s
        b = jnp.where(ok, a * pltpu.roll(b, s, 0) + b, b)
        a = jnp.where(ok, a * pltpu.roll(a, s, 0), a)
    a_s[...] = a
    b_s[...] = b

    def group(gi, hprev):
        sl = pl.ds(pl.multiple_of(gi * 8, 8), 8)
        hg = b_s[sl, :] + a_s[sl, :] * hprev
        h_s[sl, :] = hg
        return hg[7:8, :]

    hst[...] = lax.fori_loop(0, tt // 8, group, hst[...], unroll=8)

    c0 = math.sqrt(2.0 / math.pi)
    gelu = 0.5 * ga * (1.0 + jnp.tanh(c0 * (ga + 0.044715 * (ga * ga * ga))))
    y = h_s[...] * gelu
    y_ref[...] = (y * _rms(y, W) * gn_ref[...]).astype(BF16)


def _rglru(zl, cw, cb, wa, ba, wx, bx, lam, gn, B, tt):
    R = zl.shape[0]
    nt = R // B // tt
    W = GROUP_WIDTH
    return pl.pallas_call(
        _rglru_kernel,
        grid=(B, nt),
        in_specs=[pl.BlockSpec((tt, 2 * W), lambda b, t: (b * nt + t, 0)),
                  _full((4, W)), _full((1, W)), _full((W, W)), _full((1, W)), _full((W, W)),
                  _full((1, W)), _full((1, W)), _full((1, W))],
        out_specs=pl.BlockSpec((tt, W), lambda b, t: (b * nt + t, 0)),
        out_shape=jax.ShapeDtypeStruct((R, W), BF16),
        scratch_shapes=[pltpu.VMEM((tt + 8, W), F32), pltpu.VMEM((tt, W), F32),
                        pltpu.VMEM((tt, W), F32), pltpu.VMEM((tt, W), F32), pltpu.VMEM((1, W), F32)],
        compiler_params=_cparams("parallel", "arbitrary"),
        name="rglru",
    )(zl, cw, cb, wa, ba, wx, bx, lam, gn)


def _hgrn2_kernel(z_ref, lb_ref, gn_ref, y_ref, qd_s, ke_s, v_s, dec_s, o_s, st_s):
    W = GROUP_WIDTH
    tt = y_ref.shape[0]
    C = HG_CHUNK

    @pl.when(pl.program_id(1) == 0)
    def _():
        st_s[...] = jnp.zeros((W, W), F32)

    q = z_ref[:, 0:W]
    fz = z_ref[:, W:2 * W]
    v = z_ref[:, 2 * W:3 * W]
    g = z_ref[:, 3 * W:4 * W]
    lb = lb_ref[...]
    la = jnp.log(lb)
    lq = jnp.log(1.0 - lb) + _log_sigmoid(fz)
    logf = jnp.maximum(la, lq) + jnp.log(1.0 + jnp.exp(-jnp.abs(la - lq)))
    kin = 1.0 - jnp.exp(logf)

    rowc = lax.broadcasted_iota(jnp.int32, (tt, W), 0) & (C - 1)
    b = logf
    rev = logf
    s = 1
    while s < C:
        b = b + jnp.where(rowc >= s, pltpu.roll(b, s, 0), 0.0)
        rev = rev + jnp.where(rowc + s < C, pltpu.roll(rev, tt - s, 0), 0.0)
        s *= 2
    tail = rev - logf
    qd = q * jnp.exp(b)
    kd = (kin * jnp.exp(-b)).astype(BF16)
    vb = v.astype(BF16)
    qd_s[...] = qd.astype(BF16)
    ke_s[...] = (kin * jnp.exp(tail)).astype(BF16)
    v_s[...] = vb
    dec_s[...] = jnp.exp(b + tail)

    lane_head = lax.broadcasted_iota(jnp.int32, (1, W), 1) // (W // N_HEADS)

    rr = lax.broadcasted_iota(jnp.int32, (HG_BLOCK, HG_BLOCK), 0)
    cc = lax.broadcasted_iota(jnp.int32, (HG_BLOCK, HG_BLOCK), 1)
    amask = (rr // C == cc // C) & (cc <= rr)
    for jb in range(tt // HG_BLOCK):
        sl = slice(jb * HG_BLOCK, (jb + 1) * HG_BLOCK)
        qb, kb, vv = qd[sl], kd[sl], vb[sl]
        acc = jnp.zeros((HG_BLOCK, W), F32)
        for hh in range(N_HEADS):
            hm = lane_head == hh
            qh = jnp.where(hm, qb, 0.0).astype(BF16)
            att = lax.dot_general(qh, kb, NT_DIMS, preferred_element_type=F32)
            att = jnp.where(amask, att, 0.0).astype(BF16)
            acc = jnp.where(hm, jnp.dot(att, vv, preferred_element_type=F32), acc)
        o_s[sl, :] = acc

    def chunk(c, carry):
        sl = pl.ds(pl.multiple_of(c * C, C), C)
        qc = qd_s[sl, :]
        st = st_s[...]
        qbd = jnp.concatenate([jnp.where(lane_head == hh, qc, jnp.zeros_like(qc))
                               for hh in range(N_HEADS)], axis=0)
        res = lax.dot_general(qbd, st.astype(BF16), NT_DIMS, preferred_element_type=F32)
        oi = jnp.zeros((C, W), F32)
        for hh in range(N_HEADS):
            oi = jnp.where(lane_head == hh, res[hh * C:(hh + 1) * C, :], oi)
        o_s[sl, :] = o_s[sl, :] + oi
        upd = lax.dot_general(v_s[sl, :], ke_s[sl, :], TN_DIMS, preferred_element_type=F32)
        st_s[...] = st * dec_s[pl.ds(c * C, 1), :] + upd
        return carry

    lax.fori_loop(0, tt // C, chunk, 0, unroll=math.gcd(tt // C, 8))

    y = o_s[...] * (g * _sigmoid(g))
    y_ref[...] = (y * _rms(y, W) * gn_ref[...]).astype(BF16)


def _hgrn2(zh, lb, gn, B, tt):
    R = zh.shape[0]
    nt = R // B // tt
    W = GROUP_WIDTH
    return pl.pallas_call(
        _hgrn2_kernel,
        grid=(B, nt),
        in_specs=[pl.BlockSpec((tt, 4 * W), lambda b, t: (b * nt + t, 0)), _full((1, W)), _full((1, W))],
        out_specs=pl.BlockSpec((tt, W), lambda b, t: (b * nt + t, 0)),
        out_shape=jax.ShapeDtypeStruct((R, W), BF16),
        scratch_shapes=[pltpu.VMEM((tt, W), BF16), pltpu.VMEM((tt, W), BF16), pltpu.VMEM((tt, W), BF16),
                        pltpu.VMEM((tt, W), F32), pltpu.VMEM((tt, W), F32), pltpu.VMEM((W, W), F32)],
        compiler_params=_cparams("parallel", "arbitrary"),
        name="hgrn2",
    )(zh, lb, gn)


def _rope(x, cos, s_lo, s_hi):
    return x * cos + pltpu.roll(x, 16, 1) * s_hi + pltpu.roll(x, HEAD_PAD - 16, 1) * s_lo


def _mla_prep_kernel(z_ref, gq_ref, wuq_ref, gkv_ref, wuk_ref, wuv_ref, gqn_ref, gkn_ref,
                     cos_ref, slo_ref, shi_ref, q_ref, k_ref, v_ref):
    P = HEAD_PAD
    ckv = z_ref[:, 0:P]
    krb = z_ref[:, P:2 * P]
    cq = z_ref[:, 2 * P:4 * P]
    qn = (cq * _rms(cq, MLA_Q_RANK) * gq_ref[...]).astype(BF16)
    kvn = (ckv * _rms(ckv, MLA_KV_RANK) * gkv_ref[...]).astype(BF16)
    q = jnp.dot(qn, wuq_ref[...], preferred_element_type=F32)
    kn = jnp.dot(kvn, wuk_ref[...], preferred_element_type=F32)
    vv = jnp.dot(kvn, wuv_ref[...], preferred_element_type=F32)
    cos, slo, shi = cos_ref[...], slo_ref[...], shi_ref[...]
    lane = lax.broadcasted_iota(jnp.int32, (1, P), 1)
    qscale = (MLA_QK ** -0.5) * LOG2E
    for hh in range(N_HEADS):
        sl = slice(hh * P, (hh + 1) * P)
        qh = q[:, sl]
        qh = qh * _rms(qh, MLA_QK) * gqn_ref[...]
        q_ref[:, sl] = (_rope(qh, cos, slo, shi) * qscale).astype(BF16)
        kh = kn[:, sl] + krb
        kh = kh * _rms(kh, MLA_QK) * gkn_ref[...]
        k_ref[:, sl] = _rope(kh, cos, slo, shi).astype(BF16)
        v_ref[:, sl] = jnp.where(lane == ONES_LANE, 1.0, vv[:, sl]).astype(BF16)


def _mla_prep(zm, gq, wuq, gkv, wuk, wuv, gqn, gkn, cos, slo, shi, tm):
    R = zm.shape[0]
    P = HEAD_PAD
    nt = cos.shape[0] // tm
    row = lambda n: pl.BlockSpec((tm, n), lambda i: (i, 0))
    tab = pl.BlockSpec((tm, P), lambda i: (i % nt, 0))
    out = jax.ShapeDtypeStruct((R, N_HEADS * P), BF16)
    return pl.pallas_call(
        _mla_prep_kernel,
        grid=(R // tm,),
        in_specs=[row(4 * P), _full((1, 2 * P)), _full(wuq.shape), _full((1, P)), _full(wuk.shape),
                  _full(wuv.shape), _full((1, P)), _full((1, P)), tab, tab, tab],
        out_specs=[row(N_HEADS * P)] * 3,
        out_shape=[out, out, out],
        compiler_params=_cparams("parallel"),
        name="mla_prep",
    )(zm, gq, wuq, gkv, wuk, wuv, gqn, gkn, cos, slo, shi)


STAT_ROWS = 8
STAT_C_FIRST, STAT_C_LAST, STAT_Q2, STAT_K2 = range(4)


def _fox_prep_kernel(z_ref, bf_ref, gqn_ref, gkn_ref, q_ref, k_ref, v_ref, stat_ref, carry):
    P = HEAD_PAD
    tt = q_ref.shape[0]

    @pl.when(pl.program_id(1) == 0)
    def _():
        carry[...] = jnp.zeros((1, P), F32)

    c = _log_sigmoid(z_ref[:, 3 * N_HEADS * P:] + bf_ref[...])
    row = lax.broadcasted_iota(jnp.int32, (tt, P), 0)
    s = 1
    while s < tt:
        c = c + jnp.where(row >= s, pltpu.roll(c, s, 0), 0.0)
        s *= 2
    c = c + carry[...]
    carry[...] = c[tt - 1:tt, :]
    c = c * LOG2E
    c1 = c.astype(BF16).astype(F32)
    c2 = (c - c1).astype(BF16).astype(F32)
    c3 = c - c1 - c2

    lane = lax.broadcasted_iota(jnp.int32, (1, P), 1)
    qscale = (FOX_HD ** -0.5) * LOG2E
    q2 = jnp.zeros((1, P), F32)
    k2 = jnp.zeros((1, P), F32)
    for hh in range(N_HEADS):
        p1 = c1[:, hh:hh + 1]
        p2 = c2[:, hh:hh + 1]
        p3 = c3[:, hh:hh + 1]
        qh = z_ref[:, hh * P:(hh + 1) * P]
        qh = qh * _rms(qh, FOX_HD) * gqn_ref[...] * qscale
        q2 = jnp.where(lane == hh, jnp.max(jnp.sum(qh * qh, axis=-1, keepdims=True), axis=0, keepdims=True), q2)
        qh = jnp.where(lane == 64, p1, jnp.where(lane == 65, p2, jnp.where(lane == 66, p3, qh)))
        qh = jnp.where((lane >= 67) & (lane < 70), 1.0, qh)
        q_ref[:, hh * P:(hh + 1) * P] = qh.astype(BF16)
        kh = z_ref[:, (N_HEADS + hh) * P:(N_HEADS + hh + 1) * P]
        kh = kh * _rms(kh, FOX_HD) * gkn_ref[...]
        k2 = jnp.where(lane == hh, jnp.max(jnp.sum(kh * kh, axis=-1, keepdims=True), axis=0, keepdims=True), k2)
        kh = jnp.where(lane == 67, -p1, jnp.where(lane == 68, -p2, jnp.where(lane == 69, -p3, kh)))
        kh = jnp.where((lane >= 64) & (lane < 67), 1.0, kh)
        k_ref[:, hh * P:(hh + 1) * P] = kh.astype(BF16)
        vh = z_ref[:, (2 * N_HEADS + hh) * P:(2 * N_HEADS + hh + 1) * P]
        v_ref[:, hh * P:(hh + 1) * P] = jnp.where(lane == ONES_LANE, 1.0, vh).astype(BF16)

    srow = lax.broadcasted_iota(jnp.int32, (STAT_ROWS, P), 0)
    stat = jnp.zeros((STAT_ROWS, P), F32)
    for rr, val in ((STAT_C_FIRST, c[0:1, :]), (STAT_C_LAST, c[tt - 1:tt, :]), (STAT_Q2, q2), (STAT_K2, k2)):
        stat = jnp.where(srow == rr, val, stat)
    stat_ref[...] = stat


def _fox_prep(zf, bf, gqn, gkn, B, tt):
    R = zf.shape[0]
    nt = R // B // tt
    P = HEAD_PAD
    blk = lambda n: pl.BlockSpec((tt, n), lambda b, t: (b * nt + t, 0))
    out = jax.ShapeDtypeStruct((R, N_HEADS * P), BF16)
    return pl.pallas_call(
        _fox_prep_kernel,
        grid=(B, nt),
        in_specs=[blk(zf.shape[1]), _full((1, P)), _full((1, P)), _full((1, P))],
        out_specs=[blk(N_HEADS * P)] * 3 + [pl.BlockSpec((STAT_ROWS, P), lambda b, t: (b * nt + t, 0))],
        out_shape=[out, out, out, jax.ShapeDtypeStruct((B * nt * STAT_ROWS, P), F32)],
        scratch_shapes=[pltpu.VMEM((1, P), F32)],
        compiler_params=_cparams("parallel", "arbitrary"),
        name="fox_prep",
    )(zf, bf, gqn, gkn)


SKIP_LOG2_MARGIN = 40.0
NORM_SLACK = 1.02


def _fox_first_chunk(stats, B, nt):
    st = stats.reshape(B, nt, STAT_ROWS, HEAD_PAD)[..., :N_HEADS]
    c_first, c_last = st[:, :, STAT_C_FIRST], st[:, :, STAT_C_LAST]
    bound = jnp.sqrt(jnp.max(st[:, :, STAT_Q2], axis=1) * jnp.max(st[:, :, STAT_K2], axis=1)) * NORM_SLACK
    gap = 2.0 * bound[:, None, None, :] + c_first[:, :, None, :] - c_last[:, None, :, :]
    earlier = jnp.arange(nt)[None, :] < jnp.arange(nt)[:, None]
    skip = (gap < -SKIP_LOG2_MARGIN) & earlier[None, :, :, None]
    return jnp.min(jnp.sum(skip, axis=2), axis=-1).astype(jnp.int32).reshape(-1)


def _attn_kernel(first_ref, q_ref, k_ref, v_ref, gn_ref, y_ref, m_s, acc_s, o_s, *, chunk_causal, tk):
    P = HEAD_PAD
    tq = q_ref.shape[0]
    lp = k_ref.shape[0]
    blk = pl.program_id(0) * pl.num_programs(1) + pl.program_id(1)
    q0 = pl.program_id(1) * tq

    qpos = q0 + lax.broadcasted_iota(jnp.int32, (tq, 1), 0)
    if chunk_causal:
        qlim = N_META + CHUNK * ((qpos + (CHUNK - N_META)) // CHUNK)
        reach = N_META
    else:
        qlim = qpos + 1
        reach = 0
    n_full = (q0 + reach) // tk
    n_diag = (jnp.minimum(q0 + tq, lp) + tk - 1) // tk

    def visit(k0, width, masked):
        ks = pl.ds(pl.multiple_of(k0, SEQ_ALIGN), width)
        if masked:
            vis = (k0 + lax.broadcasted_iota(jnp.int32, (1, width), 1)) < qlim
        for hh in range(N_HEADS):
            hs = slice(hh * P, (hh + 1) * P)
            s = lax.dot_general(q_ref[:, hs], k_ref[ks, hs], NT_DIMS, preferred_element_type=F32)
            if masked:
                s = jnp.where(vis, s, -jnp.inf)
            tiles = [s[:, c * P:(c + 1) * P] for c in range(width // P)]
            mx = tiles[0]
            for t in tiles[1:]:
                mx = jnp.maximum(mx, t)
            m_old = m_s[hh]
            m_new = jnp.maximum(m_old, jnp.max(mx, axis=-1, keepdims=True))
            p = jnp.concatenate([jnp.exp2((t - m_new).astype(BF16)) for t in tiles], axis=1)
            acc_s[hh] = jnp.exp2(m_old - m_new) * acc_s[hh] + jnp.dot(
                p, v_ref[ks, hs], preferred_element_type=F32)
            m_s[hh] = m_new

    m_s[...] = jnp.full(m_s.shape, -jnp.inf, F32)
    acc_s[...] = jnp.zeros(acc_s.shape, F32)

    def full_body(j, carry):
        visit(j * tk, tk, False)
        return carry

    def masked_body(j, carry):
        visit(j * tk, tk, True)
        return carry

    lax.fori_loop(first_ref[blk], n_full, full_body, 0)
    lax.fori_loop(n_full, n_diag, masked_body, 0)
    if chunk_causal:
        @pl.when(q0 + tq < lp)
        def _():
            visit(q0 + tq, SEQ_ALIGN, True)

    hd = GROUP_WIDTH // N_HEADS
    for hh in range(N_HEADS):
        acc = acc_s[hh]
        o_s[:, hh * hd:(hh + 1) * hd] = acc[:, 0:hd] / acc[:, ONES_LANE:ONES_LANE + 1]
    y = o_s[...]
    y_ref[...] = (y * _rms(y, GROUP_WIDTH) * gn_ref[...]).astype(BF16)


def _attention(first_chunk, q, k, v, gn, B, tq, tk, chunk_causal):
    R = q.shape[0]
    lp = R // B
    nq = lp // tq
    P = HEAD_PAD
    W = GROUP_WIDTH
    kv_spec = pl.BlockSpec((lp, N_HEADS * P), lambda b, i, fc: (b, 0))
    return pl.pallas_call(
        functools.partial(_attn_kernel, chunk_causal=chunk_causal, tk=tk),
        grid_spec=pltpu.PrefetchScalarGridSpec(
            num_scalar_prefetch=1,
            grid=(B, nq),
            in_specs=[pl.BlockSpec((tq, N_HEADS * P), lambda b, i, fc: (b * nq + i, 0)), kv_spec, kv_spec,
                      pl.BlockSpec((1, W), lambda b, i, fc: (0, 0))],
            out_specs=pl.BlockSpec((tq, W), lambda b, i, fc: (b * nq + i, 0)),
            scratch_shapes=[pltpu.VMEM((N_HEADS, tq, P), F32), pltpu.VMEM((N_HEADS, tq, P), F32),
                            pltpu.VMEM((tq, W), F32)],
        ),
        out_shape=jax.ShapeDtypeStruct((R, W), BF16),
        compiler_params=_cparams("parallel", "arbitrary"),
        name="mla_attn" if chunk_causal else "fox_attn",
    )(first_chunk, q, k, v, gn)


ROUTE_E1, ROUTE_E2, ROUTE_R1, ROUTE_R2, ROUTE_G1, ROUTE_G2 = range(6)


def _outproj_kernel(h_ref, ya_ref, yb_ref, yc_ref, yd_ref, w_ref, g_ref, *rest, with_router):
    W = GROUP_WIDTH
    if with_router:
        wr_ref, hn_ref, u_ref, route_ref, cnt_ref = rest
    else:
        hn_ref, u_ref = rest
    acc = h_ref[...]
    for gi, y_ref in enumerate((ya_ref, yb_ref, yc_ref, yd_ref)):
        acc = acc + jnp.dot(y_ref[...], w_ref[gi * W:(gi + 1) * W, :], preferred_element_type=F32)
    hn_ref[...] = acc
    u = acc * _rms(acc, D_MODEL) * g_ref[...]
    if not with_router:
        u_ref[...] = u.astype(BF16)
        return
    u_ref[...] = u
    tm = u.shape[0]

    @pl.when(pl.program_id(0) == 0)
    def _():
        cnt_ref[...] = jnp.zeros(cnt_ref.shape, F32)

    u_hi = u.astype(BF16)
    u_lo = (u - u_hi.astype(F32)).astype(BF16)
    logits = (jnp.dot(u_hi, wr_ref[0], preferred_element_type=F32)
              + jnp.dot(u_lo, wr_ref[0], preferred_element_type=F32)
              + jnp.dot(u_hi, wr_ref[1], preferred_element_type=F32))
    lane = lax.broadcasted_iota(jnp.int32, logits.shape, 1).astype(F32)
    lg = jnp.where(lane < N_EXPERTS, logits, -jnp.inf)
    m1 = jnp.max(lg, axis=-1, keepdims=True)
    i1 = jnp.min(jnp.where(lg == m1, lane, 1e9), axis=-1, keepdims=True)
    lg2 = jnp.where(lane == i1, -jnp.inf, lg)
    m2 = jnp.max(lg2, axis=-1, keepdims=True)
    i2 = jnp.min(jnp.where(lg2 == m2, lane, 1e9), axis=-1, keepdims=True)
    e = jnp.exp(m2 - m1)
    g1 = 1.0 / (1.0 + e)
    picks = jnp.where(lane == i1, 1.0, 0.0) + jnp.where(lane == i2, 1.0, 0.0)
    earlier = (lax.broadcasted_iota(jnp.int32, (tm, tm), 0) > lax.broadcasted_iota(jnp.int32, (tm, tm), 1))
    base = cnt_ref[...] + jnp.dot(earlier.astype(BF16), picks.astype(BF16), preferred_element_type=F32)
    r1 = jnp.sum(jnp.where(lane == i1, base, 0.0), axis=-1, keepdims=True)
    r2 = jnp.sum(jnp.where(lane == i2, base, 0.0), axis=-1, keepdims=True)
    cnt_ref[...] = cnt_ref[...] + jnp.sum(picks, axis=0, keepdims=True)
    rec = jnp.zeros(logits.shape, F32)
    for ln, val in ((ROUTE_E1, i1), (ROUTE_E2, i2), (ROUTE_R1, r1), (ROUTE_R2, r2),
                    (ROUTE_G1, g1), (ROUTE_G2, e * g1)):
        rec = jnp.where(lane == ln, val, rec)
    route_ref[...] = rec


def _outproj(h, ya, yb, yc, yd, w, g, wr, tm):
    R = h.shape[0]
    W = GROUP_WIDTH
    with_router = wr is not None
    row = lambda n: pl.BlockSpec((tm, n), lambda i: (i, 0))
    in_specs = [row(D_MODEL), row(W), row(W), row(W), row(W), _full(w.shape), _full((1, D_MODEL))]
    out_specs = [row(D_MODEL), row(D_MODEL)]
    out_shape = [jax.ShapeDtypeStruct((R, D_MODEL), F32),
                 jax.ShapeDtypeStruct((R, D_MODEL), F32 if with_router else BF16)]
    args = [h, ya, yb, yc, yd, w, g]
    if with_router:
        in_specs.append(_full(wr.shape))
        out_specs += [row(wr.shape[-1]), _full((1, wr.shape[-1]))]
        out_shape += [jax.ShapeDtypeStruct((R, wr.shape[-1]), F32), jax.ShapeDtypeStruct((1, wr.shape[-1]), F32)]
        args.append(wr)
    return pl.pallas_call(
        functools.partial(_outproj_kernel, with_router=with_router),
        grid=(R // tm,),
        in_specs=in_specs, out_specs=out_specs, out_shape=out_shape,
        compiler_params=_cparams("arbitrary" if with_router else "parallel"),
        name="outproj_router" if with_router else "outproj",
    )(*args)


def _ffn_kernel(u_ref, h_ref, wg_ref, wu_ref, wd_ref, o_ref):
    @pl.when(pl.program_id(1) == 0)
    def _():
        o_ref[...] = h_ref[...]

    u = u_ref[...]
    a = jnp.dot(u, wg_ref[...], preferred_element_type=F32)
    b = jnp.dot(u, wu_ref[...], preferred_element_type=F32)
    hid = (a * _sigmoid(a) * b).astype(BF16)
    o_ref[...] += jnp.dot(hid, wd_ref[...], preferred_element_type=F32)


def _ffn(u, h, wg, wu, wd, tm, tf):
    R = h.shape[0]
    dff = wg.shape[1]
    return pl.pallas_call(
        _ffn_kernel,
        grid=(R // tm, dff // tf),
        in_specs=[pl.BlockSpec((tm, D_MODEL), lambda i, f: (i, 0)),
                  pl.BlockSpec((tm, D_MODEL), lambda i, f: (i, 0)),
                  pl.BlockSpec((D_MODEL, tf), lambda i, f: (0, f)),
                  pl.BlockSpec((D_MODEL, tf), lambda i, f: (0, f)),
                  pl.BlockSpec((tf, D_MODEL), lambda i, f: (f, 0))],
        out_specs=pl.BlockSpec((tm, D_MODEL), lambda i, f: (i, 0)),
        out_shape=jax.ShapeDtypeStruct((R, D_MODEL), F32),
        compiler_params=_cparams("parallel", "arbitrary"),
        name="ffn",
    )(u, h, wg, wu, wd)


def _row_copy(src_ref, src_row, dst_ref, dst_row, sem):
    return pltpu.make_async_copy(src_ref.at[pl.ds(src_row, 1)], dst_ref.at[pl.ds(dst_row, 1)], sem)


def _dispatch_kernel(pos1_ref, pos2_ref, u_ref, xs_in_ref, xs_ref, sem):
    del xs_in_ref
    tm = u_ref.shape[0]
    t0 = pl.program_id(0) * tm

    def issue(r, carry):
        _row_copy(u_ref, r, xs_ref, pos1_ref[t0 + r], sem).start()
        _row_copy(u_ref, r, xs_ref, pos2_ref[t0 + r], sem).start()
        return carry

    lax.fori_loop(0, tm, issue, 0, unroll=8)

    def drain(r, carry):
        _row_copy(u_ref, r, xs_ref, 0, sem).wait()
        _row_copy(u_ref, r, xs_ref, 0, sem).wait()
        return carry

    lax.fori_loop(0, tm, drain, 0, unroll=8)


def _dispatch(pos1, pos2, u, n_slots, tm):
    R = u.shape[0]
    xs0 = jnp.zeros((n_slots, D_MODEL), F32)
    return pl.pallas_call(
        _dispatch_kernel,
        grid_spec=pltpu.PrefetchScalarGridSpec(
            num_scalar_prefetch=2,
            grid=(R // tm,),
            in_specs=[pl.BlockSpec((tm, D_MODEL), lambda i, p1, p2: (i, 0)),
                      pl.BlockSpec(memory_space=pl.ANY)],
            out_specs=pl.BlockSpec(memory_space=pl.ANY),
            scratch_shapes=[pltpu.SemaphoreType.DMA],
        ),
        out_shape=jax.ShapeDtypeStruct((n_slots, D_MODEL), F32),
        input_output_aliases={3: 0},
        compiler_params=_cparams("arbitrary"),
        name="moe_dispatch",
    )(pos1, pos2, u, xs0)


def _expert_ffn_kernel(te_ref, nt_ref, x_ref, wg_ref, wu_ref, wd_ref, y_ref, xb_s):
    f = pl.program_id(1)
    used = pl.program_id(0) < nt_ref[0]

    @pl.when(jnp.logical_not(used) & (f == 0))
    def _():
        y_ref[...] = jnp.zeros(y_ref.shape, F32)

    @pl.when(used)
    def _():
        @pl.when(f == 0)
        def _():
            xb_s[...] = x_ref[...].astype(BF16)

        x = xb_s[...]
        a = jnp.dot(x, wg_ref[...], preferred_element_type=F32)
        b = jnp.dot(x, wu_ref[...], preferred_element_type=F32)
        hid = (a * _sigmoid(a) * b).astype(BF16)
        out = jnp.dot(hid, wd_ref[...], preferred_element_type=F32)

        @pl.when(f == 0)
        def _():
            y_ref[...] = out

        @pl.when(f > 0)
        def _():
            y_ref[...] += out


def _expert_ffn(tile_expert, n_tiles, xs, wg, wu, wd, tm, tf):
    n_slots = xs.shape[0]
    dff = wg.shape[2]
    row_map = lambda i, f, te, nt: (jnp.minimum(i, nt[0] - 1), 0)
    out_map = lambda i, f, te, nt: (i, 0)
    return pl.pallas_call(
        _expert_ffn_kernel,
        grid_spec=pltpu.PrefetchScalarGridSpec(
            num_scalar_prefetch=2,
            grid=(n_slots // tm, dff // tf),
            in_specs=[pl.BlockSpec((tm, D_MODEL), row_map),
                      pl.BlockSpec((None, D_MODEL, tf), lambda i, f, te, nt: (te[i], 0, f)),
                      pl.BlockSpec((None, D_MODEL, tf), lambda i, f, te, nt: (te[i], 0, f)),
                      pl.BlockSpec((None, tf, D_MODEL), lambda i, f, te, nt: (te[i], f, 0))],
            out_specs=pl.BlockSpec((tm, D_MODEL), out_map),
            scratch_shapes=[pltpu.VMEM((tm, D_MODEL), BF16)],
        ),
        out_shape=jax.ShapeDtypeStruct((n_slots, D_MODEL), F32),
        compiler_params=_cparams("arbitrary", "arbitrary"),
        name="moe_expert_ffn",
    )(tile_expert, n_tiles, xs, wg, wu, wd)


def _combine_kernel(pos1_ref, pos2_ref, h_ref, route_ref, ys_ref, o_ref, y1_s, y2_s, sem):
    tm = h_ref.shape[0]
    t0 = pl.program_id(0) * tm

    def issue(r, carry):
        _row_copy(ys_ref, pos1_ref[t0 + r], y1_s, r, sem).start()
        _row_copy(ys_ref, pos2_ref[t0 + r], y2_s, r, sem).start()
        return carry

    lax.fori_loop(0, tm, issue, 0, unroll=8)

    def drain(r, carry):
        _row_copy(ys_ref, 0, y1_s, r, sem).wait()
        _row_copy(ys_ref, 0, y2_s, r, sem).wait()
        return carry

    lax.fori_loop(0, tm, drain, 0, unroll=8)
    rec = route_ref[...]
    g1 = rec[:, ROUTE_G1:ROUTE_G1 + 1]
    g2 = rec[:, ROUTE_G2:ROUTE_G2 + 1]
    o_ref[...] = h_ref[...] + g1 * y1_s[...] + g2 * y2_s[...]


def _combine(pos1, pos2, h, route, ys, tm):
    R = h.shape[0]
    row = lambda n: pl.BlockSpec((tm, n), lambda i, p1, p2: (i, 0))
    return pl.pallas_call(
        _combine_kernel,
        grid_spec=pltpu.PrefetchScalarGridSpec(
            num_scalar_prefetch=2,
            grid=(R // tm,),
            in_specs=[row(D_MODEL), row(route.shape[1]), pl.BlockSpec(memory_space=pl.ANY)],
            out_specs=row(D_MODEL),
            scratch_shapes=[pltpu.VMEM((tm, D_MODEL), F32), pltpu.VMEM((tm, D_MODEL), F32),
                            pltpu.SemaphoreType.DMA],
        ),
        out_shape=jax.ShapeDtypeStruct((R, D_MODEL), F32),
        compiler_params=_cparams("arbitrary"),
        name="moe_combine",
    )(pos1, pos2, h, route, ys)


def _moe(u, h, route, counts, wg, wu, wd, tm_rows, tm_expert, tf):
    R = h.shape[0]
    ne = wg.shape[0]
    cnt = counts[0, :ne].astype(jnp.int32)
    padded = -(-cnt // tm_expert) * tm_expert
    ends = jnp.cumsum(padded)
    offs = ends - padded
    col = lambda ln: route[:, ln].astype(jnp.int32)
    pos1 = jnp.take(offs, col(ROUTE_E1)) + col(ROUTE_R1)
    pos2 = jnp.take(offs, col(ROUTE_E2)) + col(ROUTE_R2)
    n_slots = (2 * R // tm_expert + ne) * tm_expert
    n_tiles = (ends[-1] // tm_expert).reshape(1)
    tile_start = jnp.arange(n_slots // tm_expert, dtype=jnp.int32) * tm_expert
    tile_expert = jnp.sum(tile_start[:, None] >= ends[None, :], axis=1).astype(jnp.int32)
    last_expert = jnp.sum((ends[-1] - 1) >= ends).astype(jnp.int32)
    tile_expert = jnp.minimum(tile_expert, last_expert)

    xs = _dispatch(pos1, pos2, u, n_slots, tm_rows)
    ys = _expert_ffn(tile_expert, n_tiles, xs, wg, wu, wd, tm_expert, tf)
    return _combine(pos1, pos2, h, route, ys, tm_rows)


def _pad_heads(w, real):
    rows = w.shape[0]
    w = w.reshape(rows, N_HEADS, real)
    return jnp.pad(w, ((0, 0), (0, 0), (0, HEAD_PAD - real))).reshape(rows, N_HEADS * HEAD_PAD)


def _pad_cols(w, n):
    return jnp.pad(w, ((0, 0), (0, n - w.shape[1])))


def _row(v, n=None):
    v = v.reshape(1, -1).astype(F32)
    return v if n is None else _pad_cols(v, n)


def _block_diag(w):
    nb, c, d = w.shape
    eye = jnp.eye(nb, dtype=w.dtype)
    return (eye[:, None, :, None] * w[:, :, None, :]).reshape(nb * c, nb * d)


def _rope_tables(lp):
    half = MLA_ROPE // 2
    inv = ROPE_THETA ** (-jnp.arange(half, dtype=F32) / half)
    ang = jnp.arange(lp, dtype=jnp.int32).astype(F32)[:, None] * inv[None, :]
    cos, sin = jnp.cos(ang), jnp.sin(ang)
    one = jnp.ones((lp, MLA_NOPE), F32)
    zero = jnp.zeros((lp, MLA_NOPE), F32)
    zh = jnp.zeros((lp, half), F32)
    tail1 = jnp.ones((lp, HEAD_PAD - MLA_QK), F32)
    tail0 = jnp.zeros((lp, HEAD_PAD - MLA_QK), F32)
    cos_t = jnp.concatenate([one, cos, cos, tail1], axis=1)
    s_lo = jnp.concatenate([zero, -sin, zh, tail0], axis=1)
    s_hi = jnp.concatenate([zero, zh, sin, tail0], axis=1)
    return cos_t, s_lo, s_hi


def kernel(x, meta, norm1_g, norm2_g, w_in, w_out, out_norm_g, lru_conv_w, lru_conv_b, lru_wa, lru_ba, lru_wx, lru_bx, lru_lambda, hg_lb_logits, mla_gq, mla_w_uq, mla_gkv, mla_w_ukv, mla_gqn, mla_gkn, fox_gqn, fox_gkn, fox_bf, ffn_w_gate, ffn_w_up, ffn_w_down, moe_w_router, moe_w_gate, moe_w_up, moe_w_down):
    B, S, _ = x.shape
    depth = w_in.shape[0]
    L = N_META + S
    lp = -(-L // SEQ_ALIGN) * SEQ_ALIGN
    R = B * lp
    W = GROUP_WIDTH
    P = HEAD_PAD

    tt = _tile(lp, 640)
    tq = tk = tt

    h = jnp.concatenate([jnp.broadcast_to(meta[None].astype(x.dtype), (B, N_META, D_MODEL)), x,
                         jnp.zeros((B, lp - L, D_MODEL), x.dtype)], axis=1).reshape(R, D_MODEL)
    cos_t, s_lo, s_hi = _rope_tables(lp)
    lb_cum = jnp.cumsum(jax.nn.softmax(hg_lb_logits.astype(F32), axis=0), axis=0)

    o = 0
    offs = []
    for n in (W, W, W, W, W, W, MLA_Q_RANK, MLA_KV_RANK, MLA_ROPE, W, W, W, N_HEADS):
        offs.append(o)
        o += n
    (o_xa, _, o_hq, _, _, _, o_cq, o_ckv, o_kr, o_fq, o_fk, o_fv, o_ff) = offs

    for l in range(depth):
        w = w_in[l]
        wl = w[:, o_xa:o_xa + 2 * W].astype(BF16)
        wh = w[:, o_hq:o_hq + 4 * W].astype(BF16)
        wm = jnp.concatenate([
            w[:, o_ckv:o_ckv + MLA_KV_RANK],
            jnp.zeros((D_MODEL, MLA_NOPE), F32), w[:, o_kr:o_kr + MLA_ROPE],
            jnp.zeros((D_MODEL, P - MLA_QK), F32),
            _pad_cols(w[:, o_cq:o_cq + MLA_Q_RANK], 2 * P)], axis=1).astype(BF16)
        wf = jnp.concatenate([
            _pad_heads(w[:, o_fq:o_fq + W], FOX_HD), _pad_heads(w[:, o_fk:o_fk + W], FOX_HD),
            _pad_heads(w[:, o_fv:o_fv + W], FOX_HD), _pad_cols(w[:, o_ff:o_ff + N_HEADS], P)],
            axis=1).astype(BF16)
        zl, zh, zm, zf = _inproj(h, _row(norm1_g[l]), wl, wh, wm, wf, tt)

        gn = out_norm_g[l].astype(F32)
        ya = _rglru(zl, lru_conv_w[l].astype(F32), _row(lru_conv_b[l]),
                    _block_diag(lru_wa[l]).astype(BF16), _row(lru_ba[l]),
                    _block_diag(lru_wx[l]).astype(BF16), _row(lru_bx[l]),
                    _row(lru_lambda[l]), _row(gn[0:W]), B, tt)
        yb = _hgrn2(zh, _row(lb_cum[l] - lb_cum[0]), _row(gn[W:2 * W]), B, tt)

        wuq = jnp.pad(_pad_heads(mla_w_uq[l], MLA_QK), ((0, 2 * P - MLA_Q_RANK), (0, 0))).astype(BF16)
        wukv = mla_w_ukv[l].reshape(MLA_KV_RANK, N_HEADS, MLA_NOPE + MLA_V)
        wuk = _pad_heads(wukv[:, :, :MLA_NOPE].reshape(MLA_KV_RANK, -1), MLA_NOPE).astype(BF16)
        wuv = _pad_heads(wukv[:, :, MLA_NOPE:].reshape(MLA_KV_RANK, -1), MLA_V).astype(BF16)
        q, k, v = _mla_prep(zm, _row(mla_gq[l], 2 * P), wuq, _row(mla_gkv[l]), wuk, wuv,
                            _row(mla_gqn[l], P), _row(mla_gkn[l], P), cos_t, s_lo, s_hi, tt)
        visit_all = jnp.zeros((B * (lp // tq),), jnp.int32)
        yc = _attention(visit_all, q, k, v, _row(gn[2 * W:3 * W]), B, tq, tk, True)

        q, k, v, stats = _fox_prep(zf, _row(fox_bf[l], P), _row(fox_gqn[l], P), _row(fox_gkn[l], P), B, tt)
        yd = _attention(_fox_first_chunk(stats, B, lp // tt), q, k, v, _row(gn[3 * W:4 * W]), B, tq, tk, False)

        wo = w_out[l].astype(BF16)
        if l % 2 == 0:
            hn, u2 = _outproj(h, ya, yb, yc, yd, wo, _row(norm2_g[l]), None, tt)
            j = l // 2
            h = _ffn(u2, hn, ffn_w_gate[j].astype(BF16), ffn_w_up[j].astype(BF16),
                     ffn_w_down[j].astype(BF16), tt, 1408)
        else:
            j = l // 2
            wr = _pad_cols(moe_w_router[j].astype(F32), P)
            wr_hi = wr.astype(BF16)
            wr = jnp.stack([wr_hi, (wr - wr_hi.astype(F32)).astype(BF16)])
            hn, u2, route, counts = _outproj(h, ya, yb, yc, yd, wo, _row(norm2_g[l]), wr, tt)
            h = _moe(u2, hn, route, counts, moe_w_gate[j].astype(BF16), moe_w_up[j].astype(BF16),
                     moe_w_down[j].astype(BF16), tt, 512, 1792)
    return h.reshape(B, lp, D_MODEL)[:, N_META:L]
```

```python
import functools
import math

import jax
import jax.numpy as jnp
from jax import lax
from jax.experimental import pallas as pl
from jax.experimental.pallas import tpu as pltpu

F32 = jnp.float32
BF16 = jnp.bfloat16

D_MODEL = 1024
N_META = 16
CHUNK = 64
SEQ_ALIGN = 128
EPS = 1e-6
GROUP_WIDTH = 256
N_HEADS = 4
HEAD_PAD = 128
LRU_C = 8.0
HG_CHUNK = 16
HG_BLOCK = 128
MLA_NOPE, MLA_ROPE, MLA_V = 64, 32, 64
MLA_QK = MLA_NOPE + MLA_ROPE
MLA_Q_RANK, MLA_KV_RANK = 192, 128
ROPE_THETA = 10000.0
FOX_HD = 64
N_EXPERTS = 8
LOG2E = 1.4426950408889634
ONES_LANE = 64
VMEM_LIMIT = 56 * 1024 * 1024

NT_DIMS = (((1,), (1,)), ((), ()))
TN_DIMS = (((0,), (0,)), ((), ()))


def _cparams(*sem):
    return pltpu.CompilerParams(dimension_semantics=sem, vmem_limit_bytes=VMEM_LIMIT)


def _tile(n, pref, align=SEQ_ALIGN):
    best = None
    for t in range(align, min(n, pref) + 1, align):
        if n % t == 0:
            best = t
    assert best is not None, (n, pref, align)
    return best


def _rms(x, width):
    return lax.rsqrt(jnp.sum(x * x, axis=-1, keepdims=True) * (1.0 / width) + EPS)


def _row_ssq(x):
    ones = jnp.ones((x.shape[1], HEAD_PAD), BF16)
    return jnp.dot((x * x).astype(BF16), ones, preferred_element_type=F32)


def _rms_tiles(x, width):
    return lax.rsqrt(_row_ssq(x) * (1.0 / width) + EPS)


def _sigmoid(x):
    return 0.5 * jnp.tanh(0.5 * x) + 0.5


def _log_sigmoid(x):
    return jnp.minimum(x, 0.0) - jnp.log(1.0 + jnp.exp(-jnp.abs(x)))


def _full(shape):
    return pl.BlockSpec(shape, lambda *_: (0,) * len(shape))


def _inproj_kernel(h_ref, g_ref, wl_ref, wh_ref, wm_ref, wf_ref, zl_ref, zh_ref, zm_ref, zf_ref):
    x = h_ref[...]
    u = (x * _rms(x, D_MODEL) * g_ref[...]).astype(BF16)
    zl_ref[...] = jnp.dot(u, wl_ref[...], preferred_element_type=F32)
    zh_ref[...] = jnp.dot(u, wh_ref[...], preferred_element_type=F32)
    zm_ref[...] = jnp.dot(u, wm_ref[...], preferred_element_type=F32)
    zf_ref[...] = jnp.dot(u, wf_ref[...], preferred_element_type=F32)


def _inproj(h, g, wl, wh, wm, wf, tm):
    R = h.shape[0]
    row = lambda n: pl.BlockSpec((tm, n), lambda i: (i, 0))
    return pl.pallas_call(
        _inproj_kernel,
        grid=(R // tm,),
        in_specs=[row(D_MODEL), _full((1, D_MODEL)), _full(wl.shape), _full(wh.shape),
                  _full(wm.shape), _full(wf.shape)],
        out_specs=[row(wl.shape[1]), row(wh.shape[1]), row(wm.shape[1]), row(wf.shape[1])],
        out_shape=[jax.ShapeDtypeStruct((R, w.shape[1]), F32) for w in (wl, wh, wm, wf)],
        compiler_params=_cparams("parallel"),
        name="inproj",
    )(h, g, wl, wh, wm, wf)


def _rglru_kernel(z_ref, cw_ref, cb_ref, wa_ref, ba_ref, wx_ref, bx_ref, lam_ref, gn_ref, y_ref,
                  xbuf, a_s, b_s, h_s, hst):
    W = GROUP_WIDTH
    tt = y_ref.shape[0]

    @pl.when(pl.program_id(1) == 0)
    def _():
        xbuf[0:8, :] = jnp.zeros((8, W), F32)
        hst[...] = jnp.zeros((1, W), F32)

    xa = z_ref[:, 0:W]
    ga = z_ref[:, W:2 * W]
    xbuf[8:8 + tt, :] = xa
    u = (cb_ref[...] + xbuf[5:5 + tt, :] * cw_ref[0:1, :] + xbuf[6:6 + tt, :] * cw_ref[1:2, :]
         + xbuf[7:7 + tt, :] * cw_ref[2:3, :] + xa * cw_ref[3:4, :])
    xbuf[0:8, :] = xbuf[tt:tt + 8, :]

    ub = u.astype(BF16)
    r = _sigmoid(jnp.dot(ub, wa_ref[...], preferred_element_type=F32) + ba_ref[...])
    i = _sigmoid(jnp.dot(ub, wx_ref[...], preferred_element_type=F32) + bx_ref[...])
    lam = lam_ref[...]
    softplus_neg_lam = jnp.maximum(-lam, 0.0) + jnp.log(1.0 + jnp.exp(-jnp.abs(lam)))
    a = jnp.exp((-LRU_C) * r * softplus_neg_lam)
    b = jnp.sqrt(1.0 - a * a) * (i * u)

    row = lax.broadcasted_iota(jnp.int32, (tt, W), 0) & 7
    for s in (1, 2, 4):
        ok = row >= s
        b = jnp.where(ok, a * pltpu.roll(b, s, 0) + b, b)
        a = jnp.where(ok, a * pltpu.roll(a, s, 0), a)
    a_s[...] = a
    b_s[...] = b

    def group(gi, hprev):
        sl = pl.ds(pl.multiple_of(gi * 8, 8), 8)
        hg = b_s[sl, :] + a_s[sl, :] * hprev
        h_s[sl, :] = hg
        return hg[7:8, :]

    hst[...] = lax.fori_loop(0, tt // 8, group, hst[...], unroll=8)

    c0 = math.sqrt(2.0 / math.pi)
    gelu = 0.5 * ga * (1.0 + jnp.tanh(c0 * (ga + 0.044715 * (ga * ga * ga))))
    y = h_s[...] * gelu
    y_ref[...] = (y * _rms(y, W) * gn_ref[...]).astype(BF16)


def _rglru(zl, cw, cb, wa, ba, wx, bx, lam, gn, B, tt):
    R = zl.shape[0]
    nt = R // B // tt
    W = GROUP_WIDTH
    return pl.pallas_call(
        _rglru_kernel,
        grid=(B, nt),
        in_specs=[pl.BlockSpec((tt, 2 * W), lambda b, t: (b * nt + t, 0)),
                  _full((4, W)), _full((1, W)), _full((W, W)), _full((1, W)), _full((W, W)),
                  _full((1, W)), _full((1, W)), _full((1, W))],
        out_specs=pl.BlockSpec((tt, W), lambda b, t: (b * nt + t, 0)),
        out_shape=jax.ShapeDtypeStruct((R, W), BF16),
        scratch_shapes=[pltpu.VMEM((tt + 8, W), F32), pltpu.VMEM((tt, W), F32),
                        pltpu.VMEM((tt, W), F32), pltpu.VMEM((tt, W), F32), pltpu.VMEM((1, W), F32)],
        compiler_params=_cparams("parallel", "arbitrary"),
        name="rglru",
    )(zl, cw, cb, wa, ba, wx, bx, lam, gn)


def _hgrn2_kernel(z_ref, lb_ref, gn_ref, y_ref, qd_s, ke_s, v_s, dec_s, o_s, st_s):
    W = GROUP_WIDTH
    tt = y_ref.shape[0]
    C = HG_CHUNK

    @pl.when(pl.program_id(1) == 0)
    def _():
        st_s[...] = jnp.zeros((W, W), F32)

    q = z_ref[:, 0:W]
    fz = z_ref[:, W:2 * W]
    v = z_ref[:, 2 * W:3 * W]
    g = z_ref[:, 3 * W:4 * W]
    lb = lb_ref[...]
    la = jnp.log(lb)
    lq = jnp.log(1.0 - lb) + _log_sigmoid(fz)
    logf = jnp.maximum(la, lq) + jnp.log(1.0 + jnp.exp(-jnp.abs(la - lq)))
    kin = 1.0 - jnp.exp(logf)

    rowc = lax.broadcasted_iota(jnp.int32, (tt, W), 0) & (C - 1)
    b = logf
    rev = logf
    s = 1
    while s < C:
        b = b + jnp.where(rowc >= s, pltpu.roll(b, s, 0), 0.0)
        rev = rev + jnp.where(rowc + s < C, pltpu.roll(rev, tt - s, 0), 0.0)
        s *= 2
    tail = rev - logf
    qd = q * jnp.exp(b)
    kd = (kin * jnp.exp(-b)).astype(BF16)
    vb = v.astype(BF16)
    qd_s[...] = qd.astype(BF16)
    ke_s[...] = (kin * jnp.exp(tail)).astype(BF16)
    v_s[...] = vb
    dec_s[...] = jnp.exp(b + tail)

    lane_head = lax.broadcasted_iota(jnp.int32, (1, W), 1) // (W // N_HEADS)

    rr = lax.broadcasted_iota(jnp.int32, (HG_BLOCK, HG_BLOCK), 0)
    cc = lax.broadcasted_iota(jnp.int32, (HG_BLOCK, HG_BLOCK), 1)
    amask = (rr // C == cc // C) & (cc <= rr)
    for jb in range(tt // HG_BLOCK):
        sl = slice(jb * HG_BLOCK, (jb + 1) * HG_BLOCK)
        qb, kb, vv = qd[sl], kd[sl], vb[sl]
        acc = jnp.zeros((HG_BLOCK, W), F32)
        for hh in range(N_HEADS):
            hm = lane_head == hh
            qh = jnp.where(hm, qb, 0.0).astype(BF16)
            att = lax.dot_general(qh, kb, NT_DIMS, preferred_element_type=F32)
            att = jnp.where(amask, att, 0.0).astype(BF16)
            acc = jnp.where(hm, jnp.dot(att, vv, preferred_element_type=F32), acc)
        o_s[sl, :] = acc

    def chunk(c, carry):
        sl = pl.ds(pl.multiple_of(c * C, C), C)
        qc = qd_s[sl, :]
        st = st_s[...]
        qbd = jnp.concatenate([jnp.where(lane_head == hh, qc, jnp.zeros_like(qc))
                               for hh in range(N_HEADS)], axis=0)
        res = lax.dot_general(qbd, st.astype(BF16), NT_DIMS, preferred_element_type=F32)
        oi = jnp.zeros((C, W), F32)
        for hh in range(N_HEADS):
            oi = jnp.where(lane_head == hh, res[hh * C:(hh + 1) * C, :], oi)
        o_s[sl, :] = o_s[sl, :] + oi
        upd = lax.dot_general(v_s[sl, :], ke_s[sl, :], TN_DIMS, preferred_element_type=F32)
        st_s[...] = st * dec_s[pl.ds(c * C, 1), :] + upd
        return carry

    lax.fori_loop(0, tt // C, chunk, 0, unroll=math.gcd(tt // C, 8))

    y = o_s[...] * (g * _sigmoid(g))
    y_ref[...] = (y * _rms(y, W) * gn_ref[...]).astype(BF16)


def _hgrn2(zh, lb, gn, B, tt):
    R = zh.shape[0]
    nt = R // B // tt
    W = GROUP_WIDTH
    return pl.pallas_call(
        _hgrn2_kernel,
        grid=(B, nt),
        in_specs=[pl.BlockSpec((tt, 4 * W), lambda b, t: (b * nt + t, 0)), _full((1, W)), _full((1, W))],
        out_specs=pl.BlockSpec((tt, W), lambda b, t: (b * nt + t, 0)),
        out_shape=jax.ShapeDtypeStruct((R, W), BF16),
        scratch_shapes=[pltpu.VMEM((tt, W), BF16), pltpu.VMEM((tt, W), BF16), pltpu.VMEM((tt, W), BF16),
                        pltpu.VMEM((tt, W), F32), pltpu.VMEM((tt, W), F32), pltpu.VMEM((W, W), F32)],
        compiler_params=_cparams("parallel", "arbitrary"),
        name="hgrn2",
    )(zh, lb, gn)


def _rope(x, cos, s_lo, s_hi):
    return x * cos + pltpu.roll(x, 16, 1) * s_hi + pltpu.roll(x, HEAD_PAD - 16, 1) * s_lo


def _mla_prep_kernel(z_ref, gq_ref, wuq_ref, gkv_ref, wuk_ref, wuv_ref, gqn_ref, gkn_ref,
                     cos_ref, slo_ref, shi_ref, q_ref, k_ref, v_ref):
    P = HEAD_PAD
    ckv = z_ref[:, 0:P]
    krb = z_ref[:, P:2 * P]
    cq = z_ref[:, 2 * P:4 * P]
    rq = _rms_tiles(cq, MLA_Q_RANK)
    qn = (cq * jnp.concatenate([rq, rq], axis=1) * gq_ref[...]).astype(BF16)
    kvn = (ckv * _rms_tiles(ckv, MLA_KV_RANK) * gkv_ref[...]).astype(BF16)
    q = jnp.dot(qn, wuq_ref[...], preferred_element_type=F32)
    kn = jnp.dot(kvn, wuk_ref[...], preferred_element_type=F32)
    vv = jnp.dot(kvn, wuv_ref[...], preferred_element_type=F32)
    cos, slo, shi = cos_ref[...], slo_ref[...], shi_ref[...]
    lane = lax.broadcasted_iota(jnp.int32, (1, P), 1)
    qscale = (MLA_QK ** -0.5) * LOG2E
    for hh in range(N_HEADS):
        sl = slice(hh * P, (hh + 1) * P)
        qh = q[:, sl]
        qh = qh * _rms_tiles(qh, MLA_QK) * gqn_ref[...]
        q_ref[:, sl] = (_rope(qh, cos, slo, shi) * qscale).astype(BF16)
        kh = kn[:, sl] + krb
        kh = kh * _rms_tiles(kh, MLA_QK) * gkn_ref[...]
        k_ref[:, sl] = _rope(kh, cos, slo, shi).astype(BF16)
        v_ref[:, sl] = jnp.where(lane == ONES_LANE, 1.0, vv[:, sl]).astype(BF16)


def _mla_prep(zm, gq, wuq, gkv, wuk, wuv, gqn, gkn, cos, slo, shi, tm):
    R = zm.shape[0]
    P = HEAD_PAD
    nt = cos.shape[0] // tm
    row = lambda n: pl.BlockSpec((tm, n), lambda i: (i, 0))
    tab = pl.BlockSpec((tm, P), lambda i: (i % nt, 0))
    out = jax.ShapeDtypeStruct((R, N_HEADS * P), BF16)
    return pl.pallas_call(
        _mla_prep_kernel,
        grid=(R // tm,),
        in_specs=[row(4 * P), _full((1, 2 * P)), _full(wuq.shape), _full((1, P)), _full(wuk.shape),
                  _full(wuv.shape), _full((1, P)), _full((1, P)), tab, tab, tab],
        out_specs=[row(N_HEADS * P)] * 3,
        out_shape=[out, out, out],
        compiler_params=_cparams("parallel"),
        name="mla_prep",
    )(zm, gq, wuq, gkv, wuk, wuv, gqn, gkn, cos, slo, shi)


STAT_ROWS = 8
STAT_C_FIRST, STAT_C_LAST, STAT_Q2, STAT_K2 = range(4)


def _fox_prep_kernel(z_ref, bf_ref, gqn_ref, gkn_ref, hsum_ref, place_ref, q_ref, k_ref, v_ref, stat_ref, carry):
    P = HEAD_PAD
    W = GROUP_WIDTH
    tt = q_ref.shape[0]

    @pl.when(pl.program_id(1) == 0)
    def _():
        carry[...] = jnp.zeros((1, P), F32)

    c = _log_sigmoid(z_ref[:, 3 * W:] + bf_ref[...])
    row = lax.broadcasted_iota(jnp.int32, (tt, P), 0)
    s = 1
    while s < tt:
        c = c + jnp.where(row >= s, pltpu.roll(c, s, 0), 0.0)
        s *= 2
    c = c + carry[...]
    carry[...] = c[tt - 1:tt, :]
    c = c * LOG2E
    c1 = c.astype(BF16).astype(F32)
    c2 = (c - c1).astype(BF16).astype(F32)
    c3 = c - c1 - c2

    def norm_place(x, gain):
        ssq = jnp.dot((x * x).astype(BF16), hsum_ref[...], preferred_element_type=F32)
        xn = (x * lax.rsqrt(ssq * (1.0 / FOX_HD) + EPS) * gain).astype(BF16)
        return jnp.dot(xn, place_ref[...], preferred_element_type=F32)

    qscale = (FOX_HD ** -0.5) * LOG2E
    qp = norm_place(z_ref[:, 0:W], gqn_ref[...] * qscale)
    kp = norm_place(z_ref[:, W:2 * W], gkn_ref[...])
    vp = jnp.dot(z_ref[:, 2 * W:3 * W].astype(BF16), place_ref[...], preferred_element_type=F32)

    lane = lax.broadcasted_iota(jnp.int32, (1, P), 1)
    q2 = jnp.zeros((1, P), F32)
    k2 = jnp.zeros((1, P), F32)
    for hh in range(N_HEADS):
        hs = slice(hh * P, (hh + 1) * P)
        p1 = c1[:, hh:hh + 1]
        p2 = c2[:, hh:hh + 1]
        p3 = c3[:, hh:hh + 1]
        qh = qp[:, hs]
        q2 = jnp.where(lane == hh, jnp.max(_row_ssq(qh), axis=0, keepdims=True), q2)
        qh = jnp.where(lane == 64, p1, jnp.where(lane == 65, p2, jnp.where(lane == 66, p3, qh)))
        qh = jnp.where((lane >= 67) & (lane < 70), 1.0, qh)
        q_ref[:, hs] = qh.astype(BF16)
        kh = kp[:, hs]
        k2 = jnp.where(lane == hh, jnp.max(_row_ssq(kh), axis=0, keepdims=True), k2)
        kh = jnp.where(lane == 67, -p1, jnp.where(lane == 68, -p2, jnp.where(lane == 69, -p3, kh)))
        kh = jnp.where((lane >= 64) & (lane < 67), 1.0, kh)
        k_ref[:, hs] = kh.astype(BF16)
        v_ref[:, hs] = jnp.where(lane == ONES_LANE, 1.0, vp[:, hs]).astype(BF16)

    srow = lax.broadcasted_iota(jnp.int32, (STAT_ROWS, P), 0)
    stat = jnp.zeros((STAT_ROWS, P), F32)
    for rr, val in ((STAT_C_FIRST, c[0:1, :]), (STAT_C_LAST, c[tt - 1:tt, :]), (STAT_Q2, q2), (STAT_K2, k2)):
        stat = jnp.where(srow == rr, val, stat)
    stat_ref[...] = stat


def _fox_prep(zf, bf, gqn, gkn, B, tt):
    R = zf.shape[0]
    nt = R // B // tt
    P = HEAD_PAD
    W = GROUP_WIDTH
    blk = lambda n: pl.BlockSpec((tt, n), lambda b, t: (b * nt + t, 0))
    out = jax.ShapeDtypeStruct((R, N_HEADS * P), BF16)
    head_sum = jnp.kron(jnp.eye(N_HEADS, dtype=F32), jnp.ones((FOX_HD, FOX_HD), F32)).astype(BF16)
    place = _pad_heads(jnp.eye(W, dtype=F32), FOX_HD).astype(BF16)
    return pl.pallas_call(
        _fox_prep_kernel,
        grid=(B, nt),
        in_specs=[blk(zf.shape[1]), _full((1, P)), _full((1, W)), _full((1, W)), _full((W, W)),
                  _full((W, N_HEADS * P))],
        out_specs=[blk(N_HEADS * P)] * 3 + [pl.BlockSpec((STAT_ROWS, P), lambda b, t: (b * nt + t, 0))],
        out_shape=[out, out, out, jax.ShapeDtypeStruct((B * nt * STAT_ROWS, P), F32)],
        scratch_shapes=[pltpu.VMEM((1, P), F32)],
        compiler_params=_cparams("parallel", "arbitrary"),
        name="fox_prep",
    )(zf, bf, gqn, gkn, head_sum, place)


SKIP_LOG2_MARGIN = 40.0
NORM_SLACK = 1.02


def _fox_first_chunk(stats, B, nt):
    st = stats.reshape(B, nt, STAT_ROWS, HEAD_PAD)[..., :N_HEADS]
    c_first, c_last = st[:, :, STAT_C_FIRST], st[:, :, STAT_C_LAST]
    bound = jnp.sqrt(jnp.max(st[:, :, STAT_Q2], axis=1) * jnp.max(st[:, :, STAT_K2], axis=1)) * NORM_SLACK
    gap = 2.0 * bound[:, None, None, :] + c_first[:, :, None, :] - c_last[:, None, :, :]
    earlier = jnp.arange(nt)[None, :] < jnp.arange(nt)[:, None]
    skip = (gap < -SKIP_LOG2_MARGIN) & earlier[None, :, :, None]
    return jnp.min(jnp.sum(skip, axis=2), axis=-1).astype(jnp.int32).reshape(-1)


def _attn_kernel(first_ref, q_ref, k_ref, v_ref, gn_ref, y_ref, m_s, acc_s, o_s, *, chunk_causal, tk):
    P = HEAD_PAD
    tq = q_ref.shape[0]
    lp = k_ref.shape[0]
    blk = pl.program_id(0) * pl.num_programs(1) + pl.program_id(1)
    q0 = pl.program_id(1) * tq

    qpos = q0 + lax.broadcasted_iota(jnp.int32, (tq, 1), 0)
    if chunk_causal:
        qlim = N_META + CHUNK * ((qpos + (CHUNK - N_META)) // CHUNK)
        reach = N_META
    else:
        qlim = qpos + 1
        reach = 0
    n_full = (q0 + reach) // tk
    n_diag = (jnp.minimum(q0 + tq, lp) + tk - 1) // tk

    def visit(k0, width, masked):
        ks = pl.ds(pl.multiple_of(k0, SEQ_ALIGN), width)
        if masked:
            vis = (k0 + lax.broadcasted_iota(jnp.int32, (1, width), 1)) < qlim
        for hh in range(N_HEADS):
            hs = slice(hh * P, (hh + 1) * P)
            s = lax.dot_general(q_ref[:, hs], k_ref[ks, hs], NT_DIMS, preferred_element_type=F32)
            if masked:
                s = jnp.where(vis, s, -jnp.inf)
            tiles = [s[:, c * P:(c + 1) * P] for c in range(width // P)]
            mx = tiles[0]
            for t in tiles[1:]:
                mx = jnp.maximum(mx, t)
            m_old = m_s[hh]
            m_new = jnp.maximum(m_old, jnp.max(mx, axis=-1, keepdims=True))
            p = jnp.concatenate([jnp.exp2((t - m_new).astype(BF16)) for t in tiles], axis=1)
            acc_s[hh] = jnp.exp2(m_old - m_new) * acc_s[hh] + jnp.dot(
                p, v_ref[ks, hs], preferred_element_type=F32)
            m_s[hh] = m_new

    m_s[...] = jnp.full(m_s.shape, -jnp.inf, F32)
    acc_s[...] = jnp.zeros(acc_s.shape, F32)

    def full_body(j, carry):
        visit(j * tk, tk, False)
        return carry

    def masked_body(j, carry):
        visit(j * tk, tk, True)
        return carry

    lax.fori_loop(first_ref[blk], n_full, full_body, 0)
    lax.fori_loop(n_full, n_diag, masked_body, 0)
    if chunk_causal:
        @pl.when(q0 + tq < lp)
        def _():
            visit(q0 + tq, SEQ_ALIGN, True)

    hd = GROUP_WIDTH // N_HEADS
    for hh in range(N_HEADS):
        acc = acc_s[hh]
        o_s[:, hh * hd:(hh + 1) * hd] = acc[:, 0:hd] / acc[:, ONES_LANE:ONES_LANE + 1]
    y = o_s[...]
    y_ref[...] = (y * _rms(y, GROUP_WIDTH) * gn_ref[...]).astype(BF16)


def _attention(first_chunk, q, k, v, gn, B, tq, tk, chunk_causal):
    R = q.shape[0]
    lp = R // B
    nq = lp // tq
    P = HEAD_PAD
    W = GROUP_WIDTH
    kv_spec = pl.BlockSpec((lp, N_HEADS * P), lambda b, i, fc: (b, 0))
    return pl.pallas_call(
        functools.partial(_attn_kernel, chunk_causal=chunk_causal, tk=tk),
        grid_spec=pltpu.PrefetchScalarGridSpec(
            num_scalar_prefetch=1,
            grid=(B, nq),
            in_specs=[pl.BlockSpec((tq, N_HEADS * P), lambda b, i, fc: (b * nq + i, 0)), kv_spec, kv_spec,
                      pl.BlockSpec((1, W), lambda b, i, fc: (0, 0))],
            out_specs=pl.BlockSpec((tq, W), lambda b, i, fc: (b * nq + i, 0)),
            scratch_shapes=[pltpu.VMEM((N_HEADS, tq, P), F32), pltpu.VMEM((N_HEADS, tq, P), F32),
                            pltpu.VMEM((tq, W), F32)],
        ),
        out_shape=jax.ShapeDtypeStruct((R, W), BF16),
        compiler_params=_cparams("parallel", "arbitrary"),
        name="mla_attn" if chunk_causal else "fox_attn",
    )(first_chunk, q, k, v, gn)


ROUTE_E1, ROUTE_E2, ROUTE_R1, ROUTE_R2, ROUTE_G1, ROUTE_G2 = range(6)


def _outproj_kernel(h_ref, ya_ref, yb_ref, yc_ref, yd_ref, w_ref, g_ref, *rest, with_router):
    W = GROUP_WIDTH
    if with_router:
        wr_ref, hn_ref, u_ref, route_ref, cnt_ref = rest
    else:
        hn_ref, u_ref = rest
    acc = h_ref[...]
    for gi, y_ref in enumerate((ya_ref, yb_ref, yc_ref, yd_ref)):
        acc = acc + jnp.dot(y_ref[...], w_ref[gi * W:(gi + 1) * W, :], preferred_element_type=F32)
    hn_ref[...] = acc
    u = acc * _rms(acc, D_MODEL) * g_ref[...]
    if not with_router:
        u_ref[...] = u.astype(BF16)
        return
    u_ref[...] = u
    tm = u.shape[0]

    @pl.when(pl.program_id(0) == 0)
    def _():
        cnt_ref[...] = jnp.zeros(cnt_ref.shape, F32)

    u_hi = u.astype(BF16)
    u_lo = (u - u_hi.astype(F32)).astype(BF16)
    logits = (jnp.dot(u_hi, wr_ref[0], preferred_element_type=F32)
              + jnp.dot(u_lo, wr_ref[0], preferred_element_type=F32)
              + jnp.dot(u_hi, wr_ref[1], preferred_element_type=F32))
    lane = lax.broadcasted_iota(jnp.int32, logits.shape, 1).astype(F32)
    lg = jnp.where(lane < N_EXPERTS, logits, -jnp.inf)
    m1 = jnp.max(lg, axis=-1, keepdims=True)
    i1 = jnp.min(jnp.where(lg == m1, lane, 1e9), axis=-1, keepdims=True)
    lg2 = jnp.where(lane == i1, -jnp.inf, lg)
    m2 = jnp.max(lg2, axis=-1, keepdims=True)
    i2 = jnp.min(jnp.where(lg2 == m2, lane, 1e9), axis=-1, keepdims=True)
    e = jnp.exp(m2 - m1)
    g1 = 1.0 / (1.0 + e)
    picks = jnp.where(lane == i1, 1.0, 0.0) + jnp.where(lane == i2, 1.0, 0.0)
    earlier = (lax.broadcasted_iota(jnp.int32, (tm, tm), 0) > lax.broadcasted_iota(jnp.int32, (tm, tm), 1))
    base = cnt_ref[...] + jnp.dot(earlier.astype(BF16), picks.astype(BF16), preferred_element_type=F32)
    r1 = jnp.sum(jnp.where(lane == i1, base, 0.0), axis=-1, keepdims=True)
    r2 = jnp.sum(jnp.where(lane == i2, base, 0.0), axis=-1, keepdims=True)
    cnt_ref[...] = cnt_ref[...] + jnp.sum(picks, axis=0, keepdims=True)
    rec = jnp.zeros(logits.shape, F32)
    for ln, val in ((ROUTE_E1, i1), (ROUTE_E2, i2), (ROUTE_R1, r1), (ROUTE_R2, r2),
                    (ROUTE_G1, g1), (ROUTE_G2, e * g1)):
        rec = jnp.where(lane == ln, val, rec)
    route_ref[...] = rec


def _outproj(h, ya, yb, yc, yd, w, g, wr, tm):
    R = h.shape[0]
    W = GROUP_WIDTH
    with_router = wr is not None
    row = lambda n: pl.BlockSpec((tm, n), lambda i: (i, 0))
    in_specs = [row(D_MODEL), row(W), row(W), row(W), row(W), _full(w.shape), _full((1, D_MODEL))]
    out_specs = [row(D_MODEL), row(D_MODEL)]
    out_shape = [jax.ShapeDtypeStruct((R, D_MODEL), F32),
                 jax.ShapeDtypeStruct((R, D_MODEL), F32 if with_router else BF16)]
    args = [h, ya, yb, yc, yd, w, g]
    if with_router:
        in_specs.append(_full(wr.shape))
        out_specs += [row(wr.shape[-1]), _full((1, wr.shape[-1]))]
        out_shape += [jax.ShapeDtypeStruct((R, wr.shape[-1]), F32), jax.ShapeDtypeStruct((1, wr.shape[-1]), F32)]
        args.append(wr)
    return pl.pallas_call(
        functools.partial(_outproj_kernel, with_router=with_router),
        grid=(R // tm,),
        in_specs=in_specs, out_specs=out_specs, out_shape=out_shape,
        compiler_params=_cparams("arbitrary" if with_router else "parallel"),
        name="outproj_router" if with_router else "outproj",
    )(*args)


def _ffn_kernel(u_ref, h_ref, wg_ref, wu_ref, wd_ref, o_ref):
    @pl.when(pl.program_id(1) == 0)
    def _():
        o_ref[...] = h_ref[...]

    u = u_ref[...]
    a = jnp.dot(u, wg_ref[...], preferred_element_type=F32)
    b = jnp.dot(u, wu_ref[...], preferred_element_type=F32)
    hid = (a * _sigmoid(a) * b).astype(BF16)
    o_ref[...] += jnp.dot(hid, wd_ref[...], preferred_element_type=F32)


def _ffn(u, h, wg, wu, wd, tm, tf):
    R = h.shape[0]
    dff = wg.shape[1]
    return pl.pallas_call(
        _ffn_kernel,
        grid=(R // tm, dff // tf),
        in_specs=[pl.BlockSpec((tm, D_MODEL), lambda i, f: (i, 0)),
                  pl.BlockSpec((tm, D_MODEL), lambda i, f: (i, 0)),
                  pl.BlockSpec((D_MODEL, tf), lambda i, f: (0, f)),
                  pl.BlockSpec((D_MODEL, tf), lambda i, f: (0, f)),
                  pl.BlockSpec((tf, D_MODEL), lambda i, f: (f, 0))],
        out_specs=pl.BlockSpec((tm, D_MODEL), lambda i, f: (i, 0)),
        out_shape=jax.ShapeDtypeStruct((R, D_MODEL), F32),
        compiler_params=_cparams("parallel", "arbitrary"),
        name="ffn",
    )(u, h, wg, wu, wd)


def _row_copy(src_ref, src_row, dst_ref, dst_row, sem):
    return pltpu.make_async_copy(src_ref.at[pl.ds(src_row, 1)], dst_ref.at[pl.ds(dst_row, 1)], sem)


def _dispatch_kernel(pos1_ref, pos2_ref, u_ref, xs_in_ref, xs_ref, sem):
    del xs_in_ref
    tm = u_ref.shape[0]
    t0 = pl.program_id(0) * tm

    def issue(r, carry):
        _row_copy(u_ref, r, xs_ref, pos1_ref[t0 + r], sem).start()
        _row_copy(u_ref, r, xs_ref, pos2_ref[t0 + r], sem).start()
        return carry

    lax.fori_loop(0, tm, issue, 0, unroll=8)

    def drain(r, carry):
        _row_copy(u_ref, r, xs_ref, 0, sem).wait()
        _row_copy(u_ref, r, xs_ref, 0, sem).wait()
        return carry

    lax.fori_loop(0, tm, drain, 0, unroll=8)


def _dispatch(pos1, pos2, u, n_slots, tm):
    R = u.shape[0]
    xs0 = jnp.zeros((n_slots, D_MODEL), F32)
    return pl.pallas_call(
        _dispatch_kernel,
        grid_spec=pltpu.PrefetchScalarGridSpec(
            num_scalar_prefetch=2,
            grid=(R // tm,),
            in_specs=[pl.BlockSpec((tm, D_MODEL), lambda i, p1, p2: (i, 0)),
                      pl.BlockSpec(memory_space=pl.ANY)],
            out_specs=pl.BlockSpec(memory_space=pl.ANY),
            scratch_shapes=[pltpu.SemaphoreType.DMA],
        ),
        out_shape=jax.ShapeDtypeStruct((n_slots, D_MODEL), F32),
        input_output_aliases={3: 0},
        compiler_params=_cparams("arbitrary"),
        name="moe_dispatch",
    )(pos1, pos2, u, xs0)


def _expert_ffn_kernel(te_ref, nt_ref, x_ref, wg_ref, wu_ref, wd_ref, y_ref, xb_s):
    f = pl.program_id(1)
    used = pl.program_id(0) < nt_ref[0]

    @pl.when(jnp.logical_not(used) & (f == 0))
    def _():
        y_ref[...] = jnp.zeros(y_ref.shape, F32)

    @pl.when(used)
    def _():
        @pl.when(f == 0)
        def _():
            xb_s[...] = x_ref[...].astype(BF16)

        x = xb_s[...]
        a = jnp.dot(x, wg_ref[...], preferred_element_type=F32)
        b = jnp.dot(x, wu_ref[...], preferred_element_type=F32)
        hid = (a * _sigmoid(a) * b).astype(BF16)
        out = jnp.dot(hid, wd_ref[...], preferred_element_type=F32)

        @pl.when(f == 0)
        def _():
            y_ref[...] = out

        @pl.when(f > 0)
        def _():
            y_ref[...] += out


def _expert_ffn(tile_expert, n_tiles, xs, wg, wu, wd, tm, tf):
    n_slots = xs.shape[0]
    dff = wg.shape[2]
    row_map = lambda i, f, te, nt: (jnp.minimum(i, nt[0] - 1), 0)
    out_map = lambda i, f, te, nt: (i, 0)
    return pl.pallas_call(
        _expert_ffn_kernel,
        grid_spec=pltpu.PrefetchScalarGridSpec(
            num_scalar_prefetch=2,
            grid=(n_slots // tm, dff // tf),
            in_specs=[pl.BlockSpec((tm, D_MODEL), row_map),
                      pl.BlockSpec((None, D_MODEL, tf), lambda i, f, te, nt: (te[i], 0, f)),
                      pl.BlockSpec((None, D_MODEL, tf), lambda i, f, te, nt: (te[i], 0, f)),
                      pl.BlockSpec((None, tf, D_MODEL), lambda i, f, te, nt: (te[i], f, 0))],
            out_specs=pl.BlockSpec((tm, D_MODEL), out_map),
            scratch_shapes=[pltpu.VMEM((tm, D_MODEL), BF16)],
        ),
        out_shape=jax.ShapeDtypeStruct((n_slots, D_MODEL), F32),
        compiler_params=_cparams("arbitrary", "arbitrary"),
        name="moe_expert_ffn",
    )(tile_expert, n_tiles, xs, wg, wu, wd)


def _combine_kernel(pos1_ref, pos2_ref, h_ref, route_ref, ys_ref, o_ref, y1_s, y2_s, sem, *, row0):
    tm = h_ref.shape[0]
    t0 = row0(pl.program_id(0), pl.program_id(1))

    def issue(r, carry):
        _row_copy(ys_ref, pos1_ref[t0 + r], y1_s, r, sem).start()
        _row_copy(ys_ref, pos2_ref[t0 + r], y2_s, r, sem).start()
        return carry

    lax.fori_loop(0, tm, issue, 0, unroll=8)

    def drain(r, carry):
        _row_copy(ys_ref, 0, y1_s, r, sem).wait()
        _row_copy(ys_ref, 0, y2_s, r, sem).wait()
        return carry

    lax.fori_loop(0, tm, drain, 0, unroll=8)
    rec = route_ref[...]
    g1 = rec[:, ROUTE_G1:ROUTE_G1 + 1]
    g2 = rec[:, ROUTE_G2:ROUTE_G2 + 1]
    o_ref[...] = h_ref[...] + g1 * y1_s[...] + g2 * y2_s[...]


def _combine(pos1, pos2, h, route, ys, tm, frames=None):
    R = h.shape[0]
    if frames is None:
        grid = (R // tm, 1)
        row0 = lambda i, j: i * tm
        in_rows = lambda n: pl.BlockSpec((tm, n), lambda i, j, p1, p2: (i, 0))
        out_rows = R
        out_spec = pl.BlockSpec((tm, D_MODEL), lambda i, j, p1, p2: (i, 0))
    else:
        B, S, lp = frames
        grid = (B, S // tm)
        row0 = lambda b, w: b * lp + N_META + w * tm
        in_rows = lambda n: pl.BlockSpec((pl.Element(tm), pl.Element(n)),
                                         lambda b, w, p1, p2: (pl.multiple_of(row0(b, w), 8), 0))
        out_rows = B * S
        out_spec = pl.BlockSpec((tm, D_MODEL), lambda b, w, p1, p2: (b * (S // tm) + w, 0))
    return pl.pallas_call(
        functools.partial(_combine_kernel, row0=row0),
        grid_spec=pltpu.PrefetchScalarGridSpec(
            num_scalar_prefetch=2,
            grid=grid,
            in_specs=[in_rows(D_MODEL), in_rows(route.shape[1]), pl.BlockSpec(memory_space=pl.ANY)],
            out_specs=out_spec,
            scratch_shapes=[pltpu.VMEM((tm, D_MODEL), F32), pltpu.VMEM((tm, D_MODEL), F32),
                            pltpu.SemaphoreType.DMA],
        ),
        out_shape=jax.ShapeDtypeStruct((out_rows, D_MODEL), F32),
        compiler_params=_cparams("arbitrary", "arbitrary"),
        name="moe_combine",
    )(pos1, pos2, h, route, ys)


def _moe(u, h, route, counts, wg, wu, wd, tm_rows, tm_expert, tf, frames=None):
    R = h.shape[0]
    ne = wg.shape[0]
    cnt = counts[0, :ne].astype(jnp.int32)
    padded = -(-cnt // tm_expert) * tm_expert
    ends = jnp.cumsum(padded)
    offs = ends - padded
    col = lambda ln: route[:, ln].astype(jnp.int32)
    pos1 = jnp.take(offs, col(ROUTE_E1)) + col(ROUTE_R1)
    pos2 = jnp.take(offs, col(ROUTE_E2)) + col(ROUTE_R2)
    n_slots = (2 * R // tm_expert + ne) * tm_expert
    n_tiles = (ends[-1] // tm_expert).reshape(1)
    tile_start = jnp.arange(n_slots // tm_expert, dtype=jnp.int32) * tm_expert
    tile_expert = jnp.sum(tile_start[:, None] >= ends[None, :], axis=1).astype(jnp.int32)
    last_expert = jnp.sum((ends[-1] - 1) >= ends).astype(jnp.int32)
    tile_expert = jnp.minimum(tile_expert, last_expert)

    xs = _dispatch(pos1, pos2, u, n_slots, tm_rows)
    ys = _expert_ffn(tile_expert, n_tiles, xs, wg, wu, wd, tm_expert, tf)
    if frames is None:
        return _combine(pos1, pos2, h, route, ys, tm_rows)
    return _combine(pos1, pos2, h, route, ys, _tile(frames[1], tm_rows), frames)


def _pad_heads(w, real):
    rows = w.shape[0]
    w = w.reshape(rows, N_HEADS, real)
    return jnp.pad(w, ((0, 0), (0, 0), (0, HEAD_PAD - real))).reshape(rows, N_HEADS * HEAD_PAD)


def _pad_cols(w, n):
    return jnp.pad(w, ((0, 0), (0, n - w.shape[1])))


def _row(v, n=None):
    v = v.reshape(1, -1).astype(F32)
    return v if n is None else _pad_cols(v, n)


def _block_diag(w):
    nb, c, d = w.shape
    eye = jnp.eye(nb, dtype=w.dtype)
    return (eye[:, None, :, None] * w[:, :, None, :]).reshape(nb * c, nb * d)


def _rope_tables(lp):
    half = MLA_ROPE // 2
    inv = ROPE_THETA ** (-jnp.arange(half, dtype=F32) / half)
    ang = jnp.arange(lp, dtype=jnp.int32).astype(F32)[:, None] * inv[None, :]
    cos, sin = jnp.cos(ang), jnp.sin(ang)
    one = jnp.ones((lp, MLA_NOPE), F32)
    zero = jnp.zeros((lp, MLA_NOPE), F32)
    zh = jnp.zeros((lp, half), F32)
    tail1 = jnp.ones((lp, HEAD_PAD - MLA_QK), F32)
    tail0 = jnp.zeros((lp, HEAD_PAD - MLA_QK), F32)
    cos_t = jnp.concatenate([one, cos, cos, tail1], axis=1)
    s_lo = jnp.concatenate([zero, -sin, zh, tail0], axis=1)
    s_hi = jnp.concatenate([zero, zh, sin, tail0], axis=1)
    return cos_t, s_lo, s_hi


def kernel(x, meta, norm1_g, norm2_g, w_in, w_out, out_norm_g, lru_conv_w, lru_conv_b, lru_wa, lru_ba, lru_wx, lru_bx, lru_lambda, hg_lb_logits, mla_gq, mla_w_uq, mla_gkv, mla_w_ukv, mla_gqn, mla_gkn, fox_gqn, fox_gkn, fox_bf, ffn_w_gate, ffn_w_up, ffn_w_down, moe_w_router, moe_w_gate, moe_w_up, moe_w_down):
    B, S, _ = x.shape
    depth = w_in.shape[0]
    L = N_META + S
    lp = -(-L // SEQ_ALIGN) * SEQ_ALIGN
    R = B * lp
    W = GROUP_WIDTH
    P = HEAD_PAD

    tt = _tile(lp, 640)
    tq = tk = tt

    h = jnp.concatenate([jnp.broadcast_to(meta[None].astype(x.dtype), (B, N_META, D_MODEL)), x,
                         jnp.zeros((B, lp - L, D_MODEL), x.dtype)], axis=1).reshape(R, D_MODEL)
    cos_t, s_lo, s_hi = _rope_tables(lp)
    lb_cum = jnp.cumsum(jax.nn.softmax(hg_lb_logits.astype(F32), axis=0), axis=0)

    o = 0
    offs = []
    for n in (W, W, W, W, W, W, MLA_Q_RANK, MLA_KV_RANK, MLA_ROPE, W, W, W, N_HEADS):
        offs.append(o)
        o += n
    (o_xa, _, o_hq, _, _, _, o_cq, o_ckv, o_kr, o_fq, o_fk, o_fv, o_ff) = offs

    for l in range(depth):
        w = w_in[l]
        wl = w[:, o_xa:o_xa + 2 * W].astype(BF16)
        wh = w[:, o_hq:o_hq + 4 * W].astype(BF16)
        wm = jnp.concatenate([
            w[:, o_ckv:o_ckv + MLA_KV_RANK],
            jnp.zeros((D_MODEL, MLA_NOPE), F32), w[:, o_kr:o_kr + MLA_ROPE],
            jnp.zeros((D_MODEL, P - MLA_QK), F32),
            _pad_cols(w[:, o_cq:o_cq + MLA_Q_RANK], 2 * P)], axis=1).astype(BF16)
        wf = jnp.concatenate([w[:, o_fq:o_fq + 3 * W], _pad_cols(w[:, o_ff:o_ff + N_HEADS], P)],
                             axis=1).astype(BF16)
        zl, zh, zm, zf = _inproj(h, _row(norm1_g[l]), wl, wh, wm, wf, tt)

        gn = out_norm_g[l].astype(F32)
        ya = _rglru(zl, lru_conv_w[l].astype(F32), _row(lru_conv_b[l]),
                    _block_diag(lru_wa[l]).astype(BF16), _row(lru_ba[l]),
                    _block_diag(lru_wx[l]).astype(BF16), _row(lru_bx[l]),
                    _row(lru_lambda[l]), _row(gn[0:W]), B, tt)
        yb = _hgrn2(zh, _row(lb_cum[l] - lb_cum[0]), _row(gn[W:2 * W]), B, tt)

        wuq = jnp.pad(_pad_heads(mla_w_uq[l], MLA_QK), ((0, 2 * P - MLA_Q_RANK), (0, 0))).astype(BF16)
        wukv = mla_w_ukv[l].reshape(MLA_KV_RANK, N_HEADS, MLA_NOPE + MLA_V)
        wuk = _pad_heads(wukv[:, :, :MLA_NOPE].reshape(MLA_KV_RANK, -1), MLA_NOPE).astype(BF16)
        wuv = _pad_heads(wukv[:, :, MLA_NOPE:].reshape(MLA_KV_RANK, -1), MLA_V).astype(BF16)
        q, k, v = _mla_prep(zm, _row(mla_gq[l], 2 * P), wuq, _row(mla_gkv[l]), wuk, wuv,
                            _row(mla_gqn[l], P), _row(mla_gkn[l], P), cos_t, s_lo, s_hi, tt)
        visit_all = jnp.zeros((B * (lp // tq),), jnp.int32)
        yc = _attention(visit_all, q, k, v, _row(gn[2 * W:3 * W]), B, tq, tk, True)

        q, k, v, stats = _fox_prep(zf, _row(fox_bf[l], P), _row(jnp.tile(fox_gqn[l], N_HEADS)),
                                   _row(jnp.tile(fox_gkn[l], N_HEADS)), B, tt)
        yd = _attention(_fox_first_chunk(stats, B, lp // tt), q, k, v, _row(gn[3 * W:4 * W]), B, tq, tk, False)

        wo = w_out[l].astype(BF16)
        if l % 2 == 0:
            hn, u2 = _outproj(h, ya, yb, yc, yd, wo, _row(norm2_g[l]), None, tt)
            j = l // 2
            h = _ffn(u2, hn, ffn_w_gate[j].astype(BF16), ffn_w_up[j].astype(BF16),
                     ffn_w_down[j].astype(BF16), tt, 1408)
        else:
            j = l // 2
            wr = _pad_cols(moe_w_router[j].astype(F32), P)
            wr_hi = wr.astype(BF16)
            wr = jnp.stack([wr_hi, (wr - wr_hi.astype(F32)).astype(BF16)])
            hn, u2, route, counts = _outproj(h, ya, yb, yc, yd, wo, _row(norm2_g[l]), wr, tt)
            last = l == depth - 1
            h = _moe(u2, hn, route, counts, moe_w_gate[j].astype(BF16), moe_w_up[j].astype(BF16),
                     moe_w_down[j].astype(BF16), tt, 512, 1792, (B, S, lp) if last else None)
            if last:
                return h.reshape(B, S, D_MODEL)
    return h.reshape(B, lp, D_MODEL)[:, N_META:L]
```

```python
import functools
import math

import jax
import jax.numpy as jnp
from jax import lax
from jax.experimental import pallas as pl
from jax.experimental.pallas import tpu as pltpu

F32 = jnp.float32
BF16 = jnp.bfloat16

D_MODEL = 1024
N_META = 16
CHUNK = 64
SEQ_ALIGN = 128
EPS = 1e-6
GROUP_WIDTH = 256
N_HEADS = 4
HEAD_PAD = 128
LRU_C = 8.0
HG_CHUNK = 16
HG_BLOCK = 128
MLA_NOPE, MLA_ROPE, MLA_V = 64, 32, 64
MLA_QK = MLA_NOPE + MLA_ROPE
MLA_Q_RANK, MLA_KV_RANK = 192, 128
ROPE_THETA = 10000.0
FOX_HD = 64
N_EXPERTS = 8
LOG2E = 1.4426950408889634
ONES_LANE = 64
VMEM_LIMIT = 56 * 1024 * 1024

NT_DIMS = (((1,), (1,)), ((), ()))
TN_DIMS = (((0,), (0,)), ((), ()))


def _cparams(*sem):
    return pltpu.CompilerParams(dimension_semantics=sem, vmem_limit_bytes=VMEM_LIMIT)


def _tile(n, pref, align=SEQ_ALIGN):
    best = None
    for t in range(align, min(n, pref) + 1, align):
        if n % t == 0:
            best = t
    assert best is not None, (n, pref, align)
    return best


def _rms(x, width):
    return lax.rsqrt(jnp.sum(x * x, axis=-1, keepdims=True) * (1.0 / width) + EPS)


def _row_ssq(x):
    ones = jnp.ones((x.shape[1], HEAD_PAD), BF16)
    return jnp.dot((x * x).astype(BF16), ones, preferred_element_type=F32)


def _rms_tiles(x, width):
    return lax.rsqrt(_row_ssq(x) * (1.0 / width) + EPS)


def _sigmoid(x):
    return 0.5 * jnp.tanh(0.5 * x) + 0.5


def _log_sigmoid(x):
    return jnp.minimum(x, 0.0) - jnp.log(1.0 + jnp.exp(-jnp.abs(x)))


def _full(shape):
    return pl.BlockSpec(shape, lambda *_: (0,) * len(shape))


def _inproj_kernel(h_ref, g_ref, wl_ref, wh_ref, wm_ref, wf_ref, zl_ref, zh_ref, zm_ref, zf_ref):
    x = h_ref[...]
    u = (x * _rms(x, D_MODEL) * g_ref[...]).astype(BF16)
    zl_ref[...] = jnp.dot(u, wl_ref[...], preferred_element_type=F32)
    zh_ref[...] = jnp.dot(u, wh_ref[...], preferred_element_type=F32)
    zm_ref[...] = jnp.dot(u, wm_ref[...], preferred_element_type=F32)
    zf_ref[...] = jnp.dot(u, wf_ref[...], preferred_element_type=F32)


def _inproj(h, g, wl, wh, wm, wf, tm):
    R = h.shape[0]
    row = lambda n: pl.BlockSpec((tm, n), lambda i: (i, 0))
    return pl.pallas_call(
        _inproj_kernel,
        grid=(R // tm,),
        in_specs=[row(D_MODEL), _full((1, D_MODEL)), _full(wl.shape), _full(wh.shape),
                  _full(wm.shape), _full(wf.shape)],
        out_specs=[row(wl.shape[1]), row(wh.shape[1]), row(wm.shape[1]), row(wf.shape[1])],
        out_shape=[jax.ShapeDtypeStruct((R, w.shape[1]), F32) for w in (wl, wh, wm, wf)],
        compiler_params=_cparams("parallel"),
        name="inproj",
    )(h, g, wl, wh, wm, wf)


def _rglru_kernel(z_ref, cw_ref, cb_ref, wa_ref, ba_ref, wx_ref, bx_ref, lam_ref, gn_ref, y_ref,
                  xbuf, a_s, b_s, h_s, hst):
    W = GROUP_WIDTH
    tt = y_ref.shape[0]

    @pl.when(pl.program_id(1) == 0)
    def _():
        xbuf[0:8, :] = jnp.zeros((8, W), F32)
        hst[...] = jnp.zeros((1, W), F32)

    xa = z_ref[:, 0:W]
    ga = z_ref[:, W:2 * W]
    xbuf[8:8 + tt, :] = xa
    u = (cb_ref[...] + xbuf[5:5 + tt, :] * cw_ref[0:1, :] + xbuf[6:6 + tt, :] * cw_ref[1:2, :]
         + xbuf[7:7 + tt, :] * cw_ref[2:3, :] + xa * cw_ref[3:4, :])
    xbuf[0:8, :] = xbuf[tt:tt + 8, :]

    ub = u.astype(BF16)
    r = _sigmoid(jnp.dot(ub, wa_ref[...], preferred_element_type=F32) + ba_ref[...])
    i = _sigmoid(jnp.dot(ub, wx_ref[...], preferred_element_type=F32) + bx_ref[...])
    lam = lam_ref[...]
    softplus_neg_lam = jnp.maximum(-lam, 0.0) + jnp.log(1.0 + jnp.exp(-jnp.abs(lam)))
    a = jnp.exp((-LRU_C) * r * softplus_neg_lam)
    b = jnp.sqrt(1.0 - a * a) * (i * u)

    row = lax.broadcasted_iota(jnp.int32, (tt, W), 0) & 7
    for s in (1, 2, 4):
        ok = row >= s
        b = jnp.where(ok, a * pltpu.roll(b, s, 0) + b, b)
        a = jnp.where(ok, a * pltpu.roll(a, s, 0), a)
    a_s[...] = a
    b_s[...] = b

    def group(gi, hprev):
        sl = pl.ds(pl.multiple_of(gi * 8, 8), 8)
        hg = b_s[sl, :] + a_s[sl, :] * hprev
        h_s[sl, :] = hg
        return hg[7:8, :]

    hst[...] = lax.fori_loop(0, tt // 8, group, hst[...], unroll=8)

    c0 = math.sqrt(2.0 / math.pi)
    gelu = 0.5 * ga * (1.0 + jnp.tanh(c0 * (ga + 0.044715 * (ga * ga * ga))))
    y = h_s[...] * gelu
    y_ref[...] = (y * _rms(y, W) * gn_ref[...]).astype(BF16)


def _rglru(zl, cw, cb, wa, ba, wx, bx, lam, gn, B, tt):
    R = zl.shape[0]
    nt = R // B // tt
    W = GROUP_WIDTH
    return pl.pallas_call(
        _rglru_kernel,
        grid=(B, nt),
        in_specs=[pl.BlockSpec((tt, 2 * W), lambda b, t: (b * nt + t, 0)),
                  _full((4, W)), _full((1, W)), _full((W, W)), _full((1, W)), _full((W, W)),
                  _full((1, W)), _full((1, W)), _full((1, W))],
        out_specs=pl.BlockSpec((tt, W), lambda b, t: (b * nt + t, 0)),
        out_shape=jax.ShapeDtypeStruct((R, W), BF16),
        scratch_shapes=[pltpu.VMEM((tt + 8, W), F32), pltpu.VMEM((tt, W), F32),
                        pltpu.VMEM((tt, W), F32), pltpu.VMEM((tt, W), F32), pltpu.VMEM((1, W), F32)],
        compiler_params=_cparams("parallel", "arbitrary"),
        name="rglru",
    )(zl, cw, cb, wa, ba, wx, bx, lam, gn)


def _hgrn2_kernel(z_ref, lb_ref, gn_ref, y_ref, qd_s, ke_s, v_s, dec_s, o_s, st_s):
    W = GROUP_WIDTH
    tt = y_ref.shape[0]
    C = HG_CHUNK

    @pl.when(pl.program_id(1) == 0)
    def _():
        st_s[...] = jnp.zeros((W, W), F32)

    q = z_ref[:, 0:W]
    fz = z_ref[:, W:2 * W]
    v = z_ref[:, 2 * W:3 * W]
    g = z_ref[:, 3 * W:4 * W]
    lb = lb_ref[...]
    la = jnp.log(lb)
    lq = jnp.log(1.0 - lb) + _log_sigmoid(fz)
    logf = jnp.maximum(la, lq) + jnp.log(1.0 + jnp.exp(-jnp.abs(la - lq)))
    kin = 1.0 - jnp.exp(logf)

    rowc = lax.broadcasted_iota(jnp.int32, (tt, W), 0) & (C - 1)
    b = logf
    rev = logf
    s = 1
    while s < C:
        b = b + jnp.where(rowc >= s, pltpu.roll(b, s, 0), 0.0)
        rev = rev + jnp.where(rowc + s < C, pltpu.roll(rev, tt - s, 0), 0.0)
        s *= 2
    tail = rev - logf
    qd = q * jnp.exp(b)
    kd = (kin * jnp.exp(-b)).astype(BF16)
    vb = v.astype(BF16)
    qd_s[...] = qd.astype(BF16)
    ke_s[...] = (kin * jnp.exp(tail)).astype(BF16)
    v_s[...] = vb
    dec_s[...] = jnp.exp(b + tail)

    lane_head = lax.broadcasted_iota(jnp.int32, (1, W), 1) // (W // N_HEADS)

    rr = lax.broadcasted_iota(jnp.int32, (HG_BLOCK, HG_BLOCK), 0)
    cc = lax.broadcasted_iota(jnp.int32, (HG_BLOCK, HG_BLOCK), 1)
    amask = (rr // C == cc // C) & (cc <= rr)
    for jb in range(tt // HG_BLOCK):
        sl = slice(jb * HG_BLOCK, (jb + 1) * HG_BLOCK)
        qb, kb, vv = qd[sl], kd[sl], vb[sl]
        acc = jnp.zeros((HG_BLOCK, W), F32)
        for hh in range(N_HEADS):
            hm = lane_head == hh
            qh = jnp.where(hm, qb, 0.0).astype(BF16)
            att = lax.dot_general(qh, kb, NT_DIMS, preferred_element_type=F32)
            att = jnp.where(amask, att, 0.0).astype(BF16)
            acc = jnp.where(hm, jnp.dot(att, vv, preferred_element_type=F32), acc)
        o_s[sl, :] = acc

    def chunk(c, carry):
        sl = pl.ds(pl.multiple_of(c * C, C), C)
        qc = qd_s[sl, :]
        st = st_s[...]
        qbd = jnp.concatenate([jnp.where(lane_head == hh, qc, jnp.zeros_like(qc))
                               for hh in range(N_HEADS)], axis=0)
        res = lax.dot_general(qbd, st.astype(BF16), NT_DIMS, preferred_element_type=F32)
        oi = jnp.zeros((C, W), F32)
        for hh in range(N_HEADS):
            oi = jnp.where(lane_head == hh, res[hh * C:(hh + 1) * C, :], oi)
        o_s[sl, :] = o_s[sl, :] + oi
        upd = lax.dot_general(v_s[sl, :], ke_s[sl, :], TN_DIMS, preferred_element_type=F32)
        st_s[...] = st * dec_s[pl.ds(c * C, 1), :] + upd
        return carry

    lax.fori_loop(0, tt // C, chunk, 0, unroll=math.gcd(tt // C, 8))

    y = o_s[...] * (g * _sigmoid(g))
    y_ref[...] = (y * _rms(y, W) * gn_ref[...]).astype(BF16)


def _hgrn2(zh, lb, gn, B, tt):
    R = zh.shape[0]
    nt = R // B // tt
    W = GROUP_WIDTH
    return pl.pallas_call(
        _hgrn2_kernel,
        grid=(B, nt),
        in_specs=[pl.BlockSpec((tt, 4 * W), lambda b, t: (b * nt + t, 0)), _full((1, W)), _full((1, W))],
        out_specs=pl.BlockSpec((tt, W), lambda b, t: (b * nt + t, 0)),
        out_shape=jax.ShapeDtypeStruct((R, W), BF16),
        scratch_shapes=[pltpu.VMEM((tt, W), BF16), pltpu.VMEM((tt, W), BF16), pltpu.VMEM((tt, W), BF16),
                        pltpu.VMEM((tt, W), F32), pltpu.VMEM((tt, W), F32), pltpu.VMEM((W, W), F32)],
        compiler_params=_cparams("parallel", "arbitrary"),
        name="hgrn2",
    )(zh, lb, gn)


def _rope(x, cos, s_lo, s_hi):
    return x * cos + pltpu.roll(x, 16, 1) * s_hi + pltpu.roll(x, HEAD_PAD - 16, 1) * s_lo


STAT_ROWS = 8
STAT_C_FIRST, STAT_C_LAST, STAT_Q2, STAT_K2 = range(4)
SHIFT_LANE = 100


def _stat_block(rows):
    srow = lax.broadcasted_iota(jnp.int32, (STAT_ROWS, HEAD_PAD), 0)
    stat = jnp.zeros((STAT_ROWS, HEAD_PAD), F32)
    for rr, val in rows:
        stat = jnp.where(srow == rr, val, stat)
    return stat


def _mla_prep_kernel(z_ref, gq_ref, wuq_ref, gkv_ref, wuk_ref, wuv_ref, gqn_ref, gkn_ref,
                     cos_ref, slo_ref, shi_ref, q_ref, k_ref, v_ref, stat_ref):
    P = HEAD_PAD
    ckv = z_ref[:, 0:P]
    krb = z_ref[:, P:2 * P]
    cq = z_ref[:, 2 * P:4 * P]
    rq = _rms_tiles(cq, MLA_Q_RANK)
    qn = (cq * jnp.concatenate([rq, rq], axis=1) * gq_ref[...]).astype(BF16)
    kvn = (ckv * _rms_tiles(ckv, MLA_KV_RANK) * gkv_ref[...]).astype(BF16)
    q = jnp.dot(qn, wuq_ref[...], preferred_element_type=F32)
    kn = jnp.dot(kvn, wuk_ref[...], preferred_element_type=F32)
    vv = jnp.dot(kvn, wuv_ref[...], preferred_element_type=F32)
    cos, slo, shi = cos_ref[...], slo_ref[...], shi_ref[...]
    lane = lax.broadcasted_iota(jnp.int32, (1, P), 1)
    qscale = (MLA_QK ** -0.5) * LOG2E
    q2 = jnp.zeros((1, P), F32)
    k2 = jnp.zeros((1, P), F32)
    for hh in range(N_HEADS):
        sl = slice(hh * P, (hh + 1) * P)
        qh = q[:, sl]
        qh = qh * _rms_tiles(qh, MLA_QK) * gqn_ref[...]
        qh = _rope(qh, cos, slo, shi) * qscale
        q2 = jnp.where(lane == hh, jnp.max(_row_ssq(qh), axis=0, keepdims=True), q2)
        q_ref[:, sl] = qh.astype(BF16)
        kh = kn[:, sl] + krb
        kh = kh * _rms_tiles(kh, MLA_QK) * gkn_ref[...]
        kh = _rope(kh, cos, slo, shi)
        k2 = jnp.where(lane == hh, jnp.max(_row_ssq(kh), axis=0, keepdims=True), k2)
        k_ref[:, sl] = jnp.where(lane == SHIFT_LANE, 1.0, kh).astype(BF16)
        v_ref[:, sl] = jnp.where(lane == ONES_LANE, 1.0, vv[:, sl]).astype(BF16)
    stat_ref[...] = _stat_block(((STAT_Q2, q2), (STAT_K2, k2)))


def _mla_prep(zm, gq, wuq, gkv, wuk, wuv, gqn, gkn, cos, slo, shi, tm):
    R = zm.shape[0]
    P = HEAD_PAD
    nt = cos.shape[0] // tm
    row = lambda n: pl.BlockSpec((tm, n), lambda i: (i, 0))
    tab = pl.BlockSpec((tm, P), lambda i: (i % nt, 0))
    out = jax.ShapeDtypeStruct((R, N_HEADS * P), BF16)
    return pl.pallas_call(
        _mla_prep_kernel,
        grid=(R // tm,),
        in_specs=[row(4 * P), _full((1, 2 * P)), _full(wuq.shape), _full((1, P)), _full(wuk.shape),
                  _full(wuv.shape), _full((1, P)), _full((1, P)), tab, tab, tab],
        out_specs=[row(N_HEADS * P)] * 3 + [pl.BlockSpec((STAT_ROWS, P), lambda i: (i, 0))],
        out_shape=[out, out, out, jax.ShapeDtypeStruct((R // tm * STAT_ROWS, P), F32)],
        compiler_params=_cparams("parallel"),
        name="mla_prep",
    )(zm, gq, wuq, gkv, wuk, wuv, gqn, gkn, cos, slo, shi)


def _fox_prep_kernel(z_ref, bf_ref, gqn_ref, gkn_ref, hsum_ref, place_ref, q_ref, k_ref, v_ref, stat_ref, carry):
    P = HEAD_PAD
    W = GROUP_WIDTH
    tt = q_ref.shape[0]

    @pl.when(pl.program_id(1) == 0)
    def _():
        carry[...] = jnp.zeros((1, P), F32)

    c = _log_sigmoid(z_ref[:, 3 * W:] + bf_ref[...])
    row = lax.broadcasted_iota(jnp.int32, (tt, P), 0)
    s = 1
    while s < tt:
        c = c + jnp.where(row >= s, pltpu.roll(c, s, 0), 0.0)
        s *= 2
    c = c + carry[...]
    carry[...] = c[tt - 1:tt, :]
    c = c * LOG2E
    c1 = c.astype(BF16).astype(F32)
    c2 = (c - c1).astype(BF16).astype(F32)
    c3 = c - c1 - c2

    def norm_place(x, gain):
        ssq = jnp.dot((x * x).astype(BF16), hsum_ref[...], preferred_element_type=F32)
        xn = (x * lax.rsqrt(ssq * (1.0 / FOX_HD) + EPS) * gain).astype(BF16)
        return jnp.dot(xn, place_ref[...], preferred_element_type=F32)

    qscale = (FOX_HD ** -0.5) * LOG2E
    qp = norm_place(z_ref[:, 0:W], gqn_ref[...] * qscale)
    kp = norm_place(z_ref[:, W:2 * W], gkn_ref[...])
    vp = jnp.dot(z_ref[:, 2 * W:3 * W].astype(BF16), place_ref[...], preferred_element_type=F32)

    lane = lax.broadcasted_iota(jnp.int32, (1, P), 1)
    q2 = jnp.zeros((1, P), F32)
    k2 = jnp.zeros((1, P), F32)
    for hh in range(N_HEADS):
        hs = slice(hh * P, (hh + 1) * P)
        p1 = c1[:, hh:hh + 1]
        p2 = c2[:, hh:hh + 1]
        p3 = c3[:, hh:hh + 1]
        qh = qp[:, hs]
        q2 = jnp.where(lane == hh, jnp.max(_row_ssq(qh), axis=0, keepdims=True), q2)
        qh = jnp.where(lane == 64, p1, jnp.where(lane == 65, p2, jnp.where(lane == 66, p3, qh)))
        qh = jnp.where((lane >= 67) & (lane < 70), 1.0, qh)
        q_ref[:, hs] = qh.astype(BF16)
        kh = kp[:, hs]
        k2 = jnp.where(lane == hh, jnp.max(_row_ssq(kh), axis=0, keepdims=True), k2)
        kh = jnp.where(lane == 67, -p1, jnp.where(lane == 68, -p2, jnp.where(lane == 69, -p3, kh)))
        kh = jnp.where(((lane >= 64) & (lane < 67)) | (lane == SHIFT_LANE), 1.0, kh)
        k_ref[:, hs] = kh.astype(BF16)
        v_ref[:, hs] = jnp.where(lane == ONES_LANE, 1.0, vp[:, hs]).astype(BF16)

    stat_ref[...] = _stat_block(((STAT_C_FIRST, c[0:1, :]), (STAT_C_LAST, c[tt - 1:tt, :]),
                                 (STAT_Q2, q2), (STAT_K2, k2)))


def _fox_prep(zf, bf, gqn, gkn, B, tt):
    R = zf.shape[0]
    nt = R // B // tt
    P = HEAD_PAD
    W = GROUP_WIDTH
    blk = lambda n: pl.BlockSpec((tt, n), lambda b, t: (b * nt + t, 0))
    out = jax.ShapeDtypeStruct((R, N_HEADS * P), BF16)
    head_sum = jnp.kron(jnp.eye(N_HEADS, dtype=F32), jnp.ones((FOX_HD, FOX_HD), F32)).astype(BF16)
    place = _pad_heads(jnp.eye(W, dtype=F32), FOX_HD).astype(BF16)
    return pl.pallas_call(
        _fox_prep_kernel,
        grid=(B, nt),
        in_specs=[blk(zf.shape[1]), _full((1, P)), _full((1, W)), _full((1, W)), _full((W, W)),
                  _full((W, N_HEADS * P))],
        out_specs=[blk(N_HEADS * P)] * 3 + [pl.BlockSpec((STAT_ROWS, P), lambda b, t: (b * nt + t, 0))],
        out_shape=[out, out, out, jax.ShapeDtypeStruct((B * nt * STAT_ROWS, P), F32)],
        scratch_shapes=[pltpu.VMEM((1, P), F32)],
        compiler_params=_cparams("parallel", "arbitrary"),
        name="fox_prep",
    )(zf, bf, gqn, gkn, head_sum, place)


SKIP_LOG2_MARGIN = 40.0
NORM_SLACK = 1.02


def _fox_first_chunk(stats, B, nt):
    st = stats.reshape(B, nt, STAT_ROWS, HEAD_PAD)[..., :N_HEADS]
    c_first, c_last = st[:, :, STAT_C_FIRST], st[:, :, STAT_C_LAST]
    bound = jnp.sqrt(jnp.max(st[:, :, STAT_Q2], axis=1) * jnp.max(st[:, :, STAT_K2], axis=1)) * NORM_SLACK
    gap = 2.0 * bound[:, None, None, :] + c_first[:, :, None, :] - c_last[:, None, :, :]
    earlier = jnp.arange(nt)[None, :] < jnp.arange(nt)[:, None]
    skip = (gap < -SKIP_LOG2_MARGIN) & earlier[None, :, :, None]
    return jnp.min(jnp.sum(skip, axis=2), axis=-1).astype(jnp.int32).reshape(-1)


MAX_SCORE_SHIFT = 48.0


def _score_bound(stats, B, nt):
    st = stats.reshape(B, nt, STAT_ROWS, HEAD_PAD)[..., :N_HEADS]
    bound = jnp.sqrt(jnp.max(st[:, :, STAT_Q2], axis=1) * jnp.max(st[:, :, STAT_K2], axis=1)) * NORM_SLACK
    return bound.reshape(-1), jnp.all(bound <= MAX_SCORE_SHIFT)


def _attn_kernel(first_ref, shift_ref, q_ref, k_ref, v_ref, gn_ref, y_ref, m_s, acc_s, o_s, q_s, *,
                 chunk_causal, tk, fixed_shift):
    P = HEAD_PAD
    tq = q_ref.shape[0]
    lp = k_ref.shape[0]
    blk = pl.program_id(0) * pl.num_programs(1) + pl.program_id(1)
    q0 = pl.program_id(1) * tq

    qpos = q0 + lax.broadcasted_iota(jnp.int32, (tq, 1), 0)
    if chunk_causal:
        qlim = N_META + CHUNK * ((qpos + (CHUNK - N_META)) // CHUNK)
        reach = N_META
    else:
        qlim = qpos + 1
        reach = 0
    n_full = (q0 + reach) // tk
    n_diag = (jnp.minimum(q0 + tq, lp) + tk - 1) // tk

    if fixed_shift:
        lane = lax.broadcasted_iota(jnp.int32, (1, P), 1)
        for hh in range(N_HEADS):
            shift = jnp.full((1, P), -shift_ref[pl.program_id(0) * N_HEADS + hh], F32).astype(BF16)
            q_s[hh] = jnp.where(lane == SHIFT_LANE, shift, q_ref[:, hh * P:(hh + 1) * P])
    else:
        m_s[...] = jnp.full(m_s.shape, -jnp.inf, F32)
    acc_s[...] = jnp.zeros(acc_s.shape, F32)

    def visit(k0, width, masked):
        ks = pl.ds(pl.multiple_of(k0, SEQ_ALIGN), width)
        if masked:
            vis = (k0 + lax.broadcasted_iota(jnp.int32, (1, width), 1)) < qlim
        for hh in range(N_HEADS):
            hs = slice(hh * P, (hh + 1) * P)
            qh = q_s[hh] if fixed_shift else q_ref[:, hs]
            s = lax.dot_general(qh, k_ref[ks, hs], NT_DIMS, preferred_element_type=F32)
            if masked:
                s = jnp.where(vis, s, -jnp.inf)
            if fixed_shift:
                acc_s[hh] += jnp.dot(jnp.exp2(s).astype(BF16), v_ref[ks, hs], preferred_element_type=F32)
                continue
            tiles = [s[:, c * P:(c + 1) * P] for c in range(width // P)]
            mx = tiles[0]
            for t in tiles[1:]:
                mx = jnp.maximum(mx, t)
            m_old = m_s[hh]
            m_new = jnp.maximum(m_old, jnp.max(mx, axis=-1, keepdims=True))
            p = jnp.concatenate([jnp.exp2((t - m_new).astype(BF16)) for t in tiles], axis=1)
            acc_s[hh] = jnp.exp2(m_old - m_new) * acc_s[hh] + jnp.dot(
                p, v_ref[ks, hs], preferred_element_type=F32)
            m_s[hh] = m_new

    def full_body(j, carry):
        visit(j * tk, tk, False)
        return carry

    def masked_body(j, carry):
        visit(j * tk, tk, True)
        return carry

    lax.fori_loop(first_ref[blk], n_full, full_body, 0)
    lax.fori_loop(n_full, n_diag, masked_body, 0)
    if chunk_causal:
        @pl.when(q0 + tq < lp)
        def _():
            visit(q0 + tq, SEQ_ALIGN, True)

    hd = GROUP_WIDTH // N_HEADS
    for hh in range(N_HEADS):
        acc = acc_s[hh]
        o_s[:, hh * hd:(hh + 1) * hd] = acc[:, 0:hd] / acc[:, ONES_LANE:ONES_LANE + 1]
    y = o_s[...]
    y_ref[...] = (y * _rms(y, GROUP_WIDTH) * gn_ref[...]).astype(BF16)


def _attention(first_chunk, stats, q, k, v, gn, B, tq, tk, chunk_causal):
    R = q.shape[0]
    lp = R // B
    nq = lp // tq
    P = HEAD_PAD
    W = GROUP_WIDTH
    shift, shift_is_safe = _score_bound(stats, B, nq)
    kv_spec = pl.BlockSpec((lp, N_HEADS * P), lambda b, i, fc, sh: (b, 0))

    def run(fixed_shift):
        return pl.pallas_call(
            functools.partial(_attn_kernel, chunk_causal=chunk_causal, tk=tk, fixed_shift=fixed_shift),
            grid_spec=pltpu.PrefetchScalarGridSpec(
                num_scalar_prefetch=2,
                grid=(B, nq),
                in_specs=[pl.BlockSpec((tq, N_HEADS * P), lambda b, i, fc, sh: (b * nq + i, 0)),
                          kv_spec, kv_spec, pl.BlockSpec((1, W), lambda b, i, fc, sh: (0, 0))],
                out_specs=pl.BlockSpec((tq, W), lambda b, i, fc, sh: (b * nq + i, 0)),
                scratch_shapes=[pltpu.VMEM((N_HEADS, tq, P), F32), pltpu.VMEM((N_HEADS, tq, P), F32),
                                pltpu.VMEM((tq, W), F32), pltpu.VMEM((N_HEADS, tq, P), BF16)],
            ),
            out_shape=jax.ShapeDtypeStruct((R, W), BF16),
            compiler_params=_cparams("parallel", "arbitrary"),
            name=("mla_attn" if chunk_causal else "fox_attn") + ("_shift" if fixed_shift else ""),
        )(first_chunk, shift, q, k, v, gn)

    return lax.cond(shift_is_safe, lambda: run(True), lambda: run(False))


ROUTE_E1, ROUTE_E2, ROUTE_R1, ROUTE_R2, ROUTE_G1, ROUTE_G2 = range(6)


def _outproj_kernel(h_ref, ya_ref, yb_ref, yc_ref, yd_ref, w_ref, g_ref, *rest, with_router):
    W = GROUP_WIDTH
    if with_router:
        wr_ref, hn_ref, u_ref, route_ref, cnt_ref = rest
    else:
        hn_ref, u_ref = rest
    acc = h_ref[...]
    for gi, y_ref in enumerate((ya_ref, yb_ref, yc_ref, yd_ref)):
        acc = acc + jnp.dot(y_ref[...], w_ref[gi * W:(gi + 1) * W, :], preferred_element_type=F32)
    hn_ref[...] = acc
    u = acc * _rms(acc, D_MODEL) * g_ref[...]
    if not with_router:
        u_ref[...] = u.astype(BF16)
        return
    u_ref[...] = u
    tm = u.shape[0]

    @pl.when(pl.program_id(0) == 0)
    def _():
        cnt_ref[...] = jnp.zeros(cnt_ref.shape, F32)

    u_hi = u.astype(BF16)
    u_lo = (u - u_hi.astype(F32)).astype(BF16)
    logits = (jnp.dot(u_hi, wr_ref[0], preferred_element_type=F32)
              + jnp.dot(u_lo, wr_ref[0], preferred_element_type=F32)
              + jnp.dot(u_hi, wr_ref[1], preferred_element_type=F32))
    lane = lax.broadcasted_iota(jnp.int32, logits.shape, 1).astype(F32)
    lg = jnp.where(lane < N_EXPERTS, logits, -jnp.inf)
    m1 = jnp.max(lg, axis=-1, keepdims=True)
    i1 = jnp.min(jnp.where(lg == m1, lane, 1e9), axis=-1, keepdims=True)
    lg2 = jnp.where(lane == i1, -jnp.inf, lg)
    m2 = jnp.max(lg2, axis=-1, keepdims=True)
    i2 = jnp.min(jnp.where(lg2 == m2, lane, 1e9), axis=-1, keepdims=True)
    e = jnp.exp(m2 - m1)
    g1 = 1.0 / (1.0 + e)
    picks = jnp.where(lane == i1, 1.0, 0.0) + jnp.where(lane == i2, 1.0, 0.0)
    earlier = (lax.broadcasted_iota(jnp.int32, (tm, tm), 0) > lax.broadcasted_iota(jnp.int32, (tm, tm), 1))
    base = cnt_ref[...] + jnp.dot(earlier.astype(BF16), picks.astype(BF16), preferred_element_type=F32)
    r1 = jnp.sum(jnp.where(lane == i1, base, 0.0), axis=-1, keepdims=True)
    r2 = jnp.sum(jnp.where(lane == i2, base, 0.0), axis=-1, keepdims=True)
    cnt_ref[...] = cnt_ref[...] + jnp.sum(picks, axis=0, keepdims=True)
    rec = jnp.zeros(logits.shape, F32)
    for ln, val in ((ROUTE_E1, i1), (ROUTE_E2, i2), (ROUTE_R1, r1), (ROUTE_R2, r2),
                    (ROUTE_G1, g1), (ROUTE_G2, e * g1)):
        rec = jnp.where(lane == ln, val, rec)
    route_ref[...] = rec


def _outproj(h, ya, yb, yc, yd, w, g, wr, tm):
    R = h.shape[0]
    W = GROUP_WIDTH
    with_router = wr is not None
    row = lambda n: pl.BlockSpec((tm, n), lambda i: (i, 0))
    in_specs = [row(D_MODEL), row(W), row(W), row(W), row(W), _full(w.shape), _full((1, D_MODEL))]
    out_specs = [row(D_MODEL), row(D_MODEL)]
    out_shape = [jax.ShapeDtypeStruct((R, D_MODEL), F32),
                 jax.ShapeDtypeStruct((R, D_MODEL), F32 if with_router else BF16)]
    args = [h, ya, yb, yc, yd, w, g]
    if with_router:
        in_specs.append(_full(wr.shape))
        out_specs += [row(wr.shape[-1]), _full((1, wr.shape[-1]))]
        out_shape += [jax.ShapeDtypeStruct((R, wr.shape[-1]), F32), jax.ShapeDtypeStruct((1, wr.shape[-1]), F32)]
        args.append(wr)
    return pl.pallas_call(
        functools.partial(_outproj_kernel, with_router=with_router),
        grid=(R // tm,),
        in_specs=in_specs, out_specs=out_specs, out_shape=out_shape,
        compiler_params=_cparams("arbitrary" if with_router else "parallel"),
        name="outproj_router" if with_router else "outproj",
    )(*args)


def _ffn_kernel(u_ref, h_ref, wg_ref, wu_ref, wd_ref, o_ref):
    @pl.when(pl.program_id(1) == 0)
    def _():
        o_ref[...] = h_ref[...]

    u = u_ref[...]
    a = jnp.dot(u, wg_ref[...], preferred_element_type=F32)
    b = jnp.dot(u, wu_ref[...], preferred_element_type=F32)
    hid = (a * _sigmoid(a) * b).astype(BF16)
    o_ref[...] += jnp.dot(hid, wd_ref[...], preferred_element_type=F32)


def _ffn(u, h, wg, wu, wd, tm, tf):
    R = h.shape[0]
    dff = wg.shape[1]
    return pl.pallas_call(
        _ffn_kernel,
        grid=(R // tm, dff // tf),
        in_specs=[pl.BlockSpec((tm, D_MODEL), lambda i, f: (i, 0)),
                  pl.BlockSpec((tm, D_MODEL), lambda i, f: (i, 0)),
                  pl.BlockSpec((D_MODEL, tf), lambda i, f: (0, f)),
                  pl.BlockSpec((D_MODEL, tf), lambda i, f: (0, f)),
                  pl.BlockSpec((tf, D_MODEL), lambda i, f: (f, 0))],
        out_specs=pl.BlockSpec((tm, D_MODEL), lambda i, f: (i, 0)),
        out_shape=jax.ShapeDtypeStruct((R, D_MODEL), F32),
        compiler_params=_cparams("parallel", "arbitrary"),
        name="ffn",
    )(u, h, wg, wu, wd)


def _row_copy(src_ref, src_row, dst_ref, dst_row, sem):
    return pltpu.make_async_copy(src_ref.at[pl.ds(src_row, 1)], dst_ref.at[pl.ds(dst_row, 1)], sem)


def _dispatch_kernel(pos1_ref, pos2_ref, u_ref, xs_in_ref, xs_ref, sem):
    del xs_in_ref
    tm = u_ref.shape[0]
    t0 = pl.program_id(0) * tm

    def issue(r, carry):
        _row_copy(u_ref, r, xs_ref, pos1_ref[t0 + r], sem).start()
        _row_copy(u_ref, r, xs_ref, pos2_ref[t0 + r], sem).start()
        return carry

    lax.fori_loop(0, tm, issue, 0, unroll=8)

    def drain(r, carry):
        _row_copy(u_ref, r, xs_ref, 0, sem).wait()
        _row_copy(u_ref, r, xs_ref, 0, sem).wait()
        return carry

    lax.fori_loop(0, tm, drain, 0, unroll=8)


def _dispatch(pos1, pos2, u, n_slots, tm):
    R = u.shape[0]
    xs0 = jnp.zeros((n_slots, D_MODEL), F32)
    return pl.pallas_call(
        _dispatch_kernel,
        grid_spec=pltpu.PrefetchScalarGridSpec(
            num_scalar_prefetch=2,
            grid=(R // tm,),
            in_specs=[pl.BlockSpec((tm, D_MODEL), lambda i, p1, p2: (i, 0)),
                      pl.BlockSpec(memory_space=pl.ANY)],
            out_specs=pl.BlockSpec(memory_space=pl.ANY),
            scratch_shapes=[pltpu.SemaphoreType.DMA],
        ),
        out_shape=jax.ShapeDtypeStruct((n_slots, D_MODEL), F32),
        input_output_aliases={3: 0},
        compiler_params=_cparams("arbitrary"),
        name="moe_dispatch",
    )(pos1, pos2, u, xs0)


def _expert_ffn_kernel(te_ref, nt_ref, x_ref, wg_ref, wu_ref, wd_ref, y_ref, xb_s):
    f = pl.program_id(1)
    used = pl.program_id(0) < nt_ref[0]

    @pl.when(jnp.logical_not(used) & (f == 0))
    def _():
        y_ref[...] = jnp.zeros(y_ref.shape, F32)

    @pl.when(used)
    def _():
        @pl.when(f == 0)
        def _():
            xb_s[...] = x_ref[...].astype(BF16)

        x = xb_s[...]
        a = jnp.dot(x, wg_ref[...], preferred_element_type=F32)
        b = jnp.dot(x, wu_ref[...], preferred_element_type=F32)
        hid = (a * _sigmoid(a) * b).astype(BF16)
        out = jnp.dot(hid, wd_ref[...], preferred_element_type=F32)

        @pl.when(f == 0)
        def _():
            y_ref[...] = out

        @pl.when(f > 0)
        def _():
            y_ref[...] += out


def _expert_ffn(tile_expert, n_tiles, xs, wg, wu, wd, tm, tf):
    n_slots = xs.shape[0]
    dff = wg.shape[2]
    row_map = lambda i, f, te, nt: (jnp.minimum(i, nt[0] - 1), 0)
    out_map = lambda i, f, te, nt: (i, 0)
    return pl.pallas_call(
        _expert_ffn_kernel,
        grid_spec=pltpu.PrefetchScalarGridSpec(
            num_scalar_prefetch=2,
            grid=(n_slots // tm, dff // tf),
            in_specs=[pl.BlockSpec((tm, D_MODEL), row_map),
                      pl.BlockSpec((None, D_MODEL, tf), lambda i, f, te, nt: (te[i], 0, f)),
                      pl.BlockSpec((None, D_MODEL, tf), lambda i, f, te, nt: (te[i], 0, f)),
                      pl.BlockSpec((None, tf, D_MODEL), lambda i, f, te, nt: (te[i], f, 0))],
            out_specs=pl.BlockSpec((tm, D_MODEL), out_map),
            scratch_shapes=[pltpu.VMEM((tm, D_MODEL), BF16)],
        ),
        out_shape=jax.ShapeDtypeStruct((n_slots, D_MODEL), F32),
        compiler_params=_cparams("arbitrary", "arbitrary"),
        name="moe_expert_ffn",
    )(tile_expert, n_tiles, xs, wg, wu, wd)


def _combine_kernel(pos1_ref, pos2_ref, h_ref, route_ref, ys_ref, o_ref, y1_s, y2_s, sem, *, row0):
    tm = h_ref.shape[0]
    t0 = row0(pl.program_id(0), pl.program_id(1))

    def issue(r, carry):
        _row_copy(ys_ref, pos1_ref[t0 + r], y1_s, r, sem).start()
        _row_copy(ys_ref, pos2_ref[t0 + r], y2_s, r, sem).start()
        return carry

    lax.fori_loop(0, tm, issue, 0, unroll=8)

    def drain(r, carry):
        _row_copy(ys_ref, 0, y1_s, r, sem).wait()
        _row_copy(ys_ref, 0, y2_s, r, sem).wait()
        return carry

    lax.fori_loop(0, tm, drain, 0, unroll=8)
    rec = route_ref[...]
    g1 = rec[:, ROUTE_G1:ROUTE_G1 + 1]
    g2 = rec[:, ROUTE_G2:ROUTE_G2 + 1]
    o_ref[...] = h_ref[...] + g1 * y1_s[...] + g2 * y2_s[...]


def _combine(pos1, pos2, h, route, ys, tm, frames=None):
    R = h.shape[0]
    if frames is None:
        grid = (R // tm, 1)
        row0 = lambda i, j: i * tm
        in_rows = lambda n: pl.BlockSpec((tm, n), lambda i, j, p1, p2: (i, 0))
        out_rows = R
        out_spec = pl.BlockSpec((tm, D_MODEL), lambda i, j, p1, p2: (i, 0))
    else:
        B, S, lp = frames
        grid = (B, S // tm)
        row0 = lambda b, w: b * lp + N_META + w * tm
        in_rows = lambda n: pl.BlockSpec((pl.Element(tm), pl.Element(n)),
                                         lambda b, w, p1, p2: (pl.multiple_of(row0(b, w), 8), 0))
        out_rows = B * S
        out_spec = pl.BlockSpec((tm, D_MODEL), lambda b, w, p1, p2: (b * (S // tm) + w, 0))
    return pl.pallas_call(
        functools.partial(_combine_kernel, row0=row0),
        grid_spec=pltpu.PrefetchScalarGridSpec(
            num_scalar_prefetch=2,
            grid=grid,
            in_specs=[in_rows(D_MODEL), in_rows(route.shape[1]), pl.BlockSpec(memory_space=pl.ANY)],
            out_specs=out_spec,
            scratch_shapes=[pltpu.VMEM((tm, D_MODEL), F32), pltpu.VMEM((tm, D_MODEL), F32),
                            pltpu.SemaphoreType.DMA],
        ),
        out_shape=jax.ShapeDtypeStruct((out_rows, D_MODEL), F32),
        compiler_params=_cparams("arbitrary", "arbitrary"),
        name="moe_combine",
    )(pos1, pos2, h, route, ys)


def _moe(u, h, route, counts, wg, wu, wd, tm_rows, tm_expert, tf, frames=None):
    R = h.shape[0]
    ne = wg.shape[0]
    cnt = counts[0, :ne].astype(jnp.int32)
    padded = -(-cnt // tm_expert) * tm_expert
    ends = jnp.cumsum(padded)
    offs = ends - padded
    col = lambda ln: route[:, ln].astype(jnp.int32)
    pos1 = jnp.take(offs, col(ROUTE_E1)) + col(ROUTE_R1)
    pos2 = jnp.take(offs, col(ROUTE_E2)) + col(ROUTE_R2)
    n_slots = (2 * R // tm_expert + ne) * tm_expert
    n_tiles = (ends[-1] // tm_expert).reshape(1)
    tile_start = jnp.arange(n_slots // tm_expert, dtype=jnp.int32) * tm_expert
    tile_expert = jnp.sum(tile_start[:, None] >= ends[None, :], axis=1).astype(jnp.int32)
    last_expert = jnp.sum((ends[-1] - 1) >= ends).astype(jnp.int32)
    tile_expert = jnp.minimum(tile_expert, last_expert)

    xs = _dispatch(pos1, pos2, u, n_slots, tm_rows)
    ys = _expert_ffn(tile_expert, n_tiles, xs, wg, wu, wd, tm_expert, tf)
    if frames is None:
        return _combine(pos1, pos2, h, route, ys, tm_rows)
    return _combine(pos1, pos2, h, route, ys, _tile(frames[1], tm_rows), frames)


def _pad_heads(w, real):
    rows = w.shape[0]
    w = w.reshape(rows, N_HEADS, real)
    return jnp.pad(w, ((0, 0), (0, 0), (0, HEAD_PAD - real))).reshape(rows, N_HEADS * HEAD_PAD)


def _pad_cols(w, n):
    return jnp.pad(w, ((0, 0), (0, n - w.shape[1])))


def _row(v, n=None):
    v = v.reshape(1, -1).astype(F32)
    return v if n is None else _pad_cols(v, n)


def _block_diag(w):
    nb, c, d = w.shape
    eye = jnp.eye(nb, dtype=w.dtype)
    return (eye[:, None, :, None] * w[:, :, None, :]).reshape(nb * c, nb * d)


def _rope_tables(lp):
    half = MLA_ROPE // 2
    inv = ROPE_THETA ** (-jnp.arange(half, dtype=F32) / half)
    ang = jnp.arange(lp, dtype=jnp.int32).astype(F32)[:, None] * inv[None, :]
    cos, sin = jnp.cos(ang), jnp.sin(ang)
    one = jnp.ones((lp, MLA_NOPE), F32)
    zero = jnp.zeros((lp, MLA_NOPE), F32)
    zh = jnp.zeros((lp, half), F32)
    tail1 = jnp.ones((lp, HEAD_PAD - MLA_QK), F32)
    tail0 = jnp.zeros((lp, HEAD_PAD - MLA_QK), F32)
    cos_t = jnp.concatenate([one, cos, cos, tail1], axis=1)
    s_lo = jnp.concatenate([zero, -sin, zh, tail0], axis=1)
    s_hi = jnp.concatenate([zero, zh, sin, tail0], axis=1)
    return cos_t, s_lo, s_hi


def kernel(x, meta, norm1_g, norm2_g, w_in, w_out, out_norm_g, lru_conv_w, lru_conv_b, lru_wa, lru_ba, lru_wx, lru_bx, lru_lambda, hg_lb_logits, mla_gq, mla_w_uq, mla_gkv, mla_w_ukv, mla_gqn, mla_gkn, fox_gqn, fox_gkn, fox_bf, ffn_w_gate, ffn_w_up, ffn_w_down, moe_w_router, moe_w_gate, moe_w_up, moe_w_down):
    B, S, _ = x.shape
    depth = w_in.shape[0]
    L = N_META + S
    lp = -(-L // SEQ_ALIGN) * SEQ_ALIGN
    R = B * lp
    W = GROUP_WIDTH
    P = HEAD_PAD

    tt = _tile(lp, 640)
    tq = tk = tt

    h = jnp.concatenate([jnp.broadcast_to(meta[None].astype(x.dtype), (B, N_META, D_MODEL)), x,
                         jnp.zeros((B, lp - L, D_MODEL), x.dtype)], axis=1).reshape(R, D_MODEL)
    cos_t, s_lo, s_hi = _rope_tables(lp)
    lb_cum = jnp.cumsum(jax.nn.softmax(hg_lb_logits.astype(F32), axis=0), axis=0)

    o = 0
    offs = []
    for n in (W, W, W, W, W, W, MLA_Q_RANK, MLA_KV_RANK, MLA_ROPE, W, W, W, N_HEADS):
        offs.append(o)
        o += n
    (o_xa, _, o_hq, _, _, _, o_cq, o_ckv, o_kr, o_fq, o_fk, o_fv, o_ff) = offs

    for l in range(depth):
        w = w_in[l]
        wl = w[:, o_xa:o_xa + 2 * W].astype(BF16)
        wh = w[:, o_hq:o_hq + 4 * W].astype(BF16)
        wm = jnp.concatenate([
            w[:, o_ckv:o_ckv + MLA_KV_RANK],
            jnp.zeros((D_MODEL, MLA_NOPE), F32), w[:, o_kr:o_kr + MLA_ROPE],
            jnp.zeros((D_MODEL, P - MLA_QK), F32),
            _pad_cols(w[:, o_cq:o_cq + MLA_Q_RANK], 2 * P)], axis=1).astype(BF16)
        wf = jnp.concatenate([w[:, o_fq:o_fq + 3 * W], _pad_cols(w[:, o_ff:o_ff + N_HEADS], P)],
                             axis=1).astype(BF16)
        zl, zh, zm, zf = _inproj(h, _row(norm1_g[l]), wl, wh, wm, wf, tt)

        gn = out_norm_g[l].astype(F32)
        ya = _rglru(zl, lru_conv_w[l].astype(F32), _row(lru_conv_b[l]),
                    _block_diag(lru_wa[l]).astype(BF16), _row(lru_ba[l]),
                    _block_diag(lru_wx[l]).astype(BF16), _row(lru_bx[l]),
                    _row(lru_lambda[l]), _row(gn[0:W]), B, tt)
        yb = _hgrn2(zh, _row(lb_cum[l] - lb_cum[0]), _row(gn[W:2 * W]), B, tt)

        wuq = jnp.pad(_pad_heads(mla_w_uq[l], MLA_QK), ((0, 2 * P - MLA_Q_RANK), (0, 0))).astype(BF16)
        wukv = mla_w_ukv[l].reshape(MLA_KV_RANK, N_HEADS, MLA_NOPE + MLA_V)
        wuk = _pad_heads(wukv[:, :, :MLA_NOPE].reshape(MLA_KV_RANK, -1), MLA_NOPE).astype(BF16)
        wuv = _pad_heads(wukv[:, :, MLA_NOPE:].reshape(MLA_KV_RANK, -1), MLA_V).astype(BF16)
        q, k, v, stats = _mla_prep(zm, _row(mla_gq[l], 2 * P), wuq, _row(mla_gkv[l]), wuk, wuv,
                                   _row(mla_gqn[l], P), _row(mla_gkn[l], P), cos_t, s_lo, s_hi, tt)
        visit_all = jnp.zeros((B * (lp // tq),), jnp.int32)
        yc = _attention(visit_all, stats, q, k, v, _row(gn[2 * W:3 * W]), B, tq, tk, True)

        q, k, v, stats = _fox_prep(zf, _row(fox_bf[l], P), _row(jnp.tile(fox_gqn[l], N_HEADS)),
                                   _row(jnp.tile(fox_gkn[l], N_HEADS)), B, tt)
        yd = _attention(_fox_first_chunk(stats, B, lp // tt), stats, q, k, v, _row(gn[3 * W:4 * W]),
                        B, tq, tk, False)

        wo = w_out[l].astype(BF16)
        if l % 2 == 0:
            hn, u2 = _outproj(h, ya, yb, yc, yd, wo, _row(norm2_g[l]), None, tt)
            j = l // 2
            h = _ffn(u2, hn, ffn_w_gate[j].astype(BF16), ffn_w_up[j].astype(BF16),
                     ffn_w_down[j].astype(BF16), tt, 1408)
        else:
            j = l // 2
            wr = _pad_cols(moe_w_router[j].astype(F32), P)
            wr_hi = wr.astype(BF16)
            wr = jnp.stack([wr_hi, (wr - wr_hi.astype(F32)).astype(BF16)])
            hn, u2, route, counts = _outproj(h, ya, yb, yc, yd, wo, _row(norm2_g[l]), wr, tt)
            last = l == depth - 1
            h = _moe(u2, hn, route, counts, moe_w_gate[j].astype(BF16), moe_w_up[j].astype(BF16),
                     moe_w_down[j].astype(BF16), tt, 512, 1792, (B, S, lp) if last else None)
            if last:
                return h.reshape(B, S, D_MODEL)
    return h.reshape(B, lp, D_MODEL)[:, N_META:L]
```

```python
import functools
import math

import jax
import jax.numpy as jnp
from jax import lax
from jax.experimental import pallas as pl
from jax.experimental.pallas import tpu as pltpu

F32 = jnp.float32
BF16 = jnp.bfloat16

D_MODEL = 1024
N_META = 16
CHUNK = 64
SEQ_ALIGN = 128
EPS = 1e-6
GROUP_WIDTH = 256
N_HEADS = 4
HEAD_PAD = 128
LRU_C = 8.0
HG_CHUNK = 16
HG_BLOCK = 128
MLA_NOPE, MLA_ROPE, MLA_V = 64, 32, 64
MLA_QK = MLA_NOPE + MLA_ROPE
MLA_Q_RANK, MLA_KV_RANK = 192, 128
ROPE_THETA = 10000.0
FOX_HD = 64
N_EXPERTS = 8
LOG2E = 1.4426950408889634
ONES_LANE = 64
VMEM_LIMIT = 56 * 1024 * 1024

NT_DIMS = (((1,), (1,)), ((), ()))
TN_DIMS = (((0,), (0,)), ((), ()))


def _cparams(*sem):
    return pltpu.CompilerParams(dimension_semantics=sem, vmem_limit_bytes=VMEM_LIMIT)


def _tile(n, pref, align=SEQ_ALIGN):
    best = None
    for t in range(align, min(n, pref) + 1, align):
        if n % t == 0:
            best = t
    assert best is not None, (n, pref, align)
    return best


def _rms(x, width):
    return lax.rsqrt(jnp.sum(x * x, axis=-1, keepdims=True) * (1.0 / width) + EPS)


def _row_ssq(x):
    ones = jnp.ones((x.shape[1], HEAD_PAD), BF16)
    return jnp.dot((x * x).astype(BF16), ones, preferred_element_type=F32)


def _rms_tiles(x, width):
    return lax.rsqrt(_row_ssq(x) * (1.0 / width) + EPS)


def _sigmoid(x):
    return 0.5 * jnp.tanh(0.5 * x) + 0.5


def _log_sigmoid(x):
    return jnp.minimum(x, 0.0) - jnp.log(1.0 + jnp.exp(-jnp.abs(x)))


def _full(shape):
    return pl.BlockSpec(shape, lambda *_: (0,) * len(shape))


def _inproj_kernel(h_ref, g_ref, wl_ref, wh_ref, wm_ref, wf_ref, zl_ref, zh_ref, zm_ref, zf_ref):
    x = h_ref[...]
    u = (x * _rms(x, D_MODEL) * g_ref[...]).astype(BF16)
    zl_ref[...] = jnp.dot(u, wl_ref[...], preferred_element_type=F32)
    zh_ref[...] = jnp.dot(u, wh_ref[...], preferred_element_type=F32)
    zm_ref[...] = jnp.dot(u, wm_ref[...], preferred_element_type=F32)
    zf_ref[...] = jnp.dot(u, wf_ref[...], preferred_element_type=F32)


def _inproj(h, g, wl, wh, wm, wf, tm):
    R = h.shape[0]
    row = lambda n: pl.BlockSpec((tm, n), lambda i: (i, 0))
    return pl.pallas_call(
        _inproj_kernel,
        grid=(R // tm,),
        in_specs=[row(D_MODEL), _full((1, D_MODEL)), _full(wl.shape), _full(wh.shape),
                  _full(wm.shape), _full(wf.shape)],
        out_specs=[row(wl.shape[1]), row(wh.shape[1]), row(wm.shape[1]), row(wf.shape[1])],
        out_shape=[jax.ShapeDtypeStruct((R, w.shape[1]), F32) for w in (wl, wh, wm, wf)],
        compiler_params=_cparams("parallel"),
        name="inproj",
    )(h, g, wl, wh, wm, wf)


def _rglru_kernel(z_ref, cw_ref, cb_ref, wa_ref, ba_ref, wx_ref, bx_ref, lam_ref, gn_ref, y_ref,
                  xbuf, a_s, b_s, h_s, hst):
    W = GROUP_WIDTH
    tt = y_ref.shape[0]

    @pl.when(pl.program_id(1) == 0)
    def _():
        xbuf[0:8, :] = jnp.zeros((8, W), F32)
        hst[...] = jnp.zeros((1, W), F32)

    xa = z_ref[:, 0:W]
    ga = z_ref[:, W:2 * W]
    xbuf[8:8 + tt, :] = xa
    u = (cb_ref[...] + xbuf[5:5 + tt, :] * cw_ref[0:1, :] + xbuf[6:6 + tt, :] * cw_ref[1:2, :]
         + xbuf[7:7 + tt, :] * cw_ref[2:3, :] + xa * cw_ref[3:4, :])
    xbuf[0:8, :] = xbuf[tt:tt + 8, :]

    ub = u.astype(BF16)
    r = _sigmoid(jnp.dot(ub, wa_ref[...], preferred_element_type=F32) + ba_ref[...])
    i = _sigmoid(jnp.dot(ub, wx_ref[...], preferred_element_type=F32) + bx_ref[...])
    lam = lam_ref[...]
    softplus_neg_lam = jnp.maximum(-lam, 0.0) + jnp.log(1.0 + jnp.exp(-jnp.abs(lam)))
    a = jnp.exp((-LRU_C) * r * softplus_neg_lam)
    b = jnp.sqrt(1.0 - a * a) * (i * u)

    row = lax.broadcasted_iota(jnp.int32, (tt, W), 0) & 7
    for s in (1, 2, 4):
        ok = row >= s
        b = jnp.where(ok, a * pltpu.roll(b, s, 0) + b, b)
        a = jnp.where(ok, a * pltpu.roll(a, s, 0), a)
    a_s[...] = a
    b_s[...] = b

    def group(gi, hprev):
        sl = pl.ds(pl.multiple_of(gi * 8, 8), 8)
        hg = b_s[sl, :] + a_s[sl, :] * hprev
        h_s[sl, :] = hg
        return hg[7:8, :]

    hst[...] = lax.fori_loop(0, tt // 8, group, hst[...], unroll=8)

    c0 = math.sqrt(2.0 / math.pi)
    gelu = 0.5 * ga * (1.0 + jnp.tanh(c0 * (ga + 0.044715 * (ga * ga * ga))))
    y = h_s[...] * gelu
    y_ref[...] = (y * _rms(y, W) * gn_ref[...]).astype(BF16)


def _rglru(zl, cw, cb, wa, ba, wx, bx, lam, gn, B, tt):
    R = zl.shape[0]
    nt = R // B // tt
    W = GROUP_WIDTH
    return pl.pallas_call(
        _rglru_kernel,
        grid=(B, nt),
        in_specs=[pl.BlockSpec((tt, 2 * W), lambda b, t: (b * nt + t, 0)),
                  _full((4, W)), _full((1, W)), _full((W, W)), _full((1, W)), _full((W, W)),
                  _full((1, W)), _full((1, W)), _full((1, W))],
        out_specs=pl.BlockSpec((tt, W), lambda b, t: (b * nt + t, 0)),
        out_shape=jax.ShapeDtypeStruct((R, W), BF16),
        scratch_shapes=[pltpu.VMEM((tt + 8, W), F32), pltpu.VMEM((tt, W), F32),
                        pltpu.VMEM((tt, W), F32), pltpu.VMEM((tt, W), F32), pltpu.VMEM((1, W), F32)],
        compiler_params=_cparams("parallel", "arbitrary"),
        name="rglru",
    )(zl, cw, cb, wa, ba, wx, bx, lam, gn)


def _hgrn2_kernel(z_ref, lb_ref, gn_ref, y_ref, qd_s, ke_s, v_s, dec_s, o_s, st_s):
    W = GROUP_WIDTH
    nb, tt = y_ref.shape[0], y_ref.shape[1]
    C = HG_CHUNK

    @pl.when(pl.program_id(0) == 0)
    def _():
        st_s[...] = jnp.zeros(st_s.shape, F32)

    lb = lb_ref[...]
    la = jnp.log(lb)
    l1 = jnp.log(1.0 - lb)
    rowc = lax.broadcasted_iota(jnp.int32, (tt, W), 0) & (C - 1)
    lane_head = lax.broadcasted_iota(jnp.int32, (1, W), 1) // (W // N_HEADS)
    rr = lax.broadcasted_iota(jnp.int32, (HG_BLOCK, HG_BLOCK), 0)
    cc = lax.broadcasted_iota(jnp.int32, (HG_BLOCK, HG_BLOCK), 1)
    amask = (rr // C == cc // C) & (cc <= rr)

    for bi in range(nb):
        q = z_ref[bi, :, 0:W]
        fz = z_ref[bi, :, W:2 * W]
        v = z_ref[bi, :, 2 * W:3 * W]
        lq = l1 + _log_sigmoid(fz)
        logf = jnp.maximum(la, lq) + jnp.log(1.0 + jnp.exp(-jnp.abs(la - lq)))
        kin = 1.0 - jnp.exp(logf)

        b = logf
        s = 1
        while s < C:
            b = b + jnp.where(rowc >= s, pltpu.roll(b, s, 0), 0.0)
            s *= 2
        b3 = b.reshape(tt // C, C, W)
        tail = (jnp.broadcast_to(b3[:, C - 1:C, :], b3.shape) - b3).reshape(tt, W)
        qd = q * jnp.exp(b)
        kd = (kin * jnp.exp(-b)).astype(BF16)
        vb = v.astype(BF16)
        qd_s[bi] = qd.astype(BF16)
        ke_s[bi] = (kin * jnp.exp(tail)).astype(BF16)
        v_s[bi] = vb
        dec_s[bi] = jnp.exp(b + tail)

        for jb in range(tt // HG_BLOCK):
            sl = slice(jb * HG_BLOCK, (jb + 1) * HG_BLOCK)
            qb, kb, vv = qd[sl], kd[sl], vb[sl]
            acc = jnp.zeros((HG_BLOCK, W), F32)
            for hh in range(N_HEADS):
                hm = lane_head == hh
                qh = jnp.where(hm, qb, 0.0).astype(BF16)
                att = lax.dot_general(qh, kb, NT_DIMS, preferred_element_type=F32)
                att = jnp.where(amask, att, 0.0).astype(BF16)
                acc = jnp.where(hm, jnp.dot(att, vv, preferred_element_type=F32), acc)
            o_s[bi, sl, :] = acc

    def chunk(c, carry):
        sl = pl.ds(pl.multiple_of(c * C, C), C)
        for bi in range(nb):
            qc = qd_s[bi, sl, :]
            st = st_s[bi]
            qbd = jnp.concatenate([jnp.where(lane_head == hh, qc, jnp.zeros_like(qc))
                                   for hh in range(N_HEADS)], axis=0)
            res = lax.dot_general(qbd, st.astype(BF16), NT_DIMS, preferred_element_type=F32)
            oi = jnp.zeros((C, W), F32)
            for hh in range(N_HEADS):
                oi = jnp.where(lane_head == hh, res[hh * C:(hh + 1) * C, :], oi)
            o_s[bi, sl, :] = o_s[bi, sl, :] + oi
            upd = lax.dot_general(v_s[bi, sl, :], ke_s[bi, sl, :], TN_DIMS, preferred_element_type=F32)
            st_s[bi] = st * dec_s[bi, pl.ds(c * C, 1), :] + upd
        return carry

    lax.fori_loop(0, tt // C, chunk, 0, unroll=math.gcd(tt // C, 4))

    for bi in range(nb):
        g = z_ref[bi, :, 3 * W:4 * W]
        y = o_s[bi] * (g * _sigmoid(g))
        y_ref[bi] = (y * _rms(y, W) * gn_ref[...]).astype(BF16)


def _hgrn2(zh, lb, gn, B, tt):
    R = zh.shape[0]
    lp = R // B
    W = GROUP_WIDTH
    y = pl.pallas_call(
        _hgrn2_kernel,
        grid=(lp // tt,),
        in_specs=[pl.BlockSpec((B, tt, 4 * W), lambda t: (0, t, 0)), _full((1, W)), _full((1, W))],
        out_specs=pl.BlockSpec((B, tt, W), lambda t: (0, t, 0)),
        out_shape=jax.ShapeDtypeStruct((B, lp, W), BF16),
        scratch_shapes=[pltpu.VMEM((B, tt, W), BF16), pltpu.VMEM((B, tt, W), BF16), pltpu.VMEM((B, tt, W), BF16),
                        pltpu.VMEM((B, tt, W), F32), pltpu.VMEM((B, tt, W), F32), pltpu.VMEM((B, W, W), F32)],
        compiler_params=_cparams("arbitrary"),
        name="hgrn2",
    )(zh.reshape(B, lp, 4 * W), lb, gn)
    return y.reshape(R, W)


def _rope(x, cos, s_lo, s_hi):
    return x * cos + pltpu.roll(x, 16, 1) * s_hi + pltpu.roll(x, HEAD_PAD - 16, 1) * s_lo


STAT_ROWS = 8
STAT_C_FIRST, STAT_C_LAST, STAT_Q2, STAT_K2 = range(4)
SHIFT_LANE = 100


def _stat_block(rows):
    srow = lax.broadcasted_iota(jnp.int32, (STAT_ROWS, HEAD_PAD), 0)
    stat = jnp.zeros((STAT_ROWS, HEAD_PAD), F32)
    for rr, val in rows:
        stat = jnp.where(srow == rr, val, stat)
    return stat


def _mla_prep_kernel(z_ref, gq_ref, wuq_ref, gkv_ref, wuk_ref, wuv_ref, gqn_ref, gkn_ref,
                     cos_ref, slo_ref, shi_ref, q_ref, k_ref, v_ref, stat_ref):
    P = HEAD_PAD
    ckv = z_ref[:, 0:P]
    krb = z_ref[:, P:2 * P]
    cq = z_ref[:, 2 * P:4 * P]
    rq = _rms_tiles(cq, MLA_Q_RANK)
    qn = (cq * jnp.concatenate([rq, rq], axis=1) * gq_ref[...]).astype(BF16)
    kvn = (ckv * _rms_tiles(ckv, MLA_KV_RANK) * gkv_ref[...]).astype(BF16)
    q = jnp.dot(qn, wuq_ref[...], preferred_element_type=F32)
    kn = jnp.dot(kvn, wuk_ref[...], preferred_element_type=F32)
    vv = jnp.dot(kvn, wuv_ref[...], preferred_element_type=F32)
    cos, slo, shi = cos_ref[...], slo_ref[...], shi_ref[...]
    lane = lax.broadcasted_iota(jnp.int32, (1, P), 1)
    qscale = (MLA_QK ** -0.5) * LOG2E
    q2 = jnp.zeros((1, P), F32)
    k2 = jnp.zeros((1, P), F32)
    for hh in range(N_HEADS):
        sl = slice(hh * P, (hh + 1) * P)
        qh = q[:, sl]
        qh = qh * _rms_tiles(qh, MLA_QK) * gqn_ref[...]
        qh = _rope(qh, cos, slo, shi) * qscale
        q2 = jnp.where(lane == hh, jnp.max(_row_ssq(qh), axis=0, keepdims=True), q2)
        q_ref[:, sl] = qh.astype(BF16)
        kh = kn[:, sl] + krb
        kh = kh * _rms_tiles(kh, MLA_QK) * gkn_ref[...]
        kh = _rope(kh, cos, slo, shi)
        k2 = jnp.where(lane == hh, jnp.max(_row_ssq(kh), axis=0, keepdims=True), k2)
        k_ref[:, sl] = jnp.where(lane == SHIFT_LANE, 1.0, kh).astype(BF16)
        v_ref[:, sl] = jnp.where(lane == ONES_LANE, 1.0, vv[:, sl]).astype(BF16)
    stat_ref[...] = _stat_block(((STAT_Q2, q2), (STAT_K2, k2)))


def _mla_prep(zm, gq, wuq, gkv, wuk, wuv, gqn, gkn, cos, slo, shi, tm):
    R = zm.shape[0]
    P = HEAD_PAD
    nt = cos.shape[0] // tm
    row = lambda n: pl.BlockSpec((tm, n), lambda i: (i, 0))
    tab = pl.BlockSpec((tm, P), lambda i: (i % nt, 0))
    out = jax.ShapeDtypeStruct((R, N_HEADS * P), BF16)
    return pl.pallas_call(
        _mla_prep_kernel,
        grid=(R // tm,),
        in_specs=[row(4 * P), _full((1, 2 * P)), _full(wuq.shape), _full((1, P)), _full(wuk.shape),
                  _full(wuv.shape), _full((1, P)), _full((1, P)), tab, tab, tab],
        out_specs=[row(N_HEADS * P)] * 3 + [pl.BlockSpec((STAT_ROWS, P), lambda i: (i, 0))],
        out_shape=[out, out, out, jax.ShapeDtypeStruct((R // tm * STAT_ROWS, P), F32)],
        compiler_params=_cparams("parallel"),
        name="mla_prep",
    )(zm, gq, wuq, gkv, wuk, wuv, gqn, gkn, cos, slo, shi)


def _fox_prep_kernel(z_ref, bf_ref, gqn_ref, gkn_ref, hsum_ref, place_ref, q_ref, k_ref, v_ref, stat_ref, carry):
    P = HEAD_PAD
    W = GROUP_WIDTH
    tt = q_ref.shape[0]

    @pl.when(pl.program_id(1) == 0)
    def _():
        carry[...] = jnp.zeros((1, P), F32)

    c = _log_sigmoid(z_ref[:, 3 * W:] + bf_ref[...])
    row = lax.broadcasted_iota(jnp.int32, (tt, P), 0)
    s = 1
    while s < tt:
        c = c + jnp.where(row >= s, pltpu.roll(c, s, 0), 0.0)
        s *= 2
    c = c + carry[...]
    carry[...] = c[tt - 1:tt, :]
    c = c * LOG2E
    c1 = c.astype(BF16).astype(F32)
    c2 = (c - c1).astype(BF16).astype(F32)
    c3 = c - c1 - c2

    def norm_place(x, gain):
        ssq = jnp.dot((x * x).astype(BF16), hsum_ref[...], preferred_element_type=F32)
        xn = (x * lax.rsqrt(ssq * (1.0 / FOX_HD) + EPS) * gain).astype(BF16)
        return jnp.dot(xn, place_ref[...], preferred_element_type=F32)

    qscale = (FOX_HD ** -0.5) * LOG2E
    qp = norm_place(z_ref[:, 0:W], gqn_ref[...] * qscale)
    kp = norm_place(z_ref[:, W:2 * W], gkn_ref[...])
    vp = jnp.dot(z_ref[:, 2 * W:3 * W].astype(BF16), place_ref[...], preferred_element_type=F32)

    lane = lax.broadcasted_iota(jnp.int32, (1, P), 1)
    q2 = jnp.zeros((1, P), F32)
    k2 = jnp.zeros((1, P), F32)
    for hh in range(N_HEADS):
        hs = slice(hh * P, (hh + 1) * P)
        p1 = c1[:, hh:hh + 1]
        p2 = c2[:, hh:hh + 1]
        p3 = c3[:, hh:hh + 1]
        qh = qp[:, hs]
        q2 = jnp.where(lane == hh, jnp.max(_row_ssq(qh), axis=0, keepdims=True), q2)
        qh = jnp.where(lane == 64, p1, jnp.where(lane == 65, p2, jnp.where(lane == 66, p3, qh)))
        qh = jnp.where((lane >= 67) & (lane < 70), 1.0, qh)
        q_ref[:, hs] = qh.astype(BF16)
        kh = kp[:, hs]
        k2 = jnp.where(lane == hh, jnp.max(_row_ssq(kh), axis=0, keepdims=True), k2)
        kh = jnp.where(lane == 67, -p1, jnp.where(lane == 68, -p2, jnp.where(lane == 69, -p3, kh)))
        kh = jnp.where(((lane >= 64) & (lane < 67)) | (lane == SHIFT_LANE), 1.0, kh)
        k_ref[:, hs] = kh.astype(BF16)
        v_ref[:, hs] = jnp.where(lane == ONES_LANE, 1.0, vp[:, hs]).astype(BF16)

    stat_ref[...] = _stat_block(((STAT_C_FIRST, c[0:1, :]), (STAT_C_LAST, c[tt - 1:tt, :]),
                                 (STAT_Q2, q2), (STAT_K2, k2)))


def _fox_prep(zf, bf, gqn, gkn, B, tt):
    R = zf.shape[0]
    nt = R // B // tt
    P = HEAD_PAD
    W = GROUP_WIDTH
    blk = lambda n: pl.BlockSpec((tt, n), lambda b, t: (b * nt + t, 0))
    out = jax.ShapeDtypeStruct((R, N_HEADS * P), BF16)
    head_sum = jnp.kron(jnp.eye(N_HEADS, dtype=F32), jnp.ones((FOX_HD, FOX_HD), F32)).astype(BF16)
    place = _pad_heads(jnp.eye(W, dtype=F32), FOX_HD).astype(BF16)
    return pl.pallas_call(
        _fox_prep_kernel,
        grid=(B, nt),
        in_specs=[blk(zf.shape[1]), _full((1, P)), _full((1, W)), _full((1, W)), _full((W, W)),
                  _full((W, N_HEADS * P))],
        out_specs=[blk(N_HEADS * P)] * 3 + [pl.BlockSpec((STAT_ROWS, P), lambda b, t: (b * nt + t, 0))],
        out_shape=[out, out, out, jax.ShapeDtypeStruct((B * nt * STAT_ROWS, P), F32)],
        scratch_shapes=[pltpu.VMEM((1, P), F32)],
        compiler_params=_cparams("parallel", "arbitrary"),
        name="fox_prep",
    )(zf, bf, gqn, gkn, head_sum, place)


SKIP_LOG2_MARGIN = 40.0
NORM_SLACK = 1.02


def _fox_first_chunk(stats, B, nt):
    st = stats.reshape(B, nt, STAT_ROWS, HEAD_PAD)[..., :N_HEADS]
    c_first, c_last = st[:, :, STAT_C_FIRST], st[:, :, STAT_C_LAST]
    bound = jnp.sqrt(jnp.max(st[:, :, STAT_Q2], axis=1) * jnp.max(st[:, :, STAT_K2], axis=1)) * NORM_SLACK
    gap = 2.0 * bound[:, None, None, :] + c_first[:, :, None, :] - c_last[:, None, :, :]
    earlier = jnp.arange(nt)[None, :] < jnp.arange(nt)[:, None]
    skip = (gap < -SKIP_LOG2_MARGIN) & earlier[None, :, :, None]
    return jnp.min(jnp.sum(skip, axis=2), axis=-1).astype(jnp.int32).reshape(-1)


MAX_SCORE_SHIFT = 48.0


def _score_bound(stats, B, nt):
    st = stats.reshape(B, nt, STAT_ROWS, HEAD_PAD)[..., :N_HEADS]
    bound = jnp.sqrt(jnp.max(st[:, :, STAT_Q2], axis=1) * jnp.max(st[:, :, STAT_K2], axis=1)) * NORM_SLACK
    return bound.reshape(-1), jnp.all(bound <= MAX_SCORE_SHIFT)


def _attn_kernel(first_ref, shift_ref, q_ref, k_ref, v_ref, gn_ref, y_ref, m_s, acc_s, o_s, q_s, *,
                 chunk_causal, tk, fixed_shift):
    P = HEAD_PAD
    tq = q_ref.shape[0]
    lp = k_ref.shape[0]
    blk = pl.program_id(0) * pl.num_programs(1) + pl.program_id(1)
    q0 = pl.program_id(1) * tq

    qpos = q0 + lax.broadcasted_iota(jnp.int32, (tq, 1), 0)
    if chunk_causal:
        qlim = N_META + CHUNK * ((qpos + (CHUNK - N_META)) // CHUNK)
        reach = N_META
    else:
        qlim = qpos + 1
        reach = 0
    n_full = (q0 + reach) // tk
    n_diag = (jnp.minimum(q0 + tq, lp) + tk - 1) // tk

    if fixed_shift:
        lane = lax.broadcasted_iota(jnp.int32, (1, P), 1)
        for hh in range(N_HEADS):
            shift = jnp.full((1, P), -shift_ref[pl.program_id(0) * N_HEADS + hh], F32).astype(BF16)
            q_s[hh] = jnp.where(lane == SHIFT_LANE, shift, q_ref[:, hh * P:(hh + 1) * P])
    else:
        m_s[...] = jnp.full(m_s.shape, -jnp.inf, F32)
    acc_s[...] = jnp.zeros(acc_s.shape, F32)

    def visit(k0, width, masked):
        ks = pl.ds(pl.multiple_of(k0, SEQ_ALIGN), width)
        if masked:
            vis = (k0 + lax.broadcasted_iota(jnp.int32, (1, width), 1)) < qlim
        for hh in range(N_HEADS):
            hs = slice(hh * P, (hh + 1) * P)
            qh = q_s[hh] if fixed_shift else q_ref[:, hs]
            s = lax.dot_general(qh, k_ref[ks, hs], NT_DIMS, preferred_element_type=F32)
            if masked:
                s = jnp.where(vis, s, -jnp.inf)
            if fixed_shift:
                acc_s[hh] += jnp.dot(jnp.exp2(s).astype(BF16), v_ref[ks, hs], preferred_element_type=F32)
                continue
            tiles = [s[:, c * P:(c + 1) * P] for c in range(width // P)]
            mx = tiles[0]
            for t in tiles[1:]:
                mx = jnp.maximum(mx, t)
            m_old = m_s[hh]
            m_new = jnp.maximum(m_old, jnp.max(mx, axis=-1, keepdims=True))
            p = jnp.concatenate([jnp.exp2((t - m_new).astype(BF16)) for t in tiles], axis=1)
            acc_s[hh] = jnp.exp2(m_old - m_new) * acc_s[hh] + jnp.dot(
                p, v_ref[ks, hs], preferred_element_type=F32)
            m_s[hh] = m_new

    def full_body(j, carry):
        visit(j * tk, tk, False)
        return carry

    def masked_body(j, carry):
        visit(j * tk, tk, True)
        return carry

    lax.fori_loop(first_ref[blk], n_full, full_body, 0)
    lax.fori_loop(n_full, n_diag, masked_body, 0)
    if chunk_causal:
        @pl.when(q0 + tq < lp)
        def _():
            visit(q0 + tq, SEQ_ALIGN, True)

    hd = GROUP_WIDTH // N_HEADS
    for hh in range(N_HEADS):
        acc = acc_s[hh]
        o_s[:, hh * hd:(hh + 1) * hd] = acc[:, 0:hd] / acc[:, ONES_LANE:ONES_LANE + 1]
    y = o_s[...]
    y_ref[...] = (y * _rms(y, GROUP_WIDTH) * gn_ref[...]).astype(BF16)


def _attention(first_chunk, stats, q, k, v, gn, B, tq, tk, chunk_causal):
    R = q.shape[0]
    lp = R // B
    nq = lp // tq
    P = HEAD_PAD
    W = GROUP_WIDTH
    shift, shift_is_safe = _score_bound(stats, B, nq)
    kv_spec = pl.BlockSpec((lp, N_HEADS * P), lambda b, i, fc, sh: (b, 0))

    def run(fixed_shift):
        return pl.pallas_call(
            functools.partial(_attn_kernel, chunk_causal=chunk_causal, tk=tk, fixed_shift=fixed_shift),
            grid_spec=pltpu.PrefetchScalarGridSpec(
                num_scalar_prefetch=2,
                grid=(B, nq),
                in_specs=[pl.BlockSpec((tq, N_HEADS * P), lambda b, i, fc, sh: (b * nq + i, 0)),
                          kv_spec, kv_spec, pl.BlockSpec((1, W), lambda b, i, fc, sh: (0, 0))],
                out_specs=pl.BlockSpec((tq, W), lambda b, i, fc, sh: (b * nq + i, 0)),
                scratch_shapes=[pltpu.VMEM((N_HEADS, tq, P), F32), pltpu.VMEM((N_HEADS, tq, P), F32),
                                pltpu.VMEM((tq, W), F32), pltpu.VMEM((N_HEADS, tq, P), BF16)],
            ),
            out_shape=jax.ShapeDtypeStruct((R, W), BF16),
            compiler_params=_cparams("parallel", "arbitrary"),
            name=("mla_attn" if chunk_causal else "fox_attn") + ("_shift" if fixed_shift else ""),
        )(first_chunk, shift, q, k, v, gn)

    return lax.cond(shift_is_safe, lambda: run(True), lambda: run(False))


ROUTE_E1, ROUTE_E2, ROUTE_R1, ROUTE_R2, ROUTE_G1, ROUTE_G2 = range(6)


def _outproj_kernel(h_ref, ya_ref, yb_ref, yc_ref, yd_ref, w_ref, g_ref, *rest, with_router):
    W = GROUP_WIDTH
    if with_router:
        wr_ref, hn_ref, u_ref, route_ref, cnt_ref = rest
    else:
        hn_ref, u_ref = rest
    acc = h_ref[...]
    for gi, y_ref in enumerate((ya_ref, yb_ref, yc_ref, yd_ref)):
        acc = acc + jnp.dot(y_ref[...], w_ref[gi * W:(gi + 1) * W, :], preferred_element_type=F32)
    hn_ref[...] = acc
    u = acc * _rms(acc, D_MODEL) * g_ref[...]
    if not with_router:
        u_ref[...] = u.astype(BF16)
        return
    u_ref[...] = u
    tm = u.shape[0]

    @pl.when(pl.program_id(0) == 0)
    def _():
        cnt_ref[...] = jnp.zeros(cnt_ref.shape, F32)

    u_hi = u.astype(BF16)
    u_lo = (u - u_hi.astype(F32)).astype(BF16)
    hi_both = jnp.dot(u_hi, wr_ref[...], preferred_element_type=F32)
    logits = (hi_both[:, :HEAD_PAD] + hi_both[:, HEAD_PAD:]
              + jnp.dot(u_lo, wr_ref[:, :HEAD_PAD], preferred_element_type=F32))
    lane = lax.broadcasted_iota(jnp.int32, logits.shape, 1).astype(F32)
    lg = jnp.where(lane < N_EXPERTS, logits, -jnp.inf)
    m1 = jnp.max(lg, axis=-1, keepdims=True)
    i1 = jnp.min(jnp.where(lg == m1, lane, 1e9), axis=-1, keepdims=True)
    lg2 = jnp.where(lane == i1, -jnp.inf, lg)
    m2 = jnp.max(lg2, axis=-1, keepdims=True)
    i2 = jnp.min(jnp.where(lg2 == m2, lane, 1e9), axis=-1, keepdims=True)
    e = jnp.exp(m2 - m1)
    g1 = 1.0 / (1.0 + e)
    picks = jnp.where(lane == i1, 1.0, 0.0) + jnp.where(lane == i2, 1.0, 0.0)
    earlier = (lax.broadcasted_iota(jnp.int32, (tm, tm), 0) > lax.broadcasted_iota(jnp.int32, (tm, tm), 1))
    base = cnt_ref[...] + jnp.dot(earlier.astype(BF16), picks.astype(BF16), preferred_element_type=F32)
    r1 = jnp.sum(jnp.where(lane == i1, base, 0.0), axis=-1, keepdims=True)
    r2 = jnp.sum(jnp.where(lane == i2, base, 0.0), axis=-1, keepdims=True)
    cnt_ref[...] = cnt_ref[...] + jnp.sum(picks, axis=0, keepdims=True)
    rec = jnp.zeros(logits.shape, F32)
    for ln, val in ((ROUTE_E1, i1), (ROUTE_E2, i2), (ROUTE_R1, r1), (ROUTE_R2, r2),
                    (ROUTE_G1, g1), (ROUTE_G2, e * g1)):
        rec = jnp.where(lane == ln, val, rec)
    route_ref[...] = rec


def _outproj(h, ya, yb, yc, yd, w, g, wr, tm):
    R = h.shape[0]
    W = GROUP_WIDTH
    with_router = wr is not None
    row = lambda n: pl.BlockSpec((tm, n), lambda i: (i, 0))
    in_specs = [row(D_MODEL), row(W), row(W), row(W), row(W), _full(w.shape), _full((1, D_MODEL))]
    out_specs = [row(D_MODEL), row(D_MODEL)]
    out_shape = [jax.ShapeDtypeStruct((R, D_MODEL), F32),
                 jax.ShapeDtypeStruct((R, D_MODEL), F32 if with_router else BF16)]
    args = [h, ya, yb, yc, yd, w, g]
    if with_router:
        in_specs.append(_full(wr.shape))
        out_specs += [row(HEAD_PAD), _full((1, HEAD_PAD))]
        out_shape += [jax.ShapeDtypeStruct((R, HEAD_PAD), F32), jax.ShapeDtypeStruct((1, HEAD_PAD), F32)]
        args.append(wr)
    return pl.pallas_call(
        functools.partial(_outproj_kernel, with_router=with_router),
        grid=(R // tm,),
        in_specs=in_specs, out_specs=out_specs, out_shape=out_shape,
        compiler_params=_cparams("arbitrary" if with_router else "parallel"),
        name="outproj_router" if with_router else "outproj",
    )(*args)


def _ffn_kernel(u_ref, h_ref, wg_ref, wu_ref, wd_ref, o_ref):
    @pl.when(pl.program_id(1) == 0)
    def _():
        o_ref[...] = h_ref[...]

    u = u_ref[...]
    a = jnp.dot(u, wg_ref[...], preferred_element_type=F32)
    b = jnp.dot(u, wu_ref[...], preferred_element_type=F32)
    hid = (a * _sigmoid(a) * b).astype(BF16)
    o_ref[...] += jnp.dot(hid, wd_ref[...], preferred_element_type=F32)


def _ffn(u, h, wg, wu, wd, tm, tf):
    R = h.shape[0]
    dff = wg.shape[1]
    return pl.pallas_call(
        _ffn_kernel,
        grid=(R // tm, dff // tf),
        in_specs=[pl.BlockSpec((tm, D_MODEL), lambda i, f: (i, 0)),
                  pl.BlockSpec((tm, D_MODEL), lambda i, f: (i, 0)),
                  pl.BlockSpec((D_MODEL, tf), lambda i, f: (0, f)),
                  pl.BlockSpec((D_MODEL, tf), lambda i, f: (0, f)),
                  pl.BlockSpec((tf, D_MODEL), lambda i, f: (f, 0))],
        out_specs=pl.BlockSpec((tm, D_MODEL), lambda i, f: (i, 0)),
        out_shape=jax.ShapeDtypeStruct((R, D_MODEL), F32),
        compiler_params=_cparams("parallel", "arbitrary"),
        name="ffn",
    )(u, h, wg, wu, wd)


def _row_copy(src_ref, src_row, dst_ref, dst_row, sem):
    return pltpu.make_async_copy(src_ref.at[pl.ds(src_row, 1)], dst_ref.at[pl.ds(dst_row, 1)], sem)


def _dispatch_kernel(pos1_ref, pos2_ref, u_ref, xs_in_ref, xs_ref, sem):
    del xs_in_ref
    tm = u_ref.shape[0]
    t0 = pl.program_id(0) * tm

    def issue(r, carry):
        _row_copy(u_ref, r, xs_ref, pos1_ref[t0 + r], sem).start()
        _row_copy(u_ref, r, xs_ref, pos2_ref[t0 + r], sem).start()
        return carry

    lax.fori_loop(0, tm, issue, 0, unroll=8)

    def drain(r, carry):
        _row_copy(u_ref, r, xs_ref, 0, sem).wait()
        _row_copy(u_ref, r, xs_ref, 0, sem).wait()
        return carry

    lax.fori_loop(0, tm, drain, 0, unroll=8)


def _dispatch(pos1, pos2, u, n_slots, tm):
    R = u.shape[0]
    xs0 = jnp.zeros((n_slots, D_MODEL), F32)
    return pl.pallas_call(
        _dispatch_kernel,
        grid_spec=pltpu.PrefetchScalarGridSpec(
            num_scalar_prefetch=2,
            grid=(R // tm,),
            in_specs=[pl.BlockSpec((tm, D_MODEL), lambda i, p1, p2: (i, 0)),
                      pl.BlockSpec(memory_space=pl.ANY)],
            out_specs=pl.BlockSpec(memory_space=pl.ANY),
            scratch_shapes=[pltpu.SemaphoreType.DMA],
        ),
        out_shape=jax.ShapeDtypeStruct((n_slots, D_MODEL), F32),
        input_output_aliases={3: 0},
        compiler_params=_cparams("arbitrary"),
        name="moe_dispatch",
    )(pos1, pos2, u, xs0)


def _expert_ffn_kernel(te_ref, nt_ref, x_ref, wg_ref, wu_ref, wd_ref, y_ref, xb_s):
    f = pl.program_id(1)
    used = pl.program_id(0) < nt_ref[0]

    @pl.when(jnp.logical_not(used) & (f == 0))
    def _():
        y_ref[...] = jnp.zeros(y_ref.shape, F32)

    @pl.when(used)
    def _():
        @pl.when(f == 0)
        def _():
            xb_s[...] = x_ref[...].astype(BF16)

        x = xb_s[...]
        a = jnp.dot(x, wg_ref[...], preferred_element_type=F32)
        b = jnp.dot(x, wu_ref[...], preferred_element_type=F32)
        hid = (a * _sigmoid(a) * b).astype(BF16)
        out = jnp.dot(hid, wd_ref[...], preferred_element_type=F32)

        @pl.when(f == 0)
        def _():
            y_ref[...] = out

        @pl.when(f > 0)
        def _():
            y_ref[...] += out


def _expert_ffn(tile_expert, n_tiles, xs, wg, wu, wd, tm, tf):
    n_slots = xs.shape[0]
    dff = wg.shape[2]
    row_map = lambda i, f, te, nt: (jnp.minimum(i, nt[0] - 1), 0)
    out_map = lambda i, f, te, nt: (i, 0)
    return pl.pallas_call(
        _expert_ffn_kernel,
        grid_spec=pltpu.PrefetchScalarGridSpec(
            num_scalar_prefetch=2,
            grid=(n_slots // tm, dff // tf),
            in_specs=[pl.BlockSpec((tm, D_MODEL), row_map),
                      pl.BlockSpec((None, D_MODEL, tf), lambda i, f, te, nt: (te[i], 0, f)),
                      pl.BlockSpec((None, D_MODEL, tf), lambda i, f, te, nt: (te[i], 0, f)),
                      pl.BlockSpec((None, tf, D_MODEL), lambda i, f, te, nt: (te[i], f, 0))],
            out_specs=pl.BlockSpec((tm, D_MODEL), out_map),
            scratch_shapes=[pltpu.VMEM((tm, D_MODEL), BF16)],
        ),
        out_shape=jax.ShapeDtypeStruct((n_slots, D_MODEL), F32),
        compiler_params=_cparams("arbitrary", "arbitrary"),
        name="moe_expert_ffn",
    )(tile_expert, n_tiles, xs, wg, wu, wd)


def _combine_kernel(pos1_ref, pos2_ref, h_ref, route_ref, ys_ref, o_ref, y1_s, y2_s, sem, *, row0):
    tm = h_ref.shape[0]
    t0 = row0(pl.program_id(0), pl.program_id(1))

    def issue(r, carry):
        _row_copy(ys_ref, pos1_ref[t0 + r], y1_s, r, sem).start()
        _row_copy(ys_ref, pos2_ref[t0 + r], y2_s, r, sem).start()
        return carry

    lax.fori_loop(0, tm, issue, 0, unroll=8)

    def drain(r, carry):
        _row_copy(ys_ref, 0, y1_s, r, sem).wait()
        _row_copy(ys_ref, 0, y2_s, r, sem).wait()
        return carry

    lax.fori_loop(0, tm, drain, 0, unroll=8)
    rec = route_ref[...]
    g1 = rec[:, ROUTE_G1:ROUTE_G1 + 1]
    g2 = rec[:, ROUTE_G2:ROUTE_G2 + 1]
    o_ref[...] = h_ref[...] + g1 * y1_s[...] + g2 * y2_s[...]


def _combine(pos1, pos2, h, route, ys, tm, frames=None):
    R = h.shape[0]
    if frames is None:
        grid = (R // tm, 1)
        row0 = lambda i, j: i * tm
        in_rows = lambda n: pl.BlockSpec((tm, n), lambda i, j, p1, p2: (i, 0))
        out_rows = R
        out_spec = pl.BlockSpec((tm, D_MODEL), lambda i, j, p1, p2: (i, 0))
    else:
        B, S, lp = frames
        grid = (B, S // tm)
        row0 = lambda b, w: b * lp + N_META + w * tm
        in_rows = lambda n: pl.BlockSpec((pl.Element(tm), pl.Element(n)),
                                         lambda b, w, p1, p2: (pl.multiple_of(row0(b, w), 8), 0))
        out_rows = B * S
        out_spec = pl.BlockSpec((tm, D_MODEL), lambda b, w, p1, p2: (b * (S // tm) + w, 0))
    return pl.pallas_call(
        functools.partial(_combine_kernel, row0=row0),
        grid_spec=pltpu.PrefetchScalarGridSpec(
            num_scalar_prefetch=2,
            grid=grid,
            in_specs=[in_rows(D_MODEL), in_rows(route.shape[1]), pl.BlockSpec(memory_space=pl.ANY)],
            out_specs=out_spec,
            scratch_shapes=[pltpu.VMEM((tm, D_MODEL), F32), pltpu.VMEM((tm, D_MODEL), F32),
                            pltpu.SemaphoreType.DMA],
        ),
        out_shape=jax.ShapeDtypeStruct((out_rows, D_MODEL), F32),
        compiler_params=_cparams("arbitrary", "arbitrary"),
        name="moe_combine",
    )(pos1, pos2, h, route, ys)


def _moe(u, h, route, counts, wg, wu, wd, tm_rows, tm_expert, tf, frames=None):
    R = h.shape[0]
    ne = wg.shape[0]
    cnt = counts[0, :ne].astype(jnp.int32)
    padded = -(-cnt // tm_expert) * tm_expert
    ends = jnp.cumsum(padded)
    offs = ends - padded
    col = lambda ln: route[:, ln].astype(jnp.int32)
    pos1 = jnp.take(offs, col(ROUTE_E1)) + col(ROUTE_R1)
    pos2 = jnp.take(offs, col(ROUTE_E2)) + col(ROUTE_R2)
    n_slots = (2 * R // tm_expert + ne) * tm_expert
    n_tiles = (ends[-1] // tm_expert).reshape(1)
    tile_start = jnp.arange(n_slots // tm_expert, dtype=jnp.int32) * tm_expert
    tile_expert = jnp.sum(tile_start[:, None] >= ends[None, :], axis=1).astype(jnp.int32)
    last_expert = jnp.sum((ends[-1] - 1) >= ends).astype(jnp.int32)
    tile_expert = jnp.minimum(tile_expert, last_expert)

    xs = _dispatch(pos1, pos2, u, n_slots, tm_rows)
    ys = _expert_ffn(tile_expert, n_tiles, xs, wg, wu, wd, tm_expert, tf)
    if frames is None:
        return _combine(pos1, pos2, h, route, ys, tm_rows)
    return _combine(pos1, pos2, h, route, ys, _tile(frames[1], tm_rows), frames)


def _pad_heads(w, real):
    rows = w.shape[0]
    w = w.reshape(rows, N_HEADS, real)
    return jnp.pad(w, ((0, 0), (0, 0), (0, HEAD_PAD - real))).reshape(rows, N_HEADS * HEAD_PAD)


def _pad_cols(w, n):
    return jnp.pad(w, ((0, 0), (0, n - w.shape[1])))


def _row(v, n=None):
    v = v.reshape(1, -1).astype(F32)
    return v if n is None else _pad_cols(v, n)


def _block_diag(w):
    nb, c, d = w.shape
    eye = jnp.eye(nb, dtype=w.dtype)
    return (eye[:, None, :, None] * w[:, :, None, :]).reshape(nb * c, nb * d)


def _rope_tables(lp):
    half = MLA_ROPE // 2
    inv = ROPE_THETA ** (-jnp.arange(half, dtype=F32) / half)
    ang = jnp.arange(lp, dtype=jnp.int32).astype(F32)[:, None] * inv[None, :]
    cos, sin = jnp.cos(ang), jnp.sin(ang)
    one = jnp.ones((lp, MLA_NOPE), F32)
    zero = jnp.zeros((lp, MLA_NOPE), F32)
    zh = jnp.zeros((lp, half), F32)
    tail1 = jnp.ones((lp, HEAD_PAD - MLA_QK), F32)
    tail0 = jnp.zeros((lp, HEAD_PAD - MLA_QK), F32)
    cos_t = jnp.concatenate([one, cos, cos, tail1], axis=1)
    s_lo = jnp.concatenate([zero, -sin, zh, tail0], axis=1)
    s_hi = jnp.concatenate([zero, zh, sin, tail0], axis=1)
    return cos_t, s_lo, s_hi


def kernel(x, meta, norm1_g, norm2_g, w_in, w_out, out_norm_g, lru_conv_w, lru_conv_b, lru_wa, lru_ba, lru_wx, lru_bx, lru_lambda, hg_lb_logits, mla_gq, mla_w_uq, mla_gkv, mla_w_ukv, mla_gqn, mla_gkn, fox_gqn, fox_gkn, fox_bf, ffn_w_gate, ffn_w_up, ffn_w_down, moe_w_router, moe_w_gate, moe_w_up, moe_w_down):
    B, S, _ = x.shape
    depth = w_in.shape[0]
    L = N_META + S
    lp = -(-L // SEQ_ALIGN) * SEQ_ALIGN
    R = B * lp
    W = GROUP_WIDTH
    P = HEAD_PAD

    tt = _tile(lp, 640)
    tq = tk = tt

    h = jnp.concatenate([jnp.broadcast_to(meta[None].astype(x.dtype), (B, N_META, D_MODEL)), x,
                         jnp.zeros((B, lp - L, D_MODEL), x.dtype)], axis=1).reshape(R, D_MODEL)
    cos_t, s_lo, s_hi = _rope_tables(lp)
    lb_cum = jnp.cumsum(jax.nn.softmax(hg_lb_logits.astype(F32), axis=0), axis=0)

    o = 0
    offs = []
    for n in (W, W, W, W, W, W, MLA_Q_RANK, MLA_KV_RANK, MLA_ROPE, W, W, W, N_HEADS):
        offs.append(o)
        o += n
    (o_xa, _, o_hq, _, _, _, o_cq, o_ckv, o_kr, o_fq, o_fk, o_fv, o_ff) = offs

    for l in range(depth):
        w = w_in[l]
        wl = w[:, o_xa:o_xa + 2 * W].astype(BF16)
        wh = w[:, o_hq:o_hq + 4 * W].astype(BF16)
        wm = jnp.concatenate([
            w[:, o_ckv:o_ckv + MLA_KV_RANK],
            jnp.zeros((D_MODEL, MLA_NOPE), F32), w[:, o_kr:o_kr + MLA_ROPE],
            jnp.zeros((D_MODEL, P - MLA_QK), F32),
            _pad_cols(w[:, o_cq:o_cq + MLA_Q_RANK], 2 * P)], axis=1).astype(BF16)
        wf = jnp.concatenate([w[:, o_fq:o_fq + 3 * W], _pad_cols(w[:, o_ff:o_ff + N_HEADS], P)],
                             axis=1).astype(BF16)
        zl, zh, zm, zf = _inproj(h, _row(norm1_g[l]), wl, wh, wm, wf, tt)

        gn = out_norm_g[l].astype(F32)
        ya = _rglru(zl, lru_conv_w[l].astype(F32), _row(lru_conv_b[l]),
                    _block_diag(lru_wa[l]).astype(BF16), _row(lru_ba[l]),
                    _block_diag(lru_wx[l]).astype(BF16), _row(lru_bx[l]),
                    _row(lru_lambda[l]), _row(gn[0:W]), B, tt)
        yb = _hgrn2(zh, _row(lb_cum[l] - lb_cum[0]), _row(gn[W:2 * W]), B, tt)

        wuq = jnp.pad(_pad_heads(mla_w_uq[l], MLA_QK), ((0, 2 * P - MLA_Q_RANK), (0, 0))).astype(BF16)
        wukv = mla_w_ukv[l].reshape(MLA_KV_RANK, N_HEADS, MLA_NOPE + MLA_V)
        wuk = _pad_heads(wukv[:, :, :MLA_NOPE].reshape(MLA_KV_RANK, -1), MLA_NOPE).astype(BF16)
        wuv = _pad_heads(wukv[:, :, MLA_NOPE:].reshape(MLA_KV_RANK, -1), MLA_V).astype(BF16)
        q, k, v, stats = _mla_prep(zm, _row(mla_gq[l], 2 * P), wuq, _row(mla_gkv[l]), wuk, wuv,
                                   _row(mla_gqn[l], P), _row(mla_gkn[l], P), cos_t, s_lo, s_hi, tt)
        visit_all = jnp.zeros((B * (lp // tq),), jnp.int32)
        yc = _attention(visit_all, stats, q, k, v, _row(gn[2 * W:3 * W]), B, tq, tk, True)

        q, k, v, stats = _fox_prep(zf, _row(fox_bf[l], P), _row(jnp.tile(fox_gqn[l], N_HEADS)),
                                   _row(jnp.tile(fox_gkn[l], N_HEADS)), B, tt)
        yd = _attention(_fox_first_chunk(stats, B, lp // tt), stats, q, k, v, _row(gn[3 * W:4 * W]),
                        B, tq, tk, False)

        wo = w_out[l].astype(BF16)
        if l % 2 == 0:
            hn, u2 = _outproj(h, ya, yb, yc, yd, wo, _row(norm2_g[l]), None, tt)
            j = l // 2
            h = _ffn(u2, hn, ffn_w_gate[j].astype(BF16), ffn_w_up[j].astype(BF16),
                     ffn_w_down[j].astype(BF16), tt, 1408)
        else:
            j = l // 2
            wr = _pad_cols(moe_w_router[j].astype(F32), P)
            wr_hi = wr.astype(BF16)
            wr = jnp.concatenate([wr_hi, (wr - wr_hi.astype(F32)).astype(BF16)], axis=1)
            hn, u2, route, counts = _outproj(h, ya, yb, yc, yd, wo, _row(norm2_g[l]), wr, tt)
            last = l == depth - 1
            h = _moe(u2, hn, route, counts, moe_w_gate[j].astype(BF16), moe_w_up[j].astype(BF16),
                     moe_w_down[j].astype(BF16), tt, 512, 1792, (B, S, lp) if last else None)
            if last:
                return h.reshape(B, S, D_MODEL)
    return h.reshape(B, lp, D_MODEL)[:, N_META:L]
```

```python
import functools
import math

import jax
import jax.numpy as jnp
from jax import lax
from jax.experimental import pallas as pl
from jax.experimental.pallas import tpu as pltpu

F32 = jnp.float32
BF16 = jnp.bfloat16

D_MODEL = 1024
N_META = 16
CHUNK = 64
SEQ_ALIGN = 128
EPS = 1e-6
GROUP_WIDTH = 256
N_HEADS = 4
HEAD_PAD = 128
LRU_C = 8.0
HG_CHUNK = 16
HG_BLOCK = 128
MLA_NOPE, MLA_ROPE, MLA_V = 64, 32, 64
MLA_QK = MLA_NOPE + MLA_ROPE
MLA_Q_RANK, MLA_KV_RANK = 192, 128
ROPE_THETA = 10000.0
FOX_HD = 64
N_EXPERTS = 8
LOG2E = 1.4426950408889634
ONES_LANE = 64
VMEM_LIMIT = 56 * 1024 * 1024

NT_DIMS = (((1,), (1,)), ((), ()))
TN_DIMS = (((0,), (0,)), ((), ()))


def _cparams(*sem):
    return pltpu.CompilerParams(dimension_semantics=sem, vmem_limit_bytes=VMEM_LIMIT)


def _tile(n, pref, align=SEQ_ALIGN):
    best = None
    for t in range(align, min(n, pref) + 1, align):
        if n % t == 0:
            best = t
    assert best is not None, (n, pref, align)
    return best


def _rms(x, width):
    return lax.rsqrt(jnp.sum(x * x, axis=-1, keepdims=True) * (1.0 / width) + EPS)


def _row_ssq(x):
    ones = jnp.ones((x.shape[1], HEAD_PAD), BF16)
    return jnp.dot((x * x).astype(BF16), ones, preferred_element_type=F32)


def _rms_tiles(x, width):
    return lax.rsqrt(_row_ssq(x) * (1.0 / width) + EPS)


def _sigmoid(x):
    return 0.5 * jnp.tanh(0.5 * x) + 0.5


def _log_sigmoid(x):
    return jnp.minimum(x, 0.0) - jnp.log(1.0 + jnp.exp(-jnp.abs(x)))


def _full(shape):
    return pl.BlockSpec(shape, lambda *_: (0,) * len(shape))


def _inproj_kernel(h_ref, g_ref, wl_ref, wh_ref, wm_ref, wf_ref, zl_ref, zh_ref, zm_ref, zf_ref):
    x = h_ref[...]
    u = (x * _rms(x, D_MODEL) * g_ref[...]).astype(BF16)
    zl_ref[...] = jnp.dot(u, wl_ref[...], preferred_element_type=F32)
    zh_ref[...] = jnp.dot(u, wh_ref[...], preferred_element_type=F32)
    zm_ref[...] = jnp.dot(u, wm_ref[...], preferred_element_type=F32)
    zf_ref[...] = jnp.dot(u, wf_ref[...], preferred_element_type=F32)


def _inproj(h, g, wl, wh, wm, wf, tm):
    R = h.shape[0]
    row = lambda n: pl.BlockSpec((tm, n), lambda i: (i, 0))
    return pl.pallas_call(
        _inproj_kernel,
        grid=(R // tm,),
        in_specs=[row(D_MODEL), _full((1, D_MODEL)), _full(wl.shape), _full(wh.shape),
                  _full(wm.shape), _full(wf.shape)],
        out_specs=[row(wl.shape[1]), row(wh.shape[1]), row(wm.shape[1]), row(wf.shape[1])],
        out_shape=[jax.ShapeDtypeStruct((R, w.shape[1]), F32) for w in (wl, wh, wm, wf)],
        compiler_params=_cparams("parallel"),
        name="inproj",
    )(h, g, wl, wh, wm, wf)


def _rglru_kernel(z_ref, cw_ref, cb_ref, wa_ref, ba_ref, wx_ref, bx_ref, lam_ref, gn_ref, y_ref,
                  xbuf, a_s, b_s, h_s, hst):
    W = GROUP_WIDTH
    tt = y_ref.shape[0]

    @pl.when(pl.program_id(1) == 0)
    def _():
        xbuf[0:8, :] = jnp.zeros((8, W), F32)
        hst[...] = jnp.zeros((1, W), F32)

    xa = z_ref[:, 0:W]
    ga = z_ref[:, W:2 * W]
    xbuf[8:8 + tt, :] = xa
    u = (cb_ref[...] + xbuf[5:5 + tt, :] * cw_ref[0:1, :] + xbuf[6:6 + tt, :] * cw_ref[1:2, :]
         + xbuf[7:7 + tt, :] * cw_ref[2:3, :] + xa * cw_ref[3:4, :])
    xbuf[0:8, :] = xbuf[tt:tt + 8, :]

    ub = u.astype(BF16)
    r = _sigmoid(jnp.dot(ub, wa_ref[...], preferred_element_type=F32) + ba_ref[...])
    i = _sigmoid(jnp.dot(ub, wx_ref[...], preferred_element_type=F32) + bx_ref[...])
    lam = lam_ref[...]
    softplus_neg_lam = jnp.maximum(-lam, 0.0) + jnp.log(1.0 + jnp.exp(-jnp.abs(lam)))
    a = jnp.exp((-LRU_C) * r * softplus_neg_lam)
    b = jnp.sqrt(1.0 - a * a) * (i * u)

    row = lax.broadcasted_iota(jnp.int32, (tt, W), 0) & 7
    for s in (1, 2, 4):
        ok = row >= s
        b = jnp.where(ok, a * pltpu.roll(b, s, 0) + b, b)
        a = jnp.where(ok, a * pltpu.roll(a, s, 0), a)
    a_s[...] = a
    b_s[...] = b

    def group(gi, hprev):
        sl = pl.ds(pl.multiple_of(gi * 8, 8), 8)
        hg = b_s[sl, :] + a_s[sl, :] * hprev
        h_s[sl, :] = hg
        return hg[7:8, :]

    hst[...] = lax.fori_loop(0, tt // 8, group, hst[...], unroll=8)

    c0 = math.sqrt(2.0 / math.pi)
    gelu = 0.5 * ga * (1.0 + jnp.tanh(c0 * (ga + 0.044715 * (ga * ga * ga))))
    y = h_s[...] * gelu
    y_ref[...] = (y * _rms(y, W) * gn_ref[...]).astype(BF16)


def _rglru(zl, cw, cb, wa, ba, wx, bx, lam, gn, B, tt):
    R = zl.shape[0]
    nt = R // B // tt
    W = GROUP_WIDTH
    return pl.pallas_call(
        _rglru_kernel,
        grid=(B, nt),
        in_specs=[pl.BlockSpec((tt, 2 * W), lambda b, t: (b * nt + t, 0)),
                  _full((4, W)), _full((1, W)), _full((W, W)), _full((1, W)), _full((W, W)),
                  _full((1, W)), _full((1, W)), _full((1, W))],
        out_specs=pl.BlockSpec((tt, W), lambda b, t: (b * nt + t, 0)),
        out_shape=jax.ShapeDtypeStruct((R, W), BF16),
        scratch_shapes=[pltpu.VMEM((tt + 8, W), F32), pltpu.VMEM((tt, W), F32),
                        pltpu.VMEM((tt, W), F32), pltpu.VMEM((tt, W), F32), pltpu.VMEM((1, W), F32)],
        compiler_params=_cparams("parallel", "arbitrary"),
        name="rglru",
    )(zl, cw, cb, wa, ba, wx, bx, lam, gn)


def _hgrn2_kernel(z_ref, lb_ref, gn_ref, y_ref, qd_s, ke_s, v_s, dec_s, o_s, st_s):
    W = GROUP_WIDTH
    nb, tt = y_ref.shape[0], y_ref.shape[1]
    C = HG_CHUNK

    @pl.when(pl.program_id(0) == 0)
    def _():
        st_s[...] = jnp.zeros(st_s.shape, F32)

    lb = lb_ref[...]
    la = jnp.log(lb)
    l1 = jnp.log(1.0 - lb)
    rowc = lax.broadcasted_iota(jnp.int32, (tt, W), 0) & (C - 1)
    lane_head = lax.broadcasted_iota(jnp.int32, (1, W), 1) // (W // N_HEADS)
    rr = lax.broadcasted_iota(jnp.int32, (HG_BLOCK, HG_BLOCK), 0)
    cc = lax.broadcasted_iota(jnp.int32, (HG_BLOCK, HG_BLOCK), 1)
    amask = (rr // C == cc // C) & (cc <= rr)

    for bi in range(nb):
        q = z_ref[bi, :, 0:W]
        fz = z_ref[bi, :, W:2 * W]
        v = z_ref[bi, :, 2 * W:3 * W]
        lq = l1 + _log_sigmoid(fz)
        logf = jnp.maximum(la, lq) + jnp.log(1.0 + jnp.exp(-jnp.abs(la - lq)))
        kin = 1.0 - jnp.exp(logf)

        b = logf
        s = 1
        while s < C:
            b = b + jnp.where(rowc >= s, pltpu.roll(b, s, 0), 0.0)
            s *= 2
        b3 = b.reshape(tt // C, C, W)
        tail = (jnp.broadcast_to(b3[:, C - 1:C, :], b3.shape) - b3).reshape(tt, W)
        qd = q * jnp.exp(b)
        kd = (kin * jnp.exp(-b)).astype(BF16)
        vb = v.astype(BF16)
        qd_s[bi] = qd.astype(BF16)
        ke_s[bi] = (kin * jnp.exp(tail)).astype(BF16)
        v_s[bi] = vb
        dec_s[bi] = jnp.exp(b + tail)

        for jb in range(tt // HG_BLOCK):
            sl = slice(jb * HG_BLOCK, (jb + 1) * HG_BLOCK)
            qb, kb, vv = qd[sl], kd[sl], vb[sl]
            acc = jnp.zeros((HG_BLOCK, W), F32)
            for hh in range(N_HEADS):
                hm = lane_head == hh
                qh = jnp.where(hm, qb, 0.0).astype(BF16)
                att = lax.dot_general(qh, kb, NT_DIMS, preferred_element_type=F32)
                att = jnp.where(amask, att, 0.0).astype(BF16)
                acc = jnp.where(hm, jnp.dot(att, vv, preferred_element_type=F32), acc)
            o_s[bi, sl, :] = acc

    def chunk(c, carry):
        sl = pl.ds(pl.multiple_of(c * C, C), C)
        for bi in range(nb):
            qc = qd_s[bi, sl, :]
            st = st_s[bi]
            qbd = jnp.concatenate([jnp.where(lane_head == hh, qc, jnp.zeros_like(qc))
                                   for hh in range(N_HEADS)], axis=0)
            res = lax.dot_general(qbd, st.astype(BF16), NT_DIMS, preferred_element_type=F32)
            oi = jnp.zeros((C, W), F32)
            for hh in range(N_HEADS):
                oi = jnp.where(lane_head == hh, res[hh * C:(hh + 1) * C, :], oi)
            o_s[bi, sl, :] = o_s[bi, sl, :] + oi
            upd = lax.dot_general(v_s[bi, sl, :], ke_s[bi, sl, :], TN_DIMS, preferred_element_type=F32)
            st_s[bi] = st * dec_s[bi, pl.ds(c * C, 1), :] + upd
        return carry

    lax.fori_loop(0, tt // C, chunk, 0, unroll=math.gcd(tt // C, 4))

    for bi in range(nb):
        g = z_ref[bi, :, 3 * W:4 * W]
        y = o_s[bi] * (g * _sigmoid(g))
        y_ref[bi] = (y * _rms(y, W) * gn_ref[...]).astype(BF16)


def _hgrn2(zh, lb, gn, B, tt):
    R = zh.shape[0]
    lp = R // B
    W = GROUP_WIDTH
    return pl.pallas_call(
        _hgrn2_kernel,
        grid=(lp // tt,),
        in_specs=[pl.BlockSpec((B, tt, 4 * W), lambda t: (0, t, 0)), _full((1, W)), _full((1, W))],
        out_specs=pl.BlockSpec((B, tt, W), lambda t: (0, t, 0)),
        out_shape=jax.ShapeDtypeStruct((B, lp, W), BF16),
        scratch_shapes=[pltpu.VMEM((B, tt, W), BF16), pltpu.VMEM((B, tt, W), BF16), pltpu.VMEM((B, tt, W), BF16),
                        pltpu.VMEM((B, tt, W), F32), pltpu.VMEM((B, tt, W), F32), pltpu.VMEM((B, W, W), F32)],
        compiler_params=_cparams("arbitrary"),
        name="hgrn2",
    )(zh.reshape(B, lp, 4 * W), lb, gn)


def _rope(x, cos, s_lo, s_hi):
    return x * cos + pltpu.roll(x, 16, 1) * s_hi + pltpu.roll(x, HEAD_PAD - 16, 1) * s_lo


STAT_ROWS = 8
STAT_C_FIRST, STAT_C_LAST, STAT_Q2, STAT_K2 = range(4)
SHIFT_LANE = 100


def _stat_block(rows):
    srow = lax.broadcasted_iota(jnp.int32, (STAT_ROWS, HEAD_PAD), 0)
    stat = jnp.zeros((STAT_ROWS, HEAD_PAD), F32)
    for rr, val in rows:
        stat = jnp.where(srow == rr, val, stat)
    return stat


def _mla_prep_kernel(z_ref, gq_ref, wuq_ref, gkv_ref, wuk_ref, wuv_ref, gqn_ref, gkn_ref,
                     cos_ref, slo_ref, shi_ref, q_ref, k_ref, v_ref, stat_ref):
    P = HEAD_PAD
    ckv = z_ref[:, 0:P]
    krb = z_ref[:, P:2 * P]
    cq = z_ref[:, 2 * P:4 * P]
    rq = _rms_tiles(cq, MLA_Q_RANK)
    qn = (cq * jnp.concatenate([rq, rq], axis=1) * gq_ref[...]).astype(BF16)
    kvn = (ckv * _rms_tiles(ckv, MLA_KV_RANK) * gkv_ref[...]).astype(BF16)
    q = jnp.dot(qn, wuq_ref[...], preferred_element_type=F32)
    kn = jnp.dot(kvn, wuk_ref[...], preferred_element_type=F32)
    vv = jnp.dot(kvn, wuv_ref[...], preferred_element_type=F32)
    cos, slo, shi = cos_ref[...], slo_ref[...], shi_ref[...]
    lane = lax.broadcasted_iota(jnp.int32, (1, P), 1)
    qscale = (MLA_QK ** -0.5) * LOG2E
    q2 = jnp.zeros((1, P), F32)
    k2 = jnp.zeros((1, P), F32)
    for hh in range(N_HEADS):
        sl = slice(hh * P, (hh + 1) * P)
        qh = q[:, sl]
        qh = qh * _rms_tiles(qh, MLA_QK) * gqn_ref[...]
        qh = _rope(qh, cos, slo, shi) * qscale
        q2 = jnp.where(lane == hh, jnp.max(_row_ssq(qh), axis=0, keepdims=True), q2)
        q_ref[:, sl] = qh.astype(BF16)
        kh = kn[:, sl] + krb
        kh = kh * _rms_tiles(kh, MLA_QK) * gkn_ref[...]
        kh = _rope(kh, cos, slo, shi)
        k2 = jnp.where(lane == hh, jnp.max(_row_ssq(kh), axis=0, keepdims=True), k2)
        k_ref[:, sl] = jnp.where(lane == SHIFT_LANE, 1.0, kh).astype(BF16)
        v_ref[:, sl] = jnp.where(lane == ONES_LANE, 1.0, vv[:, sl]).astype(BF16)
    stat_ref[...] = _stat_block(((STAT_Q2, q2), (STAT_K2, k2)))


def _mla_prep(zm, gq, wuq, gkv, wuk, wuv, gqn, gkn, cos, slo, shi, tm):
    R = zm.shape[0]
    P = HEAD_PAD
    nt = cos.shape[0] // tm
    row = lambda n: pl.BlockSpec((tm, n), lambda i: (i, 0))
    tab = pl.BlockSpec((tm, P), lambda i: (i % nt, 0))
    out = jax.ShapeDtypeStruct((R, N_HEADS * P), BF16)
    return pl.pallas_call(
        _mla_prep_kernel,
        grid=(R // tm,),
        in_specs=[row(4 * P), _full((1, 2 * P)), _full(wuq.shape), _full((1, P)), _full(wuk.shape),
                  _full(wuv.shape), _full((1, P)), _full((1, P)), tab, tab, tab],
        out_specs=[row(N_HEADS * P)] * 3 + [pl.BlockSpec((STAT_ROWS, P), lambda i: (i, 0))],
        out_shape=[out, out, out, jax.ShapeDtypeStruct((R // tm * STAT_ROWS, P), F32)],
        compiler_params=_cparams("parallel"),
        name="mla_prep",
    )(zm, gq, wuq, gkv, wuk, wuv, gqn, gkn, cos, slo, shi)


def _fox_prep_kernel(z_ref, bf_ref, gqn_ref, gkn_ref, hsum_ref, place_ref, q_ref, k_ref, v_ref, stat_ref, carry):
    P = HEAD_PAD
    W = GROUP_WIDTH
    tt = q_ref.shape[0]

    @pl.when(pl.program_id(1) == 0)
    def _():
        carry[...] = jnp.zeros((1, P), F32)

    c = _log_sigmoid(z_ref[:, 3 * W:] + bf_ref[...])
    row = lax.broadcasted_iota(jnp.int32, (tt, P), 0)
    s = 1
    while s < tt:
        c = c + jnp.where(row >= s, pltpu.roll(c, s, 0), 0.0)
        s *= 2
    c = c + carry[...]
    carry[...] = c[tt - 1:tt, :]
    c = c * LOG2E
    c1 = c.astype(BF16).astype(F32)
    c2 = (c - c1).astype(BF16).astype(F32)
    c3 = c - c1 - c2

    def norm_place(x, gain):
        ssq = jnp.dot((x * x).astype(BF16), hsum_ref[...], preferred_element_type=F32)
        xn = (x * lax.rsqrt(ssq * (1.0 / FOX_HD) + EPS) * gain).astype(BF16)
        return jnp.dot(xn, place_ref[...], preferred_element_type=F32)

    qscale = (FOX_HD ** -0.5) * LOG2E
    qp = norm_place(z_ref[:, 0:W], gqn_ref[...] * qscale)
    kp = norm_place(z_ref[:, W:2 * W], gkn_ref[...])
    vp = jnp.dot(z_ref[:, 2 * W:3 * W].astype(BF16), place_ref[...], preferred_element_type=F32)

    lane = lax.broadcasted_iota(jnp.int32, (1, P), 1)
    q2 = jnp.zeros((1, P), F32)
    k2 = jnp.zeros((1, P), F32)
    for hh in range(N_HEADS):
        hs = slice(hh * P, (hh + 1) * P)
        p1 = c1[:, hh:hh + 1]
        p2 = c2[:, hh:hh + 1]
        p3 = c3[:, hh:hh + 1]
        qh = qp[:, hs]
        q2 = jnp.where(lane == hh, jnp.max(_row_ssq(qh), axis=0, keepdims=True), q2)
        qh = jnp.where(lane == 64, p1, jnp.where(lane == 65, p2, jnp.where(lane == 66, p3, qh)))
        qh = jnp.where((lane >= 67) & (lane < 70), 1.0, qh)
        q_ref[:, hs] = qh.astype(BF16)
        kh = kp[:, hs]
        k2 = jnp.where(lane == hh, jnp.max(_row_ssq(kh), axis=0, keepdims=True), k2)
        kh = jnp.where(lane == 67, -p1, jnp.where(lane == 68, -p2, jnp.where(lane == 69, -p3, kh)))
        kh = jnp.where(((lane >= 64) & (lane < 67)) | (lane == SHIFT_LANE), 1.0, kh)
        k_ref[:, hs] = kh.astype(BF16)
        v_ref[:, hs] = jnp.where(lane == ONES_LANE, 1.0, vp[:, hs]).astype(BF16)

    stat_ref[...] = _stat_block(((STAT_C_FIRST, c[0:1, :]), (STAT_C_LAST, c[tt - 1:tt, :]),
                                 (STAT_Q2, q2), (STAT_K2, k2)))


def _fox_prep(zf, bf, gqn, gkn, B, tt):
    R = zf.shape[0]
    nt = R // B // tt
    P = HEAD_PAD
    W = GROUP_WIDTH
    blk = lambda n: pl.BlockSpec((tt, n), lambda b, t: (b * nt + t, 0))
    out = jax.ShapeDtypeStruct((R, N_HEADS * P), BF16)
    head_sum = jnp.kron(jnp.eye(N_HEADS, dtype=F32), jnp.ones((FOX_HD, FOX_HD), F32)).astype(BF16)
    place = _pad_heads(jnp.eye(W, dtype=F32), FOX_HD).astype(BF16)
    return pl.pallas_call(
        _fox_prep_kernel,
        grid=(B, nt),
        in_specs=[blk(zf.shape[1]), _full((1, P)), _full((1, W)), _full((1, W)), _full((W, W)),
                  _full((W, N_HEADS * P))],
        out_specs=[blk(N_HEADS * P)] * 3 + [pl.BlockSpec((STAT_ROWS, P), lambda b, t: (b * nt + t, 0))],
        out_shape=[out, out, out, jax.ShapeDtypeStruct((B * nt * STAT_ROWS, P), F32)],
        scratch_shapes=[pltpu.VMEM((1, P), F32)],
        compiler_params=_cparams("parallel", "arbitrary"),
        name="fox_prep",
    )(zf, bf, gqn, gkn, head_sum, place)


SKIP_LOG2_MARGIN = 40.0
NORM_SLACK = 1.02


def _fox_first_chunk(stats, B, nt):
    st = stats.reshape(B, nt, STAT_ROWS, HEAD_PAD)[..., :N_HEADS]
    c_first, c_last = st[:, :, STAT_C_FIRST], st[:, :, STAT_C_LAST]
    bound = jnp.sqrt(jnp.max(st[:, :, STAT_Q2], axis=1) * jnp.max(st[:, :, STAT_K2], axis=1)) * NORM_SLACK
    gap = 2.0 * bound[:, None, None, :] + c_first[:, :, None, :] - c_last[:, None, :, :]
    earlier = jnp.arange(nt)[None, :] < jnp.arange(nt)[:, None]
    skip = (gap < -SKIP_LOG2_MARGIN) & earlier[None, :, :, None]
    return jnp.min(jnp.sum(skip, axis=2), axis=-1).astype(jnp.int32).reshape(-1)


MAX_SCORE_SHIFT = 48.0


def _score_bound(stats, B, nt):
    st = stats.reshape(B, nt, STAT_ROWS, HEAD_PAD)[..., :N_HEADS]
    bound = jnp.sqrt(jnp.max(st[:, :, STAT_Q2], axis=1) * jnp.max(st[:, :, STAT_K2], axis=1)) * NORM_SLACK
    return bound.reshape(-1), jnp.all(bound <= MAX_SCORE_SHIFT)


def _attn_kernel(first_ref, shift_ref, q_ref, k_ref, v_ref, gn_ref, y_ref, m_s, acc_s, o_s, q_s, *,
                 chunk_causal, tk, fixed_shift):
    P = HEAD_PAD
    tq = q_ref.shape[0]
    lp = k_ref.shape[0]
    blk = pl.program_id(0) * pl.num_programs(1) + pl.program_id(1)
    q0 = pl.program_id(1) * tq

    qpos = q0 + lax.broadcasted_iota(jnp.int32, (tq, 1), 0)
    if chunk_causal:
        qlim = N_META + CHUNK * ((qpos + (CHUNK - N_META)) // CHUNK)
        reach = N_META
    else:
        qlim = qpos + 1
        reach = 0
    n_full = (q0 + reach) // tk
    n_diag = (jnp.minimum(q0 + tq, lp) + tk - 1) // tk

    if fixed_shift:
        lane = lax.broadcasted_iota(jnp.int32, (1, P), 1)
        for hh in range(N_HEADS):
            shift = jnp.full((1, P), -shift_ref[pl.program_id(0) * N_HEADS + hh], F32).astype(BF16)
            q_s[hh] = jnp.where(lane == SHIFT_LANE, shift, q_ref[:, hh * P:(hh + 1) * P])
    else:
        m_s[...] = jnp.full(m_s.shape, -jnp.inf, F32)
    acc_s[...] = jnp.zeros(acc_s.shape, F32)

    def visit(k0, width, masked):
        ks = pl.ds(pl.multiple_of(k0, SEQ_ALIGN), width)
        if masked:
            vis = (k0 + lax.broadcasted_iota(jnp.int32, (1, width), 1)) < qlim
        for hh in range(N_HEADS):
            hs = slice(hh * P, (hh + 1) * P)
            qh = q_s[hh] if fixed_shift else q_ref[:, hs]
            s = lax.dot_general(qh, k_ref[ks, hs], NT_DIMS, preferred_element_type=F32)
            if masked:
                s = jnp.where(vis, s, -jnp.inf)
            if fixed_shift:
                acc_s[hh] += jnp.dot(jnp.exp2(s).astype(BF16), v_ref[ks, hs], preferred_element_type=F32)
                continue
            tiles = [s[:, c * P:(c + 1) * P] for c in range(width // P)]
            mx = tiles[0]
            for t in tiles[1:]:
                mx = jnp.maximum(mx, t)
            m_old = m_s[hh]
            m_new = jnp.maximum(m_old, jnp.max(mx, axis=-1, keepdims=True))
            p = jnp.concatenate([jnp.exp2((t - m_new).astype(BF16)) for t in tiles], axis=1)
            acc_s[hh] = jnp.exp2(m_old - m_new) * acc_s[hh] + jnp.dot(
                p, v_ref[ks, hs], preferred_element_type=F32)
            m_s[hh] = m_new

    def full_body(j, carry):
        visit(j * tk, tk, False)
        return carry

    def masked_body(j, carry):
        visit(j * tk, tk, True)
        return carry

    lax.fori_loop(first_ref[blk], n_full, full_body, 0)
    lax.fori_loop(n_full, n_diag, masked_body, 0)
    if chunk_causal:
        @pl.when(q0 + tq < lp)
        def _():
            visit(q0 + tq, SEQ_ALIGN, True)

    hd = GROUP_WIDTH // N_HEADS
    for hh in range(N_HEADS):
        acc = acc_s[hh]
        o_s[:, hh * hd:(hh + 1) * hd] = acc[:, 0:hd] / acc[:, ONES_LANE:ONES_LANE + 1]
    y = o_s[...]
    y_ref[...] = (y * _rms(y, GROUP_WIDTH) * gn_ref[...]).astype(BF16)


def _attention(first_chunk, stats, q, k, v, gn, B, tq, tk, chunk_causal):
    R = q.shape[0]
    lp = R // B
    nq = lp // tq
    P = HEAD_PAD
    W = GROUP_WIDTH
    shift, shift_is_safe = _score_bound(stats, B, nq)
    kv_spec = pl.BlockSpec((lp, N_HEADS * P), lambda b, i, fc, sh: (b, 0))

    def run(fixed_shift):
        return pl.pallas_call(
            functools.partial(_attn_kernel, chunk_causal=chunk_causal, tk=tk, fixed_shift=fixed_shift),
            grid_spec=pltpu.PrefetchScalarGridSpec(
                num_scalar_prefetch=2,
                grid=(B, nq),
                in_specs=[pl.BlockSpec((tq, N_HEADS * P), lambda b, i, fc, sh: (b * nq + i, 0)),
                          kv_spec, kv_spec, pl.BlockSpec((1, W), lambda b, i, fc, sh: (0, 0))],
                out_specs=pl.BlockSpec((tq, W), lambda b, i, fc, sh: (b * nq + i, 0)),
                scratch_shapes=[pltpu.VMEM((N_HEADS, tq, P), F32), pltpu.VMEM((N_HEADS, tq, P), F32),
                                pltpu.VMEM((tq, W), F32), pltpu.VMEM((N_HEADS, tq, P), BF16)],
            ),
            out_shape=jax.ShapeDtypeStruct((R, W), BF16),
            compiler_params=_cparams("parallel", "arbitrary"),
            name=("mla_attn" if chunk_causal else "fox_attn") + ("_shift" if fixed_shift else ""),
        )(first_chunk, shift, q, k, v, gn)

    return lax.cond(shift_is_safe, lambda: run(True), lambda: run(False))


ROUTE_E1, ROUTE_E2, ROUTE_R1, ROUTE_R2, ROUTE_G1, ROUTE_G2 = range(6)


def _outproj_kernel(h_ref, ya_ref, yb_ref, yc_ref, yd_ref, w_ref, g_ref, *rest, with_router):
    if with_router:
        wr_ref, hn_ref, u_ref, route_ref, cnt_ref = rest
    else:
        hn_ref, u_ref = rest
    y = jnp.concatenate([ya_ref[...], yb_ref[...], yc_ref[...], yd_ref[...]], axis=1)
    acc = h_ref[...] + jnp.dot(y, w_ref[...], preferred_element_type=F32)
    hn_ref[...] = acc
    u = acc * _rms(acc, D_MODEL) * g_ref[...]
    if not with_router:
        u_ref[...] = u.astype(BF16)
        return
    u_ref[...] = u
    tm = u.shape[0]

    @pl.when(pl.program_id(0) == 0)
    def _():
        cnt_ref[...] = jnp.zeros(cnt_ref.shape, F32)

    u_hi = u.astype(BF16)
    u_lo = (u - u_hi.astype(F32)).astype(BF16)
    hi_both = jnp.dot(u_hi, wr_ref[...], preferred_element_type=F32)
    logits = (hi_both[:, :HEAD_PAD] + hi_both[:, HEAD_PAD:]
              + jnp.dot(u_lo, wr_ref[:, :HEAD_PAD], preferred_element_type=F32))
    lane = lax.broadcasted_iota(jnp.int32, logits.shape, 1).astype(F32)
    lg = jnp.where(lane < N_EXPERTS, logits, -jnp.inf)
    m1 = jnp.max(lg, axis=-1, keepdims=True)
    i1 = jnp.min(jnp.where(lg == m1, lane, 1e9), axis=-1, keepdims=True)
    lg2 = jnp.where(lane == i1, -jnp.inf, lg)
    m2 = jnp.max(lg2, axis=-1, keepdims=True)
    i2 = jnp.min(jnp.where(lg2 == m2, lane, 1e9), axis=-1, keepdims=True)
    e = jnp.exp(m2 - m1)
    g1 = 1.0 / (1.0 + e)
    picks = jnp.where(lane == i1, 1.0, 0.0) + jnp.where(lane == i2, 1.0, 0.0)
    earlier = (lax.broadcasted_iota(jnp.int32, (tm, tm), 0) > lax.broadcasted_iota(jnp.int32, (tm, tm), 1))
    base = cnt_ref[...] + jnp.dot(earlier.astype(BF16), picks.astype(BF16), preferred_element_type=F32)
    r1 = jnp.sum(jnp.where(lane == i1, base, 0.0), axis=-1, keepdims=True)
    r2 = jnp.sum(jnp.where(lane == i2, base, 0.0), axis=-1, keepdims=True)
    cnt_ref[...] = cnt_ref[...] + jnp.sum(picks, axis=0, keepdims=True)
    rec = jnp.zeros(logits.shape, F32)
    for ln, val in ((ROUTE_E1, i1), (ROUTE_E2, i2), (ROUTE_R1, r1), (ROUTE_R2, r2),
                    (ROUTE_G1, g1), (ROUTE_G2, e * g1)):
        rec = jnp.where(lane == ln, val, rec)
    route_ref[...] = rec


def _outproj(h, ya, yb, yc, yd, w, g, wr, tm):
    R = h.shape[0]
    W = GROUP_WIDTH
    with_router = wr is not None
    row = lambda n: pl.BlockSpec((tm, n), lambda i: (i, 0))
    nt = yb.shape[1] // tm
    yb_spec = pl.BlockSpec((None, tm, W), lambda i: (i // nt, i % nt, 0))
    in_specs = [row(D_MODEL), row(W), yb_spec, row(W), row(W), _full(w.shape), _full((1, D_MODEL))]
    out_specs = [row(D_MODEL), row(D_MODEL)]
    out_shape = [jax.ShapeDtypeStruct((R, D_MODEL), F32),
                 jax.ShapeDtypeStruct((R, D_MODEL), F32 if with_router else BF16)]
    args = [h, ya, yb, yc, yd, w, g]
    if with_router:
        in_specs.append(_full(wr.shape))
        out_specs += [row(HEAD_PAD), _full((1, HEAD_PAD))]
        out_shape += [jax.ShapeDtypeStruct((R, HEAD_PAD), F32), jax.ShapeDtypeStruct((1, HEAD_PAD), F32)]
        args.append(wr)
    return pl.pallas_call(
        functools.partial(_outproj_kernel, with_router=with_router),
        grid=(R // tm,),
        in_specs=in_specs, out_specs=out_specs, out_shape=out_shape,
        compiler_params=_cparams("arbitrary" if with_router else "parallel"),
        name="outproj_router" if with_router else "outproj",
    )(*args)


def _ffn_kernel(u_ref, h_ref, wg_ref, wu_ref, wd_ref, o_ref):
    @pl.when(pl.program_id(1) == 0)
    def _():
        o_ref[...] = h_ref[...]

    u = u_ref[...]
    a = jnp.dot(u, wg_ref[...], preferred_element_type=F32)
    b = jnp.dot(u, wu_ref[...], preferred_element_type=F32)
    hid = (a * _sigmoid(a) * b).astype(BF16)
    o_ref[...] += jnp.dot(hid, wd_ref[...], preferred_element_type=F32)


def _ffn(u, h, wg, wu, wd, tm, tf):
    R = h.shape[0]
    dff = wg.shape[1]
    return pl.pallas_call(
        _ffn_kernel,
        grid=(R // tm, dff // tf),
        in_specs=[pl.BlockSpec((tm, D_MODEL), lambda i, f: (i, 0)),
                  pl.BlockSpec((tm, D_MODEL), lambda i, f: (i, 0)),
                  pl.BlockSpec((D_MODEL, tf), lambda i, f: (0, f)),
                  pl.BlockSpec((D_MODEL, tf), lambda i, f: (0, f)),
                  pl.BlockSpec((tf, D_MODEL), lambda i, f: (f, 0))],
        out_specs=pl.BlockSpec((tm, D_MODEL), lambda i, f: (i, 0)),
        out_shape=jax.ShapeDtypeStruct((R, D_MODEL), F32),
        compiler_params=_cparams("parallel", "arbitrary"),
        name="ffn",
    )(u, h, wg, wu, wd)


def _row_copy(src_ref, src_row, dst_ref, dst_row, sem):
    return pltpu.make_async_copy(src_ref.at[pl.ds(src_row, 1)], dst_ref.at[pl.ds(dst_row, 1)], sem)


def _dispatch_kernel(pos1_ref, pos2_ref, u_ref, xs_in_ref, xs_ref, sem):
    del xs_in_ref
    tm = u_ref.shape[0]
    t0 = pl.program_id(0) * tm

    def issue(r, carry):
        _row_copy(u_ref, r, xs_ref, pos1_ref[t0 + r], sem).start()
        _row_copy(u_ref, r, xs_ref, pos2_ref[t0 + r], sem).start()
        return carry

    lax.fori_loop(0, tm, issue, 0, unroll=8)

    def drain(r, carry):
        _row_copy(u_ref, r, xs_ref, 0, sem).wait()
        _row_copy(u_ref, r, xs_ref, 0, sem).wait()
        return carry

    lax.fori_loop(0, tm, drain, 0, unroll=8)


def _dispatch(pos1, pos2, u, n_slots, tm):
    R = u.shape[0]
    xs0 = jnp.zeros((n_slots, D_MODEL), F32)
    return pl.pallas_call(
        _dispatch_kernel,
        grid_spec=pltpu.PrefetchScalarGridSpec(
            num_scalar_prefetch=2,
            grid=(R // tm,),
            in_specs=[pl.BlockSpec((tm, D_MODEL), lambda i, p1, p2: (i, 0)),
                      pl.BlockSpec(memory_space=pl.ANY)],
            out_specs=pl.BlockSpec(memory_space=pl.ANY),
            scratch_shapes=[pltpu.SemaphoreType.DMA],
        ),
        out_shape=jax.ShapeDtypeStruct((n_slots, D_MODEL), F32),
        input_output_aliases={3: 0},
        compiler_params=_cparams("arbitrary"),
        name="moe_dispatch",
    )(pos1, pos2, u, xs0)


def _expert_ffn_kernel(te_ref, nt_ref, x_ref, wg_ref, wu_ref, wd_ref, y_ref, xb_s):
    f = pl.program_id(1)
    used = pl.program_id(0) < nt_ref[0]

    @pl.when(jnp.logical_not(used) & (f == 0))
    def _():
        y_ref[...] = jnp.zeros(y_ref.shape, F32)

    @pl.when(used)
    def _():
        @pl.when(f == 0)
        def _():
            xb_s[...] = x_ref[...].astype(BF16)

        x = xb_s[...]
        a = jnp.dot(x, wg_ref[...], preferred_element_type=F32)
        b = jnp.dot(x, wu_ref[...], preferred_element_type=F32)
        hid = (a * _sigmoid(a) * b).astype(BF16)
        out = jnp.dot(hid, wd_ref[...], preferred_element_type=F32)

        @pl.when(f == 0)
        def _():
            y_ref[...] = out

        @pl.when(f > 0)
        def _():
            y_ref[...] += out


def _expert_ffn(tile_expert, n_tiles, xs, wg, wu, wd, tm, tf):
    n_slots = xs.shape[0]
    dff = wg.shape[2]
    row_map = lambda i, f, te, nt: (jnp.minimum(i, nt[0] - 1), 0)
    out_map = lambda i, f, te, nt: (i, 0)
    return pl.pallas_call(
        _expert_ffn_kernel,
        grid_spec=pltpu.PrefetchScalarGridSpec(
            num_scalar_prefetch=2,
            grid=(n_slots // tm, dff // tf),
            in_specs=[pl.BlockSpec((tm, D_MODEL), row_map),
                      pl.BlockSpec((None, D_MODEL, tf), lambda i, f, te, nt: (te[i], 0, f)),
                      pl.BlockSpec((None, D_MODEL, tf), lambda i, f, te, nt: (te[i], 0, f)),
                      pl.BlockSpec((None, tf, D_MODEL), lambda i, f, te, nt: (te[i], f, 0))],
            out_specs=pl.BlockSpec((tm, D_MODEL), out_map),
            scratch_shapes=[pltpu.VMEM((tm, D_MODEL), BF16)],
        ),
        out_shape=jax.ShapeDtypeStruct((n_slots, D_MODEL), F32),
        compiler_params=_cparams("arbitrary", "arbitrary"),
        name="moe_expert_ffn",
    )(tile_expert, n_tiles, xs, wg, wu, wd)


def _combine_kernel(pos1_ref, pos2_ref, h_ref, route_ref, ys_ref, o_ref, y1_s, y2_s, sem, *, row0):
    tm = h_ref.shape[0]
    t0 = row0(pl.program_id(0), pl.program_id(1))

    def issue(r, carry):
        _row_copy(ys_ref, pos1_ref[t0 + r], y1_s, r, sem).start()
        _row_copy(ys_ref, pos2_ref[t0 + r], y2_s, r, sem).start()
        return carry

    lax.fori_loop(0, tm, issue, 0, unroll=8)

    def drain(r, carry):
        _row_copy(ys_ref, 0, y1_s, r, sem).wait()
        _row_copy(ys_ref, 0, y2_s, r, sem).wait()
        return carry

    lax.fori_loop(0, tm, drain, 0, unroll=8)
    rec = route_ref[...]
    g1 = rec[:, ROUTE_G1:ROUTE_G1 + 1]
    g2 = rec[:, ROUTE_G2:ROUTE_G2 + 1]
    o_ref[...] = h_ref[...] + g1 * y1_s[...] + g2 * y2_s[...]


def _combine(pos1, pos2, h, route, ys, tm, frames=None):
    R = h.shape[0]
    if frames is None:
        grid = (R // tm, 1)
        row0 = lambda i, j: i * tm
        in_rows = lambda n: pl.BlockSpec((tm, n), lambda i, j, p1, p2: (i, 0))
        out_rows = R
        out_spec = pl.BlockSpec((tm, D_MODEL), lambda i, j, p1, p2: (i, 0))
    else:
        B, S, lp = frames
        grid = (B, S // tm)
        row0 = lambda b, w: b * lp + N_META + w * tm
        in_rows = lambda n: pl.BlockSpec((pl.Element(tm), pl.Element(n)),
                                         lambda b, w, p1, p2: (pl.multiple_of(row0(b, w), 8), 0))
        out_rows = B * S
        out_spec = pl.BlockSpec((tm, D_MODEL), lambda b, w, p1, p2: (b * (S // tm) + w, 0))
    return pl.pallas_call(
        functools.partial(_combine_kernel, row0=row0),
        grid_spec=pltpu.PrefetchScalarGridSpec(
            num_scalar_prefetch=2,
            grid=grid,
            in_specs=[in_rows(D_MODEL), in_rows(route.shape[1]), pl.BlockSpec(memory_space=pl.ANY)],
            out_specs=out_spec,
            scratch_shapes=[pltpu.VMEM((tm, D_MODEL), F32), pltpu.VMEM((tm, D_MODEL), F32),
                            pltpu.SemaphoreType.DMA],
        ),
        out_shape=jax.ShapeDtypeStruct((out_rows, D_MODEL), F32),
        compiler_params=_cparams("arbitrary", "arbitrary"),
        name="moe_combine",
    )(pos1, pos2, h, route, ys)


def _moe(u, h, route, counts, wg, wu, wd, tm_rows, tm_expert, tf, frames=None):
    R = h.shape[0]
    ne = wg.shape[0]
    cnt = counts[0, :ne].astype(jnp.int32)
    padded = -(-cnt // tm_expert) * tm_expert
    ends = jnp.cumsum(padded)
    offs = ends - padded
    col = lambda ln: route[:, ln].astype(jnp.int32)
    pos1 = jnp.take(offs, col(ROUTE_E1)) + col(ROUTE_R1)
    pos2 = jnp.take(offs, col(ROUTE_E2)) + col(ROUTE_R2)
    n_slots = (2 * R // tm_expert + ne) * tm_expert
    n_tiles = (ends[-1] // tm_expert).reshape(1)
    tile_start = jnp.arange(n_slots // tm_expert, dtype=jnp.int32) * tm_expert
    tile_expert = jnp.sum(tile_start[:, None] >= ends[None, :], axis=1).astype(jnp.int32)
    last_expert = jnp.sum((ends[-1] - 1) >= ends).astype(jnp.int32)
    tile_expert = jnp.minimum(tile_expert, last_expert)

    xs = _dispatch(pos1, pos2, u, n_slots, tm_rows)
    ys = _expert_ffn(tile_expert, n_tiles, xs, wg, wu, wd, tm_expert, tf)
    if frames is None:
        return _combine(pos1, pos2, h, route, ys, tm_rows)
    return _combine(pos1, pos2, h, route, ys, _tile(frames[1], tm_rows), frames)


def _pad_heads(w, real):
    rows = w.shape[0]
    w = w.reshape(rows, N_HEADS, real)
    return jnp.pad(w, ((0, 0), (0, 0), (0, HEAD_PAD - real))).reshape(rows, N_HEADS * HEAD_PAD)


def _pad_cols(w, n):
    return jnp.pad(w, ((0, 0), (0, n - w.shape[1])))


def _row(v, n=None):
    v = v.reshape(1, -1).astype(F32)
    return v if n is None else _pad_cols(v, n)


def _block_diag(w):
    nb, c, d = w.shape
    eye = jnp.eye(nb, dtype=w.dtype)
    return (eye[:, None, :, None] * w[:, :, None, :]).reshape(nb * c, nb * d)


def _rope_tables(lp):
    half = MLA_ROPE // 2
    inv = ROPE_THETA ** (-jnp.arange(half, dtype=F32) / half)
    ang = jnp.arange(lp, dtype=jnp.int32).astype(F32)[:, None] * inv[None, :]
    cos, sin = jnp.cos(ang), jnp.sin(ang)
    one = jnp.ones((lp, MLA_NOPE), F32)
    zero = jnp.zeros((lp, MLA_NOPE), F32)
    zh = jnp.zeros((lp, half), F32)
    tail1 = jnp.ones((lp, HEAD_PAD - MLA_QK), F32)
    tail0 = jnp.zeros((lp, HEAD_PAD - MLA_QK), F32)
    cos_t = jnp.concatenate([one, cos, cos, tail1], axis=1)
    s_lo = jnp.concatenate([zero, -sin, zh, tail0], axis=1)
    s_hi = jnp.concatenate([zero, zh, sin, tail0], axis=1)
    return cos_t, s_lo, s_hi


def kernel(x, meta, norm1_g, norm2_g, w_in, w_out, out_norm_g, lru_conv_w, lru_conv_b, lru_wa, lru_ba, lru_wx, lru_bx, lru_lambda, hg_lb_logits, mla_gq, mla_w_uq, mla_gkv, mla_w_ukv, mla_gqn, mla_gkn, fox_gqn, fox_gkn, fox_bf, ffn_w_gate, ffn_w_up, ffn_w_down, moe_w_router, moe_w_gate, moe_w_up, moe_w_down):
    B, S, _ = x.shape
    depth = w_in.shape[0]
    L = N_META + S
    lp = -(-L // SEQ_ALIGN) * SEQ_ALIGN
    R = B * lp
    W = GROUP_WIDTH
    P = HEAD_PAD

    tt = _tile(lp, 640)
    tq = tk = tt

    h = jnp.concatenate([jnp.broadcast_to(meta[None].astype(x.dtype), (B, N_META, D_MODEL)), x,
                         jnp.zeros((B, lp - L, D_MODEL), x.dtype)], axis=1).reshape(R, D_MODEL)
    cos_t, s_lo, s_hi = _rope_tables(lp)
    lb_cum = jnp.cumsum(jax.nn.softmax(hg_lb_logits.astype(F32), axis=0), axis=0)

    o = 0
    offs = []
    for n in (W, W, W, W, W, W, MLA_Q_RANK, MLA_KV_RANK, MLA_ROPE, W, W, W, N_HEADS):
        offs.append(o)
        o += n
    (o_xa, _, o_hq, _, _, _, o_cq, o_ckv, o_kr, o_fq, o_fk, o_fv, o_ff) = offs

    for l in range(depth):
        w = w_in[l]
        wl = w[:, o_xa:o_xa + 2 * W].astype(BF16)
        wh = w[:, o_hq:o_hq + 4 * W].astype(BF16)
        wm = jnp.concatenate([
            w[:, o_ckv:o_ckv + MLA_KV_RANK],
            jnp.zeros((D_MODEL, MLA_NOPE), F32), w[:, o_kr:o_kr + MLA_ROPE],
            jnp.zeros((D_MODEL, P - MLA_QK), F32),
            _pad_cols(w[:, o_cq:o_cq + MLA_Q_RANK], 2 * P)], axis=1).astype(BF16)
        wf = jnp.concatenate([w[:, o_fq:o_fq + 3 * W], _pad_cols(w[:, o_ff:o_ff + N_HEADS], P)],
                             axis=1).astype(BF16)
        zl, zh, zm, zf = _inproj(h, _row(norm1_g[l]), wl, wh, wm, wf, tt)

        gn = out_norm_g[l].astype(F32)
        ya = _rglru(zl, lru_conv_w[l].astype(F32), _row(lru_conv_b[l]),
                    _block_diag(lru_wa[l]).astype(BF16), _row(lru_ba[l]),
                    _block_diag(lru_wx[l]).astype(BF16), _row(lru_bx[l]),
                    _row(lru_lambda[l]), _row(gn[0:W]), B, tt)
        yb = _hgrn2(zh, _row(lb_cum[l] - lb_cum[0]), _row(gn[W:2 * W]), B, tt)

        wuq = jnp.pad(_pad_heads(mla_w_uq[l], MLA_QK), ((0, 2 * P - MLA_Q_RANK), (0, 0))).astype(BF16)
        wukv = mla_w_ukv[l].reshape(MLA_KV_RANK, N_HEADS, MLA_NOPE + MLA_V)
        wuk = _pad_heads(wukv[:, :, :MLA_NOPE].reshape(MLA_KV_RANK, -1), MLA_NOPE).astype(BF16)
        wuv = _pad_heads(wukv[:, :, MLA_NOPE:].reshape(MLA_KV_RANK, -1), MLA_V).astype(BF16)
        q, k, v, stats = _mla_prep(zm, _row(mla_gq[l], 2 * P), wuq, _row(mla_gkv[l]), wuk, wuv,
                                   _row(mla_gqn[l], P), _row(mla_gkn[l], P), cos_t, s_lo, s_hi, tt)
        visit_all = jnp.zeros((B * (lp // tq),), jnp.int32)
        yc = _attention(visit_all, stats, q, k, v, _row(gn[2 * W:3 * W]), B, tq, tk, True)

        q, k, v, stats = _fox_prep(zf, _row(fox_bf[l], P), _row(jnp.tile(fox_gqn[l], N_HEADS)),
                                   _row(jnp.tile(fox_gkn[l], N_HEADS)), B, tt)
        yd = _attention(_fox_first_chunk(stats, B, lp // tt), stats, q, k, v, _row(gn[3 * W:4 * W]),
                        B, tq, tk, False)

        wo = w_out[l].astype(BF16)
        if l % 2 == 0:
            hn, u2 = _outproj(h, ya, yb, yc, yd, wo, _row(norm2_g[l]), None, tt)
            j = l // 2
            h = _ffn(u2, hn, ffn_w_gate[j].astype(BF16), ffn_w_up[j].astype(BF16),
                     ffn_w_down[j].astype(BF16), tt, 1408)
        else:
            j = l // 2
            wr = _pad_cols(moe_w_router[j].astype(F32), P)
            wr_hi = wr.astype(BF16)
            wr = jnp.concatenate([wr_hi, (wr - wr_hi.astype(F32)).astype(BF16)], axis=1)
            hn, u2, route, counts = _outproj(h, ya, yb, yc, yd, wo, _row(norm2_g[l]), wr, tt)
            last = l == depth - 1
            h = _moe(u2, hn, route, counts, moe_w_gate[j].astype(BF16), moe_w_up[j].astype(BF16),
                     moe_w_down[j].astype(BF16), tt, 512, 1792, (B, S, lp) if last else None)
            if last:
                return h.reshape(B, S, D_MODEL)
    return h.reshape(B, lp, D_MODEL)[:, N_META:L]
```

```python
import functools
import math

import jax
import jax.numpy as jnp
from jax import lax
from jax.experimental import pallas as pl
from jax.experimental.pallas import tpu as pltpu

F32 = jnp.float32
BF16 = jnp.bfloat16

D_MODEL = 1024
N_META = 16
CHUNK = 64
SEQ_ALIGN = 128
EPS = 1e-6
GROUP_WIDTH = 256
N_HEADS = 4
HEAD_PAD = 128
LRU_C = 8.0
HG_CHUNK = 16
HG_BLOCK = 128
MLA_NOPE, MLA_ROPE, MLA_V = 64, 32, 64
MLA_QK = MLA_NOPE + MLA_ROPE
MLA_Q_RANK, MLA_KV_RANK = 192, 128
ROPE_THETA = 10000.0
FOX_HD = 64
N_EXPERTS = 8
LOG2E = 1.4426950408889634
ONES_LANE = 64
VMEM_LIMIT = 56 * 1024 * 1024

NT_DIMS = (((1,), (1,)), ((), ()))
TN_DIMS = (((0,), (0,)), ((), ()))


def _cparams(*sem):
    return pltpu.CompilerParams(dimension_semantics=sem, vmem_limit_bytes=VMEM_LIMIT)


def _tile(n, pref, align=SEQ_ALIGN):
    best = None
    for t in range(align, min(n, pref) + 1, align):
        if n % t == 0:
            best = t
    assert best is not None, (n, pref, align)
    return best


def _rms(x, width):
    return lax.rsqrt(jnp.sum(x * x, axis=-1, keepdims=True) * (1.0 / width) + EPS)


def _row_ssq(x):
    ones = jnp.ones((x.shape[1], HEAD_PAD), BF16)
    return jnp.dot((x * x).astype(BF16), ones, preferred_element_type=F32)


def _rms_tiles(x, width):
    return lax.rsqrt(_row_ssq(x) * (1.0 / width) + EPS)


def _sigmoid(x):
    return 0.5 * jnp.tanh(0.5 * x) + 0.5


def _log_sigmoid(x):
    return jnp.minimum(x, 0.0) - jnp.log(1.0 + jnp.exp(-jnp.abs(x)))


def _full(shape):
    return pl.BlockSpec(shape, lambda *_: (0,) * len(shape))


def _inproj_kernel(h_ref, g_ref, wl_ref, wh_ref, wm_ref, wf_ref, zl_ref, zh_ref, zm_ref, zf_ref):
    x = h_ref[...]
    u = (x * _rms(x, D_MODEL) * g_ref[...]).astype(BF16)
    zl_ref[...] = jnp.dot(u, wl_ref[...], preferred_element_type=F32)
    zh_ref[...] = jnp.dot(u, wh_ref[...], preferred_element_type=F32)
    zm_ref[...] = jnp.dot(u, wm_ref[...], preferred_element_type=F32)
    zf_ref[...] = jnp.dot(u, wf_ref[...], preferred_element_type=F32)


def _inproj(h, g, wl, wh, wm, wf, tm):
    R = h.shape[0]
    row = lambda n: pl.BlockSpec((tm, n), lambda i: (i, 0))
    return pl.pallas_call(
        _inproj_kernel,
        grid=(R // tm,),
        in_specs=[row(D_MODEL), _full((1, D_MODEL)), _full(wl.shape), _full(wh.shape),
                  _full(wm.shape), _full(wf.shape)],
        out_specs=[row(wl.shape[1]), row(wh.shape[1]), row(wm.shape[1]), row(wf.shape[1])],
        out_shape=[jax.ShapeDtypeStruct((R, w.shape[1]), F32) for w in (wl, wh, wm, wf)],
        compiler_params=_cparams("parallel"),
        name="inproj",
    )(h, g, wl, wh, wm, wf)


def _rglru_kernel(z_ref, cw_ref, cb_ref, wa_ref, ba_ref, wx_ref, bx_ref, lam_ref, gn_ref, y_ref,
                  xbuf, a_s, b_s, h_s, hst):
    W = GROUP_WIDTH
    tt = y_ref.shape[0]

    @pl.when(pl.program_id(1) == 0)
    def _():
        xbuf[0:8, :] = jnp.zeros((8, W), F32)
        hst[...] = jnp.zeros((1, W), F32)

    xa = z_ref[:, 0:W]
    ga = z_ref[:, W:2 * W]
    xbuf[8:8 + tt, :] = xa
    u = (cb_ref[...] + xbuf[5:5 + tt, :] * cw_ref[0:1, :] + xbuf[6:6 + tt, :] * cw_ref[1:2, :]
         + xbuf[7:7 + tt, :] * cw_ref[2:3, :] + xa * cw_ref[3:4, :])
    xbuf[0:8, :] = xbuf[tt:tt + 8, :]

    ub = u.astype(BF16)
    r = _sigmoid(jnp.dot(ub, wa_ref[...], preferred_element_type=F32) + ba_ref[...])
    i = _sigmoid(jnp.dot(ub, wx_ref[...], preferred_element_type=F32) + bx_ref[...])
    lam = lam_ref[...]
    softplus_neg_lam = jnp.maximum(-lam, 0.0) + jnp.log(1.0 + jnp.exp(-jnp.abs(lam)))
    a = jnp.exp((-LRU_C) * r * softplus_neg_lam)
    b = jnp.sqrt(1.0 - a * a) * (i * u)

    row = lax.broadcasted_iota(jnp.int32, (tt, W), 0) & 7
    for s in (1, 2, 4):
        ok = row >= s
        b = jnp.where(ok, a * pltpu.roll(b, s, 0) + b, b)
        a = jnp.where(ok, a * pltpu.roll(a, s, 0), a)
    a_s[...] = a
    b_s[...] = b

    def group(gi, hprev):
        sl = pl.ds(pl.multiple_of(gi * 8, 8), 8)
        hg = b_s[sl, :] + a_s[sl, :] * hprev
        h_s[sl, :] = hg
        return hg[7:8, :]

    hst[...] = lax.fori_loop(0, tt // 8, group, hst[...], unroll=8)

    c0 = math.sqrt(2.0 / math.pi)
    gelu = 0.5 * ga * (1.0 + jnp.tanh(c0 * (ga + 0.044715 * (ga * ga * ga))))
    y = h_s[...] * gelu
    y_ref[...] = (y * _rms(y, W) * gn_ref[...]).astype(BF16)


def _rglru(zl, cw, cb, wa, ba, wx, bx, lam, gn, B, tt):
    R = zl.shape[0]
    nt = R // B // tt
    W = GROUP_WIDTH
    return pl.pallas_call(
        _rglru_kernel,
        grid=(B, nt),
        in_specs=[pl.BlockSpec((tt, 2 * W), lambda b, t: (b * nt + t, 0)),
                  _full((4, W)), _full((1, W)), _full((W, W)), _full((1, W)), _full((W, W)),
                  _full((1, W)), _full((1, W)), _full((1, W))],
        out_specs=pl.BlockSpec((tt, W), lambda b, t: (b * nt + t, 0)),
        out_shape=jax.ShapeDtypeStruct((R, W), BF16),
        scratch_shapes=[pltpu.VMEM((tt + 8, W), F32), pltpu.VMEM((tt, W), F32),
                        pltpu.VMEM((tt, W), F32), pltpu.VMEM((tt, W), F32), pltpu.VMEM((1, W), F32)],
        compiler_params=_cparams("parallel", "arbitrary"),
        name="rglru",
    )(zl, cw, cb, wa, ba, wx, bx, lam, gn)


def _hgrn2_kernel(z_ref, lb_ref, gn_ref, y_ref, qd_s, ke_s, v_s, dec_s, o_s, st_s):
    W = GROUP_WIDTH
    nb, tt = y_ref.shape[0], y_ref.shape[1]
    C = HG_CHUNK

    @pl.when(pl.program_id(0) == 0)
    def _():
        st_s[...] = jnp.zeros(st_s.shape, F32)

    lb = lb_ref[...]
    la = jnp.log(lb)
    l1 = jnp.log(1.0 - lb)
    rowc = lax.broadcasted_iota(jnp.int32, (tt, W), 0) & (C - 1)
    lane_head = lax.broadcasted_iota(jnp.int32, (1, W), 1) // (W // N_HEADS)
    rr = lax.broadcasted_iota(jnp.int32, (HG_BLOCK, HG_BLOCK), 0)
    cc = lax.broadcasted_iota(jnp.int32, (HG_BLOCK, HG_BLOCK), 1)
    amask = (rr // C == cc // C) & (cc <= rr)

    for bi in range(nb):
        q = z_ref[bi, :, 0:W]
        fz = z_ref[bi, :, W:2 * W]
        v = z_ref[bi, :, 2 * W:3 * W]
        lq = l1 + _log_sigmoid(fz)
        logf = jnp.maximum(la, lq) + jnp.log(1.0 + jnp.exp(-jnp.abs(la - lq)))
        kin = 1.0 - jnp.exp(logf)

        b = logf
        s = 1
        while s < C:
            b = b + jnp.where(rowc >= s, pltpu.roll(b, s, 0), 0.0)
            s *= 2
        b3 = b.reshape(tt // C, C, W)
        tail = (jnp.broadcast_to(b3[:, C - 1:C, :], b3.shape) - b3).reshape(tt, W)
        qd = q * jnp.exp(b)
        kd = (kin * jnp.exp(-b)).astype(BF16)
        vb = v.astype(BF16)
        qd_s[bi] = qd.astype(BF16)
        ke_s[bi] = (kin * jnp.exp(tail)).astype(BF16)
        v_s[bi] = vb
        dec_s[bi] = jnp.exp(b + tail)

        for jb in range(tt // HG_BLOCK):
            sl = slice(jb * HG_BLOCK, (jb + 1) * HG_BLOCK)
            qb, kb, vv = qd[sl], kd[sl], vb[sl]
            acc = jnp.zeros((HG_BLOCK, W), F32)
            for hh in range(N_HEADS):
                hm = lane_head == hh
                qh = jnp.where(hm, qb, 0.0).astype(BF16)
                att = lax.dot_general(qh, kb, NT_DIMS, preferred_element_type=F32)
                att = jnp.where(amask, att, 0.0).astype(BF16)
                acc = jnp.where(hm, jnp.dot(att, vv, preferred_element_type=F32), acc)
            o_s[bi, sl, :] = acc

    def chunk(c, carry):
        sl = pl.ds(pl.multiple_of(c * C, C), C)
        for bi in range(nb):
            qc = qd_s[bi, sl, :]
            st = st_s[bi]
            qbd = jnp.concatenate([jnp.where(lane_head == hh, qc, jnp.zeros_like(qc))
                                   for hh in range(N_HEADS)], axis=0)
            res = lax.dot_general(qbd, st.astype(BF16), NT_DIMS, preferred_element_type=F32)
            oi = jnp.zeros((C, W), F32)
            for hh in range(N_HEADS):
                oi = jnp.where(lane_head == hh, res[hh * C:(hh + 1) * C, :], oi)
            o_s[bi, sl, :] = o_s[bi, sl, :] + oi
            upd = lax.dot_general(v_s[bi, sl, :], ke_s[bi, sl, :], TN_DIMS, preferred_element_type=F32)
            st_s[bi] = st * dec_s[bi, pl.ds(c * C, 1), :] + upd
        return carry

    lax.fori_loop(0, tt // C, chunk, 0, unroll=math.gcd(tt // C, 4))

    for bi in range(nb):
        g = z_ref[bi, :, 3 * W:4 * W]
        y = o_s[bi] * (g * _sigmoid(g))
        y_ref[bi] = (y * _rms(y, W) * gn_ref[...]).astype(BF16)


def _hgrn2(zh, lb, gn, B, tt):
    R = zh.shape[0]
    lp = R // B
    W = GROUP_WIDTH
    return pl.pallas_call(
        _hgrn2_kernel,
        grid=(lp // tt,),
        in_specs=[pl.BlockSpec((B, tt, 4 * W), lambda t: (0, t, 0)), _full((1, W)), _full((1, W))],
        out_specs=pl.BlockSpec((B, tt, W), lambda t: (0, t, 0)),
        out_shape=jax.ShapeDtypeStruct((B, lp, W), BF16),
        scratch_shapes=[pltpu.VMEM((B, tt, W), BF16), pltpu.VMEM((B, tt, W), BF16), pltpu.VMEM((B, tt, W), BF16),
                        pltpu.VMEM((B, tt, W), F32), pltpu.VMEM((B, tt, W), F32), pltpu.VMEM((B, W, W), F32)],
        compiler_params=_cparams("arbitrary"),
        name="hgrn2",
    )(zh.reshape(B, lp, 4 * W), lb, gn)


def _rope(x, cos, s_lo, s_hi):
    return x * cos + pltpu.roll(x, 16, 1) * s_hi + pltpu.roll(x, HEAD_PAD - 16, 1) * s_lo


STAT_ROWS = 8
STAT_C_FIRST, STAT_C_LAST, STAT_Q2, STAT_K2 = range(4)
SHIFT_LANE = 100


def _stat_block(rows):
    srow = lax.broadcasted_iota(jnp.int32, (STAT_ROWS, HEAD_PAD), 0)
    stat = jnp.zeros((STAT_ROWS, HEAD_PAD), F32)
    for rr, val in rows:
        stat = jnp.where(srow == rr, val, stat)
    return stat


def _mla_prep_kernel(z_ref, gq_ref, wuq_ref, gkv_ref, wuk_ref, wuv_ref, gqn_ref, gkn_ref,
                     cos_ref, slo_ref, shi_ref, q_ref, k_ref, v_ref, stat_ref):
    P = HEAD_PAD
    ckv = z_ref[:, 0:P]
    krb = z_ref[:, P:2 * P]
    cq = z_ref[:, 2 * P:4 * P]
    rq = _rms_tiles(cq, MLA_Q_RANK)
    qn = (cq * jnp.concatenate([rq, rq], axis=1) * gq_ref[...]).astype(BF16)
    kvn = (ckv * _rms_tiles(ckv, MLA_KV_RANK) * gkv_ref[...]).astype(BF16)
    q = jnp.dot(qn, wuq_ref[...], preferred_element_type=F32)
    kn = jnp.dot(kvn, wuk_ref[...], preferred_element_type=F32)
    vv = jnp.dot(kvn, wuv_ref[...], preferred_element_type=F32)
    cos, slo, shi = cos_ref[...], slo_ref[...], shi_ref[...]
    lane = lax.broadcasted_iota(jnp.int32, (1, P), 1)
    qscale = (MLA_QK ** -0.5) * LOG2E
    q2 = jnp.zeros((1, P), F32)
    k2 = jnp.zeros((1, P), F32)
    for hh in range(N_HEADS):
        sl = slice(hh * P, (hh + 1) * P)
        qh = q[:, sl]
        qh = qh * _rms_tiles(qh, MLA_QK) * gqn_ref[...]
        qh = _rope(qh, cos, slo, shi) * qscale
        q2 = jnp.where(lane == hh, jnp.max(_row_ssq(qh), axis=0, keepdims=True), q2)
        q_ref[:, sl] = qh.astype(BF16)
        kh = kn[:, sl] + krb
        kh = kh * _rms_tiles(kh, MLA_QK) * gkn_ref[...]
        kh = _rope(kh, cos, slo, shi)
        k2 = jnp.where(lane == hh, jnp.max(_row_ssq(kh), axis=0, keepdims=True), k2)
        k_ref[:, sl] = jnp.where(lane == SHIFT_LANE, 1.0, kh).astype(BF16)
        v_ref[:, sl] = jnp.where(lane == ONES_LANE, 1.0, vv[:, sl]).astype(BF16)
    stat_ref[...] = _stat_block(((STAT_Q2, q2), (STAT_K2, k2)))


def _mla_prep(zm, gq, wuq, gkv, wuk, wuv, gqn, gkn, cos, slo, shi, tm):
    R = zm.shape[0]
    P = HEAD_PAD
    nt = cos.shape[0] // tm
    row = lambda n: pl.BlockSpec((tm, n), lambda i: (i, 0))
    tab = pl.BlockSpec((tm, P), lambda i: (i % nt, 0))
    out = jax.ShapeDtypeStruct((R, N_HEADS * P), BF16)
    return pl.pallas_call(
        _mla_prep_kernel,
        grid=(R // tm,),
        in_specs=[row(4 * P), _full((1, 2 * P)), _full(wuq.shape), _full((1, P)), _full(wuk.shape),
                  _full(wuv.shape), _full((1, P)), _full((1, P)), tab, tab, tab],
        out_specs=[row(N_HEADS * P)] * 3 + [pl.BlockSpec((STAT_ROWS, P), lambda i: (i, 0))],
        out_shape=[out, out, out, jax.ShapeDtypeStruct((R // tm * STAT_ROWS, P), F32)],
        compiler_params=_cparams("parallel"),
        name="mla_prep",
    )(zm, gq, wuq, gkv, wuk, wuv, gqn, gkn, cos, slo, shi)


def _fox_prep_kernel(z_ref, bf_ref, gqn_ref, gkn_ref, hsum_ref, place_ref, q_ref, k_ref, v_ref, stat_ref, carry):
    P = HEAD_PAD
    W = GROUP_WIDTH
    tt = q_ref.shape[0]

    @pl.when(pl.program_id(1) == 0)
    def _():
        carry[...] = jnp.zeros((1, P), F32)

    c = _log_sigmoid(z_ref[:, 3 * W:] + bf_ref[...])
    row = lax.broadcasted_iota(jnp.int32, (tt, P), 0)
    s = 1
    while s < tt:
        c = c + jnp.where(row >= s, pltpu.roll(c, s, 0), 0.0)
        s *= 2
    c = c + carry[...]
    carry[...] = c[tt - 1:tt, :]
    c = c * LOG2E
    c1 = c.astype(BF16).astype(F32)
    c2 = (c - c1).astype(BF16).astype(F32)
    c3 = c - c1 - c2

    def norm_place(x, gain):
        ssq = jnp.dot((x * x).astype(BF16), hsum_ref[...], preferred_element_type=F32)
        xn = (x * lax.rsqrt(ssq * (1.0 / FOX_HD) + EPS) * gain).astype(BF16)
        return jnp.dot(xn, place_ref[...], preferred_element_type=F32)

    qscale = (FOX_HD ** -0.5) * LOG2E
    qp = norm_place(z_ref[:, 0:W], gqn_ref[...] * qscale)
    kp = norm_place(z_ref[:, W:2 * W], gkn_ref[...])
    vp = jnp.dot(z_ref[:, 2 * W:3 * W].astype(BF16), place_ref[...], preferred_element_type=F32)

    lane = lax.broadcasted_iota(jnp.int32, (1, P), 1)
    q2 = jnp.zeros((1, P), F32)
    k2 = jnp.zeros((1, P), F32)
    for hh in range(N_HEADS):
        hs = slice(hh * P, (hh + 1) * P)
        p1 = c1[:, hh:hh + 1]
        p2 = c2[:, hh:hh + 1]
        p3 = c3[:, hh:hh + 1]
        qh = qp[:, hs]
        q2 = jnp.where(lane == hh, jnp.max(_row_ssq(qh), axis=0, keepdims=True), q2)
        qh = jnp.where(lane == 64, p1, jnp.where(lane == 65, p2, jnp.where(lane == 66, p3, qh)))
        qh = jnp.where((lane >= 67) & (lane < 70), 1.0, qh)
        q_ref[:, hs] = qh.astype(BF16)
        kh = kp[:, hs]
        k2 = jnp.where(lane == hh, jnp.max(_row_ssq(kh), axis=0, keepdims=True), k2)
        kh = jnp.where(lane == 67, -p1, jnp.where(lane == 68, -p2, jnp.where(lane == 69, -p3, kh)))
        kh = jnp.where(((lane >= 64) & (lane < 67)) | (lane == SHIFT_LANE), 1.0, kh)
        k_ref[:, hs] = kh.astype(BF16)
        v_ref[:, hs] = jnp.where(lane == ONES_LANE, 1.0, vp[:, hs]).astype(BF16)

    stat_ref[...] = _stat_block(((STAT_C_FIRST, c[0:1, :]), (STAT_C_LAST, c[tt - 1:tt, :]),
                                 (STAT_Q2, q2), (STAT_K2, k2)))


def _fox_prep(zf, bf, gqn, gkn, B, tt):
    R = zf.shape[0]
    nt = R // B // tt
    P = HEAD_PAD
    W = GROUP_WIDTH
    blk = lambda n: pl.BlockSpec((tt, n), lambda b, t: (b * nt + t, 0))
    out = jax.ShapeDtypeStruct((R, N_HEADS * P), BF16)
    head_sum = jnp.kron(jnp.eye(N_HEADS, dtype=F32), jnp.ones((FOX_HD, FOX_HD), F32)).astype(BF16)
    place = _pad_heads(jnp.eye(W, dtype=F32), FOX_HD).astype(BF16)
    return pl.pallas_call(
        _fox_prep_kernel,
        grid=(B, nt),
        in_specs=[blk(zf.shape[1]), _full((1, P)), _full((1, W)), _full((1, W)), _full((W, W)),
                  _full((W, N_HEADS * P))],
        out_specs=[blk(N_HEADS * P)] * 3 + [pl.BlockSpec((STAT_ROWS, P), lambda b, t: (b * nt + t, 0))],
        out_shape=[out, out, out, jax.ShapeDtypeStruct((B * nt * STAT_ROWS, P), F32)],
        scratch_shapes=[pltpu.VMEM((1, P), F32)],
        compiler_params=_cparams("parallel", "arbitrary"),
        name="fox_prep",
    )(zf, bf, gqn, gkn, head_sum, place)


SKIP_LOG2_MARGIN = 40.0
NORM_SLACK = 1.02


def _fox_first_chunk(stats, B, nt):
    st = stats.reshape(B, nt, STAT_ROWS, HEAD_PAD)[..., :N_HEADS]
    c_first, c_last = st[:, :, STAT_C_FIRST], st[:, :, STAT_C_LAST]
    bound = jnp.sqrt(jnp.max(st[:, :, STAT_Q2], axis=1) * jnp.max(st[:, :, STAT_K2], axis=1)) * NORM_SLACK
    gap = 2.0 * bound[:, None, None, :] + c_first[:, :, None, :] - c_last[:, None, :, :]
    earlier = jnp.arange(nt)[None, :] < jnp.arange(nt)[:, None]
    skip = (gap < -SKIP_LOG2_MARGIN) & earlier[None, :, :, None]
    return jnp.min(jnp.sum(skip, axis=2), axis=-1).astype(jnp.int32).reshape(-1)


MAX_SCORE_SHIFT = 48.0


def _score_bound(stats, B, nt):
    st = stats.reshape(B, nt, STAT_ROWS, HEAD_PAD)[..., :N_HEADS]
    bound = jnp.sqrt(jnp.max(st[:, :, STAT_Q2], axis=1) * jnp.max(st[:, :, STAT_K2], axis=1)) * NORM_SLACK
    return bound.reshape(-1), jnp.all(bound <= MAX_SCORE_SHIFT)


def _attn_kernel(first_ref, shift_ref, q_ref, k_ref, v_ref, gn_ref, y_ref, m_s, acc_s, o_s, q_s, *,
                 chunk_causal, tk, fixed_shift):
    P = HEAD_PAD
    tq = q_ref.shape[0]
    lp = k_ref.shape[0]
    blk = pl.program_id(0) * pl.num_programs(1) + pl.program_id(1)
    q0 = pl.program_id(1) * tq

    qpos = q0 + lax.broadcasted_iota(jnp.int32, (tq, 1), 0)
    if chunk_causal:
        qlim = N_META + CHUNK * ((qpos + (CHUNK - N_META)) // CHUNK)
        reach = N_META
    else:
        qlim = qpos + 1
        reach = 0
    n_full = (q0 + reach) // tk
    n_diag = (jnp.minimum(q0 + tq, lp) + tk - 1) // tk

    if fixed_shift:
        lane = lax.broadcasted_iota(jnp.int32, (1, P), 1)
        for hh in range(N_HEADS):
            shift = jnp.full((1, P), -shift_ref[pl.program_id(0) * N_HEADS + hh], F32).astype(BF16)
            q_s[hh] = jnp.where(lane == SHIFT_LANE, shift, q_ref[:, hh * P:(hh + 1) * P])
    else:
        m_s[...] = jnp.full(m_s.shape, -jnp.inf, F32)
    acc_s[...] = jnp.zeros(acc_s.shape, F32)

    def visit(k0, width, masked):
        ks = pl.ds(pl.multiple_of(k0, SEQ_ALIGN), width)
        if masked:
            vis = (k0 + lax.broadcasted_iota(jnp.int32, (1, width), 1)) < qlim
        for hh in range(N_HEADS):
            hs = slice(hh * P, (hh + 1) * P)
            qh = q_s[hh] if fixed_shift else q_ref[:, hs]
            s = lax.dot_general(qh, k_ref[ks, hs], NT_DIMS, preferred_element_type=F32)
            if masked:
                s = jnp.where(vis, s, -jnp.inf)
            if fixed_shift:
                acc_s[hh] += jnp.dot(jnp.exp2(s).astype(BF16), v_ref[ks, hs], preferred_element_type=F32)
                continue
            tiles = [s[:, c * P:(c + 1) * P] for c in range(width // P)]
            mx = tiles[0]
            for t in tiles[1:]:
                mx = jnp.maximum(mx, t)
            m_old = m_s[hh]
            m_new = jnp.maximum(m_old, jnp.max(mx, axis=-1, keepdims=True))
            p = jnp.concatenate([jnp.exp2((t - m_new).astype(BF16)) for t in tiles], axis=1)
            acc_s[hh] = jnp.exp2(m_old - m_new) * acc_s[hh] + jnp.dot(
                p, v_ref[ks, hs], preferred_element_type=F32)
            m_s[hh] = m_new

    def full_body(j, carry):
        visit(j * tk, tk, False)
        return carry

    def masked_body(j, carry):
        visit(j * tk, tk, True)
        return carry

    lax.fori_loop(first_ref[blk], n_full, full_body, 0)
    lax.fori_loop(n_full, n_diag, masked_body, 0)
    if chunk_causal:
        @pl.when(q0 + tq < lp)
        def _():
            visit(q0 + tq, SEQ_ALIGN, True)

    hd = GROUP_WIDTH // N_HEADS
    for hh in range(N_HEADS):
        acc = acc_s[hh]
        o_s[:, hh * hd:(hh + 1) * hd] = acc[:, 0:hd] / acc[:, ONES_LANE:ONES_LANE + 1]
    y = o_s[...]
    y_ref[...] = (y * _rms(y, GROUP_WIDTH) * gn_ref[...]).astype(BF16)


def _attention(first_chunk, stats, q, k, v, gn, B, tq, tk, chunk_causal):
    R = q.shape[0]
    lp = R // B
    nq = lp // tq
    P = HEAD_PAD
    W = GROUP_WIDTH
    shift, shift_is_safe = _score_bound(stats, B, nq)
    kv_spec = pl.BlockSpec((lp, N_HEADS * P), lambda b, i, fc, sh: (b, 0))

    def run(fixed_shift):
        return pl.pallas_call(
            functools.partial(_attn_kernel, chunk_causal=chunk_causal, tk=tk, fixed_shift=fixed_shift),
            grid_spec=pltpu.PrefetchScalarGridSpec(
                num_scalar_prefetch=2,
                grid=(B, nq),
                in_specs=[pl.BlockSpec((tq, N_HEADS * P), lambda b, i, fc, sh: (b * nq + i, 0)),
                          kv_spec, kv_spec, pl.BlockSpec((1, W), lambda b, i, fc, sh: (0, 0))],
                out_specs=pl.BlockSpec((tq, W), lambda b, i, fc, sh: (b * nq + i, 0)),
                scratch_shapes=[pltpu.VMEM((N_HEADS, tq, P), F32), pltpu.VMEM((N_HEADS, tq, P), F32),
                                pltpu.VMEM((tq, W), F32), pltpu.VMEM((N_HEADS, tq, P), BF16)],
            ),
            out_shape=jax.ShapeDtypeStruct((R, W), BF16),
            compiler_params=_cparams("parallel", "arbitrary"),
            name=("mla_attn" if chunk_causal else "fox_attn") + ("_shift" if fixed_shift else ""),
        )(first_chunk, shift, q, k, v, gn)

    return lax.cond(shift_is_safe, lambda: run(True), lambda: run(False))


ROUTE_E1, ROUTE_E2, ROUTE_R1, ROUTE_R2, ROUTE_G1, ROUTE_G2 = range(6)
ROUTE_ROWS = 8


def _outproj_kernel(h_ref, ya_ref, yb_ref, yc_ref, yd_ref, w_ref, g_ref, *rest, with_router):
    if with_router:
        wr_ref, hn_ref, u_ref, route_ref, route_t_ref, cnt_ref = rest
    else:
        hn_ref, u_ref = rest
    y = jnp.concatenate([ya_ref[...], yb_ref[...], yc_ref[...], yd_ref[...]], axis=1)
    acc = h_ref[...] + jnp.dot(y, w_ref[...], preferred_element_type=F32)
    hn_ref[...] = acc
    u = acc * _rms(acc, D_MODEL) * g_ref[...]
    if not with_router:
        u_ref[...] = u.astype(BF16)
        return
    u_ref[...] = u
    tm = u.shape[0]

    @pl.when(pl.program_id(0) == 0)
    def _():
        cnt_ref[...] = jnp.zeros(cnt_ref.shape, F32)

    u_hi = u.astype(BF16)
    u_lo = (u - u_hi.astype(F32)).astype(BF16)
    hi_both = jnp.dot(u_hi, wr_ref[...], preferred_element_type=F32)
    logits = (hi_both[:, :HEAD_PAD] + hi_both[:, HEAD_PAD:]
              + jnp.dot(u_lo, wr_ref[:, :HEAD_PAD], preferred_element_type=F32))
    lane = lax.broadcasted_iota(jnp.int32, logits.shape, 1).astype(F32)
    lg = jnp.where(lane < N_EXPERTS, logits, -jnp.inf)
    m1 = jnp.max(lg, axis=-1, keepdims=True)
    i1 = jnp.min(jnp.where(lg == m1, lane, 1e9), axis=-1, keepdims=True)
    lg2 = jnp.where(lane == i1, -jnp.inf, lg)
    m2 = jnp.max(lg2, axis=-1, keepdims=True)
    i2 = jnp.min(jnp.where(lg2 == m2, lane, 1e9), axis=-1, keepdims=True)
    e = jnp.exp(m2 - m1)
    g1 = 1.0 / (1.0 + e)
    picks = jnp.where(lane == i1, 1.0, 0.0) + jnp.where(lane == i2, 1.0, 0.0)
    earlier = (lax.broadcasted_iota(jnp.int32, (tm, tm), 0) > lax.broadcasted_iota(jnp.int32, (tm, tm), 1))
    base = cnt_ref[...] + jnp.dot(earlier.astype(BF16), picks.astype(BF16), preferred_element_type=F32)
    r1 = jnp.sum(jnp.where(lane == i1, base, 0.0), axis=-1, keepdims=True)
    r2 = jnp.sum(jnp.where(lane == i2, base, 0.0), axis=-1, keepdims=True)
    cnt_ref[...] = cnt_ref[...] + jnp.sum(picks, axis=0, keepdims=True)
    rec = jnp.zeros(logits.shape, F32)
    for ln, val in ((ROUTE_E1, i1), (ROUTE_E2, i2), (ROUTE_R1, r1), (ROUTE_R2, r2),
                    (ROUTE_G1, g1), (ROUTE_G2, e * g1)):
        rec = jnp.where(lane == ln, val, rec)
    route_ref[...] = rec
    route_t_ref[...] = rec.T[0:ROUTE_ROWS, :]


def _outproj(h, ya, yb, yc, yd, w, g, wr, tm):
    R = h.shape[0]
    W = GROUP_WIDTH
    with_router = wr is not None
    row = lambda n: pl.BlockSpec((tm, n), lambda i: (i, 0))
    nt = yb.shape[1] // tm
    yb_spec = pl.BlockSpec((None, tm, W), lambda i: (i // nt, i % nt, 0))
    in_specs = [row(D_MODEL), row(W), yb_spec, row(W), row(W), _full(w.shape), _full((1, D_MODEL))]
    out_specs = [row(D_MODEL), row(D_MODEL)]
    out_shape = [jax.ShapeDtypeStruct((R, D_MODEL), F32),
                 jax.ShapeDtypeStruct((R, D_MODEL), F32 if with_router else BF16)]
    args = [h, ya, yb, yc, yd, w, g]
    if with_router:
        in_specs.append(_full(wr.shape))
        out_specs += [row(HEAD_PAD), pl.BlockSpec((ROUTE_ROWS, tm), lambda i: (0, i)), _full((1, HEAD_PAD))]
        out_shape += [jax.ShapeDtypeStruct((R, HEAD_PAD), F32), jax.ShapeDtypeStruct((ROUTE_ROWS, R), F32),
                      jax.ShapeDtypeStruct((1, HEAD_PAD), F32)]
        args.append(wr)
    return pl.pallas_call(
        functools.partial(_outproj_kernel, with_router=with_router),
        grid=(R // tm,),
        in_specs=in_specs, out_specs=out_specs, out_shape=out_shape,
        compiler_params=_cparams("arbitrary" if with_router else "parallel"),
        name="outproj_router" if with_router else "outproj",
    )(*args)


def _ffn_kernel(u_ref, h_ref, wg_ref, wu_ref, wd_ref, o_ref):
    @pl.when(pl.program_id(1) == 0)
    def _():
        o_ref[...] = h_ref[...]

    u = u_ref[...]
    a = jnp.dot(u, wg_ref[...], preferred_element_type=F32)
    b = jnp.dot(u, wu_ref[...], preferred_element_type=F32)
    hid = (a * _sigmoid(a) * b).astype(BF16)
    o_ref[...] += jnp.dot(hid, wd_ref[...], preferred_element_type=F32)


def _ffn(u, h, wg, wu, wd, tm, tf):
    R = h.shape[0]
    dff = wg.shape[1]
    return pl.pallas_call(
        _ffn_kernel,
        grid=(R // tm, dff // tf),
        in_specs=[pl.BlockSpec((tm, D_MODEL), lambda i, f: (i, 0)),
                  pl.BlockSpec((tm, D_MODEL), lambda i, f: (i, 0)),
                  pl.BlockSpec((D_MODEL, tf), lambda i, f: (0, f)),
                  pl.BlockSpec((D_MODEL, tf), lambda i, f: (0, f)),
                  pl.BlockSpec((tf, D_MODEL), lambda i, f: (f, 0))],
        out_specs=pl.BlockSpec((tm, D_MODEL), lambda i, f: (i, 0)),
        out_shape=jax.ShapeDtypeStruct((R, D_MODEL), F32),
        compiler_params=_cparams("parallel", "arbitrary"),
        name="ffn",
    )(u, h, wg, wu, wd)


def _row_copy(src_ref, src_row, dst_ref, dst_row, sem):
    return pltpu.make_async_copy(src_ref.at[pl.ds(src_row, 1)], dst_ref.at[pl.ds(dst_row, 1)], sem)


def _dispatch_kernel(pos1_ref, pos2_ref, u_ref, xs_in_ref, xs_ref, sem):
    del xs_in_ref
    tm = u_ref.shape[0]
    t0 = pl.program_id(0) * tm

    def issue(r, carry):
        _row_copy(u_ref, r, xs_ref, pos1_ref[t0 + r], sem).start()
        _row_copy(u_ref, r, xs_ref, pos2_ref[t0 + r], sem).start()
        return carry

    lax.fori_loop(0, tm, issue, 0, unroll=8)

    def drain(r, carry):
        _row_copy(u_ref, r, xs_ref, 0, sem).wait()
        _row_copy(u_ref, r, xs_ref, 0, sem).wait()
        return carry

    lax.fori_loop(0, tm, drain, 0, unroll=8)


def _dispatch(pos1, pos2, u, n_slots, tm):
    R = u.shape[0]
    xs0 = jnp.zeros((n_slots, D_MODEL), F32)
    return pl.pallas_call(
        _dispatch_kernel,
        grid_spec=pltpu.PrefetchScalarGridSpec(
            num_scalar_prefetch=2,
            grid=(R // tm,),
            in_specs=[pl.BlockSpec((tm, D_MODEL), lambda i, p1, p2: (i, 0)),
                      pl.BlockSpec(memory_space=pl.ANY)],
            out_specs=pl.BlockSpec(memory_space=pl.ANY),
            scratch_shapes=[pltpu.SemaphoreType.DMA],
        ),
        out_shape=jax.ShapeDtypeStruct((n_slots, D_MODEL), F32),
        input_output_aliases={3: 0},
        compiler_params=_cparams("arbitrary"),
        name="moe_dispatch",
    )(pos1, pos2, u, xs0)


def _expert_ffn_kernel(te_ref, nt_ref, x_ref, wg_ref, wu_ref, wd_ref, y_ref, xb_s):
    f = pl.program_id(1)
    used = pl.program_id(0) < nt_ref[0]

    @pl.when(jnp.logical_not(used) & (f == 0))
    def _():
        y_ref[...] = jnp.zeros(y_ref.shape, F32)

    @pl.when(used)
    def _():
        @pl.when(f == 0)
        def _():
            xb_s[...] = x_ref[...].astype(BF16)

        x = xb_s[...]
        a = jnp.dot(x, wg_ref[...], preferred_element_type=F32)
        b = jnp.dot(x, wu_ref[...], preferred_element_type=F32)
        hid = (a * _sigmoid(a) * b).astype(BF16)
        out = jnp.dot(hid, wd_ref[...], preferred_element_type=F32)

        @pl.when(f == 0)
        def _():
            y_ref[...] = out

        @pl.when(f > 0)
        def _():
            y_ref[...] += out


def _expert_ffn(tile_expert, n_tiles, xs, wg, wu, wd, tm, tf):
    n_slots = xs.shape[0]
    dff = wg.shape[2]
    row_map = lambda i, f, te, nt: (jnp.minimum(i, nt[0] - 1), 0)
    out_map = lambda i, f, te, nt: (i, 0)
    return pl.pallas_call(
        _expert_ffn_kernel,
        grid_spec=pltpu.PrefetchScalarGridSpec(
            num_scalar_prefetch=2,
            grid=(n_slots // tm, dff // tf),
            in_specs=[pl.BlockSpec((tm, D_MODEL), row_map),
                      pl.BlockSpec((None, D_MODEL, tf), lambda i, f, te, nt: (te[i], 0, f)),
                      pl.BlockSpec((None, D_MODEL, tf), lambda i, f, te, nt: (te[i], 0, f)),
                      pl.BlockSpec((None, tf, D_MODEL), lambda i, f, te, nt: (te[i], f, 0))],
            out_specs=pl.BlockSpec((tm, D_MODEL), out_map),
            scratch_shapes=[pltpu.VMEM((tm, D_MODEL), BF16)],
        ),
        out_shape=jax.ShapeDtypeStruct((n_slots, D_MODEL), F32),
        compiler_params=_cparams("arbitrary", "arbitrary"),
        name="moe_expert_ffn",
    )(tile_expert, n_tiles, xs, wg, wu, wd)


def _combine_kernel(pos1_ref, pos2_ref, h_ref, route_ref, ys_ref, o_ref, y1_s, y2_s, sem, *, row0):
    tm = h_ref.shape[0]
    t0 = row0(pl.program_id(0), pl.program_id(1))

    def issue(r, carry):
        _row_copy(ys_ref, pos1_ref[t0 + r], y1_s, r, sem).start()
        _row_copy(ys_ref, pos2_ref[t0 + r], y2_s, r, sem).start()
        return carry

    lax.fori_loop(0, tm, issue, 0, unroll=8)

    def drain(r, carry):
        _row_copy(ys_ref, 0, y1_s, r, sem).wait()
        _row_copy(ys_ref, 0, y2_s, r, sem).wait()
        return carry

    lax.fori_loop(0, tm, drain, 0, unroll=8)
    rec = route_ref[...]
    g1 = rec[:, ROUTE_G1:ROUTE_G1 + 1]
    g2 = rec[:, ROUTE_G2:ROUTE_G2 + 1]
    o_ref[...] = h_ref[...] + g1 * y1_s[...] + g2 * y2_s[...]


def _combine(pos1, pos2, h, route, ys, tm, frames=None):
    R = h.shape[0]
    if frames is None:
        grid = (R // tm, 1)
        row0 = lambda i, j: i * tm
        in_rows = lambda n: pl.BlockSpec((tm, n), lambda i, j, p1, p2: (i, 0))
        out_rows = R
        out_spec = pl.BlockSpec((tm, D_MODEL), lambda i, j, p1, p2: (i, 0))
    else:
        B, S, lp = frames
        grid = (B, S // tm)
        row0 = lambda b, w: b * lp + N_META + w * tm
        in_rows = lambda n: pl.BlockSpec((pl.Element(tm), pl.Element(n)),
                                         lambda b, w, p1, p2: (pl.multiple_of(row0(b, w), 8), 0))
        out_rows = B * S
        out_spec = pl.BlockSpec((tm, D_MODEL), lambda b, w, p1, p2: (b * (S // tm) + w, 0))
    return pl.pallas_call(
        functools.partial(_combine_kernel, row0=row0),
        grid_spec=pltpu.PrefetchScalarGridSpec(
            num_scalar_prefetch=2,
            grid=grid,
            in_specs=[in_rows(D_MODEL), in_rows(route.shape[1]), pl.BlockSpec(memory_space=pl.ANY)],
            out_specs=out_spec,
            scratch_shapes=[pltpu.VMEM((tm, D_MODEL), F32), pltpu.VMEM((tm, D_MODEL), F32),
                            pltpu.SemaphoreType.DMA],
        ),
        out_shape=jax.ShapeDtypeStruct((out_rows, D_MODEL), F32),
        compiler_params=_cparams("arbitrary", "arbitrary"),
        name="moe_combine",
    )(pos1, pos2, h, route, ys)


def _moe(u, h, route, route_t, counts, wg, wu, wd, tm_rows, tm_expert, tf, frames=None):
    R = h.shape[0]
    ne = wg.shape[0]
    cnt = counts[0, :ne].astype(jnp.int32)
    padded = -(-cnt // tm_expert) * tm_expert
    ends = jnp.cumsum(padded)
    offs = ends - padded
    col = lambda ln: route_t[ln].astype(jnp.int32)
    pos1 = jnp.take(offs, col(ROUTE_E1)) + col(ROUTE_R1)
    pos2 = jnp.take(offs, col(ROUTE_E2)) + col(ROUTE_R2)
    n_slots = (2 * R // tm_expert + ne) * tm_expert
    n_tiles = (ends[-1] // tm_expert).reshape(1)
    tile_start = jnp.arange(n_slots // tm_expert, dtype=jnp.int32) * tm_expert
    tile_expert = jnp.sum(tile_start[:, None] >= ends[None, :], axis=1).astype(jnp.int32)
    last_expert = jnp.sum((ends[-1] - 1) >= ends).astype(jnp.int32)
    tile_expert = jnp.minimum(tile_expert, last_expert)

    xs = _dispatch(pos1, pos2, u, n_slots, tm_rows)
    ys = _expert_ffn(tile_expert, n_tiles, xs, wg, wu, wd, tm_expert, tf)
    if frames is None:
        return _combine(pos1, pos2, h, route, ys, tm_rows)
    return _combine(pos1, pos2, h, route, ys, _tile(frames[1], tm_rows), frames)


def _pad_heads(w, real):
    rows = w.shape[0]
    w = w.reshape(rows, N_HEADS, real)
    return jnp.pad(w, ((0, 0), (0, 0), (0, HEAD_PAD - real))).reshape(rows, N_HEADS * HEAD_PAD)


def _pad_cols(w, n):
    return jnp.pad(w, ((0, 0), (0, n - w.shape[1])))


def _row(v, n=None):
    v = v.reshape(1, -1).astype(F32)
    return v if n is None else _pad_cols(v, n)


def _block_diag(w):
    nb, c, d = w.shape
    eye = jnp.eye(nb, dtype=w.dtype)
    return (eye[:, None, :, None] * w[:, :, None, :]).reshape(nb * c, nb * d)


def _rope_tables(lp):
    half = MLA_ROPE // 2
    inv = ROPE_THETA ** (-jnp.arange(half, dtype=F32) / half)
    ang = jnp.arange(lp, dtype=jnp.int32).astype(F32)[:, None] * inv[None, :]
    cos, sin = jnp.cos(ang), jnp.sin(ang)
    one = jnp.ones((lp, MLA_NOPE), F32)
    zero = jnp.zeros((lp, MLA_NOPE), F32)
    zh = jnp.zeros((lp, half), F32)
    tail1 = jnp.ones((lp, HEAD_PAD - MLA_QK), F32)
    tail0 = jnp.zeros((lp, HEAD_PAD - MLA_QK), F32)
    cos_t = jnp.concatenate([one, cos, cos, tail1], axis=1)
    s_lo = jnp.concatenate([zero, -sin, zh, tail0], axis=1)
    s_hi = jnp.concatenate([zero, zh, sin, tail0], axis=1)
    return cos_t, s_lo, s_hi


def kernel(x, meta, norm1_g, norm2_g, w_in, w_out, out_norm_g, lru_conv_w, lru_conv_b, lru_wa, lru_ba, lru_wx, lru_bx, lru_lambda, hg_lb_logits, mla_gq, mla_w_uq, mla_gkv, mla_w_ukv, mla_gqn, mla_gkn, fox_gqn, fox_gkn, fox_bf, ffn_w_gate, ffn_w_up, ffn_w_down, moe_w_router, moe_w_gate, moe_w_up, moe_w_down):
    B, S, _ = x.shape
    depth = w_in.shape[0]
    L = N_META + S
    lp = -(-L // SEQ_ALIGN) * SEQ_ALIGN
    R = B * lp
    W = GROUP_WIDTH
    P = HEAD_PAD

    tt = _tile(lp, 640)
    tq = tk = tt

    h = jnp.concatenate([jnp.broadcast_to(meta[None].astype(x.dtype), (B, N_META, D_MODEL)), x,
                         jnp.zeros((B, lp - L, D_MODEL), x.dtype)], axis=1).reshape(R, D_MODEL)
    cos_t, s_lo, s_hi = _rope_tables(lp)
    lb_cum = jnp.cumsum(jax.nn.softmax(hg_lb_logits.astype(F32), axis=0), axis=0)

    o = 0
    offs = []
    for n in (W, W, W, W, W, W, MLA_Q_RANK, MLA_KV_RANK, MLA_ROPE, W, W, W, N_HEADS):
        offs.append(o)
        o += n
    (o_xa, _, o_hq, _, _, _, o_cq, o_ckv, o_kr, o_fq, o_fk, o_fv, o_ff) = offs

    for l in range(depth):
        wcols = lambda start, n: lax.slice(w_in, (l, 0, start), (l + 1, D_MODEL, start + n))[0].astype(BF16)
        wl = wcols(o_xa, 2 * W)
        wh = wcols(o_hq, 4 * W)
        wm = jnp.concatenate([
            wcols(o_ckv, MLA_KV_RANK),
            jnp.zeros((D_MODEL, MLA_NOPE), BF16), wcols(o_kr, MLA_ROPE),
            jnp.zeros((D_MODEL, P - MLA_QK), BF16),
            _pad_cols(wcols(o_cq, MLA_Q_RANK), 2 * P)], axis=1)
        wf = jnp.concatenate([wcols(o_fq, 3 * W), _pad_cols(wcols(o_ff, N_HEADS), P)], axis=1)
        zl, zh, zm, zf = _inproj(h, _row(norm1_g[l]), wl, wh, wm, wf, tt)

        gn = out_norm_g[l].astype(F32)
        ya = _rglru(zl, lru_conv_w[l].astype(F32), _row(lru_conv_b[l]),
                    _block_diag(lru_wa[l]).astype(BF16), _row(lru_ba[l]),
                    _block_diag(lru_wx[l]).astype(BF16), _row(lru_bx[l]),
                    _row(lru_lambda[l]), _row(gn[0:W]), B, tt)
        yb = _hgrn2(zh, _row(lb_cum[l] - lb_cum[0]), _row(gn[W:2 * W]), B, tt)

        wuq = jnp.pad(_pad_heads(mla_w_uq[l], MLA_QK), ((0, 2 * P - MLA_Q_RANK), (0, 0))).astype(BF16)
        wukv = mla_w_ukv[l].reshape(MLA_KV_RANK, N_HEADS, MLA_NOPE + MLA_V)
        wuk = _pad_heads(wukv[:, :, :MLA_NOPE].reshape(MLA_KV_RANK, -1), MLA_NOPE).astype(BF16)
        wuv = _pad_heads(wukv[:, :, MLA_NOPE:].reshape(MLA_KV_RANK, -1), MLA_V).astype(BF16)
        q, k, v, stats = _mla_prep(zm, _row(mla_gq[l], 2 * P), wuq, _row(mla_gkv[l]), wuk, wuv,
                                   _row(mla_gqn[l], P), _row(mla_gkn[l], P), cos_t, s_lo, s_hi, tt)
        visit_all = jnp.zeros((B * (lp // tq),), jnp.int32)
        yc = _attention(visit_all, stats, q, k, v, _row(gn[2 * W:3 * W]), B, tq, tk, True)

        q, k, v, stats = _fox_prep(zf, _row(fox_bf[l], P), _row(jnp.tile(fox_gqn[l], N_HEADS)),
                                   _row(jnp.tile(fox_gkn[l], N_HEADS)), B, tt)
        yd = _attention(_fox_first_chunk(stats, B, lp // tt), stats, q, k, v, _row(gn[3 * W:4 * W]),
                        B, tq, tk, False)

        wo = w_out[l].astype(BF16)
        if l % 2 == 0:
            hn, u2 = _outproj(h, ya, yb, yc, yd, wo, _row(norm2_g[l]), None, tt)
            j = l // 2
            h = _ffn(u2, hn, ffn_w_gate[j].astype(BF16), ffn_w_up[j].astype(BF16),
                     ffn_w_down[j].astype(BF16), tt, 1408)
        else:
            j = l // 2
            wr = _pad_cols(moe_w_router[j].astype(F32), P)
            wr_hi = wr.astype(BF16)
            wr = jnp.concatenate([wr_hi, (wr - wr_hi.astype(F32)).astype(BF16)], axis=1)
            hn, u2, route, route_t, counts = _outproj(h, ya, yb, yc, yd, wo, _row(norm2_g[l]), wr, tt)
            last = l == depth - 1
            h = _moe(u2, hn, route, route_t, counts, moe_w_gate[j].astype(BF16), moe_w_up[j].astype(BF16),
                     moe_w_down[j].astype(BF16), tt, 512, 1792, (B, S, lp) if last else None)
            if last:
                return h.reshape(B, S, D_MODEL)
    return h.reshape(B, lp, D_MODEL)[:, N_META:L]
```

```python
import functools
import math

import jax
import jax.numpy as jnp
from jax import lax
from jax.experimental import pallas as pl
from jax.experimental.pallas import tpu as pltpu

F32 = jnp.float32
BF16 = jnp.bfloat16

D_MODEL = 1024
N_META = 16
CHUNK = 64
SEQ_ALIGN = 128
EPS = 1e-6
GROUP_WIDTH = 256
N_HEADS = 4
HEAD_PAD = 128
LRU_C = 8.0
HG_CHUNK = 16
HG_BLOCK = 128
MLA_NOPE, MLA_ROPE, MLA_V = 64, 32, 64
MLA_QK = MLA_NOPE + MLA_ROPE
MLA_Q_RANK, MLA_KV_RANK = 192, 128
ROPE_THETA = 10000.0
FOX_HD = 64
N_EXPERTS = 8
LOG2E = 1.4426950408889634
ONES_LANE = 64
VMEM_LIMIT = 56 * 1024 * 1024

NT_DIMS = (((1,), (1,)), ((), ()))
TN_DIMS = (((0,), (0,)), ((), ()))


def _cparams(*sem):
    return pltpu.CompilerParams(dimension_semantics=sem, vmem_limit_bytes=VMEM_LIMIT)


def _tile(n, pref, align=SEQ_ALIGN):
    best = None
    for t in range(align, min(n, pref) + 1, align):
        if n % t == 0:
            best = t
    assert best is not None, (n, pref, align)
    return best


def _rms(x, width):
    return lax.rsqrt(jnp.sum(x * x, axis=-1, keepdims=True) * (1.0 / width) + EPS)


def _row_ssq(x):
    ones = jnp.ones((x.shape[1], HEAD_PAD), BF16)
    return jnp.dot((x * x).astype(BF16), ones, preferred_element_type=F32)


def _rms_tiles(x, width):
    return lax.rsqrt(_row_ssq(x) * (1.0 / width) + EPS)


def _sigmoid(x):
    return 0.5 * jnp.tanh(0.5 * x) + 0.5


def _log_sigmoid(x):
    return jnp.minimum(x, 0.0) - jnp.log(1.0 + jnp.exp(-jnp.abs(x)))


def _full(shape):
    return pl.BlockSpec(shape, lambda *_: (0,) * len(shape))


def _inproj_kernel(h_ref, g_ref, wl_ref, wh_ref, wm_ref, wf_ref, zl_ref, zh_ref, zm_ref, zf_ref):
    x = h_ref[...]
    u = (x * _rms(x, D_MODEL) * g_ref[...]).astype(BF16)
    zl_ref[...] = jnp.dot(u, wl_ref[...], preferred_element_type=F32)
    zh_ref[...] = jnp.dot(u, wh_ref[...], preferred_element_type=F32)
    zm_ref[...] = jnp.dot(u, wm_ref[...], preferred_element_type=F32)
    zf_ref[...] = jnp.dot(u, wf_ref[...], preferred_element_type=F32)


def _inproj(h, g, wl, wh, wm, wf, tm):
    R = h.shape[0]
    row = lambda n: pl.BlockSpec((tm, n), lambda i: (i, 0))
    return pl.pallas_call(
        _inproj_kernel,
        grid=(R // tm,),
        in_specs=[row(D_MODEL), _full((1, D_MODEL)), _full(wl.shape), _full(wh.shape),
                  _full(wm.shape), _full(wf.shape)],
        out_specs=[row(wl.shape[1]), row(wh.shape[1]), row(wm.shape[1]), row(wf.shape[1])],
        out_shape=[jax.ShapeDtypeStruct((R, w.shape[1]), F32) for w in (wl, wh, wm, wf)],
        compiler_params=_cparams("parallel"),
        name="inproj",
    )(h, g, wl, wh, wm, wf)


def _rglru_kernel(z_ref, cw_ref, cb_ref, wa_ref, ba_ref, wx_ref, bx_ref, lam_ref, gn_ref, y_ref,
                  xbuf, a_s, b_s, h_s, hst):
    W = GROUP_WIDTH
    tt = y_ref.shape[0]

    @pl.when(pl.program_id(1) == 0)
    def _():
        xbuf[0:8, :] = jnp.zeros((8, W), F32)
        hst[...] = jnp.zeros((1, W), F32)

    xa = z_ref[:, 0:W]
    ga = z_ref[:, W:2 * W]
    xbuf[8:8 + tt, :] = xa
    u = (cb_ref[...] + xbuf[5:5 + tt, :] * cw_ref[0:1, :] + xbuf[6:6 + tt, :] * cw_ref[1:2, :]
         + xbuf[7:7 + tt, :] * cw_ref[2:3, :] + xa * cw_ref[3:4, :])
    xbuf[0:8, :] = xbuf[tt:tt + 8, :]

    ub = u.astype(BF16)
    r = _sigmoid(jnp.dot(ub, wa_ref[...], preferred_element_type=F32) + ba_ref[...])
    i = _sigmoid(jnp.dot(ub, wx_ref[...], preferred_element_type=F32) + bx_ref[...])
    lam = lam_ref[...]
    softplus_neg_lam = jnp.maximum(-lam, 0.0) + jnp.log(1.0 + jnp.exp(-jnp.abs(lam)))
    a = jnp.exp((-LRU_C) * r * softplus_neg_lam)
    b = jnp.sqrt(1.0 - a * a) * (i * u)

    row = lax.broadcasted_iota(jnp.int32, (tt, W), 0) & 7
    for s in (1, 2, 4):
        ok = row >= s
        b = jnp.where(ok, a * pltpu.roll(b, s, 0) + b, b)
        a = jnp.where(ok, a * pltpu.roll(a, s, 0), a)
    a_s[...] = a
    b_s[...] = b

    def group(gi, hprev):
        sl = pl.ds(pl.multiple_of(gi * 8, 8), 8)
        hg = b_s[sl, :] + a_s[sl, :] * hprev
        h_s[sl, :] = hg
        return hg[7:8, :]

    hst[...] = lax.fori_loop(0, tt // 8, group, hst[...], unroll=8)

    c0 = math.sqrt(2.0 / math.pi)
    gelu = 0.5 * ga * (1.0 + jnp.tanh(c0 * (ga + 0.044715 * (ga * ga * ga))))
    y = h_s[...] * gelu
    y_ref[...] = (y * _rms(y, W) * gn_ref[...]).astype(BF16)


def _rglru(zl, cw, cb, wa, ba, wx, bx, lam, gn, B, tt):
    R = zl.shape[0]
    nt = R // B // tt
    W = GROUP_WIDTH
    return pl.pallas_call(
        _rglru_kernel,
        grid=(B, nt),
        in_specs=[pl.BlockSpec((tt, 2 * W), lambda b, t: (b * nt + t, 0)),
                  _full((4, W)), _full((1, W)), _full((W, W)), _full((1, W)), _full((W, W)),
                  _full((1, W)), _full((1, W)), _full((1, W))],
        out_specs=pl.BlockSpec((tt, W), lambda b, t: (b * nt + t, 0)),
        out_shape=jax.ShapeDtypeStruct((R, W), BF16),
        scratch_shapes=[pltpu.VMEM((tt + 8, W), F32), pltpu.VMEM((tt, W), F32),
                        pltpu.VMEM((tt, W), F32), pltpu.VMEM((tt, W), F32), pltpu.VMEM((1, W), F32)],
        compiler_params=_cparams("parallel", "arbitrary"),
        name="rglru",
    )(zl, cw, cb, wa, ba, wx, bx, lam, gn)


def _hgrn2_kernel(z_ref, lb_ref, gn_ref, y_ref, qd_s, ke_s, v_s, dec_s, o_s, st_s):
    W = GROUP_WIDTH
    nb, tt = y_ref.shape[0], y_ref.shape[1]
    C = HG_CHUNK

    @pl.when(pl.program_id(0) == 0)
    def _():
        st_s[...] = jnp.zeros(st_s.shape, F32)

    lb = lb_ref[...]
    la = jnp.log(lb)
    l1 = jnp.log(1.0 - lb)
    rowc = lax.broadcasted_iota(jnp.int32, (tt, W), 0) & (C - 1)
    lane_head = lax.broadcasted_iota(jnp.int32, (1, W), 1) // (W // N_HEADS)
    rr = lax.broadcasted_iota(jnp.int32, (HG_BLOCK, HG_BLOCK), 0)
    cc = lax.broadcasted_iota(jnp.int32, (HG_BLOCK, HG_BLOCK), 1)
    amask = (rr // C == cc // C) & (cc <= rr)

    for bi in range(nb):
        q = z_ref[bi, :, 0:W]
        fz = z_ref[bi, :, W:2 * W]
        v = z_ref[bi, :, 2 * W:3 * W]
        lq = l1 + _log_sigmoid(fz)
        logf = jnp.maximum(la, lq) + jnp.log(1.0 + jnp.exp(-jnp.abs(la - lq)))
        kin = 1.0 - jnp.exp(logf)

        b = logf
        s = 1
        while s < C:
            b = b + jnp.where(rowc >= s, pltpu.roll(b, s, 0), 0.0)
            s *= 2
        b3 = b.reshape(tt // C, C, W)
        tail = (jnp.broadcast_to(b3[:, C - 1:C, :], b3.shape) - b3).reshape(tt, W)
        qd = q * jnp.exp(b)
        kd = (kin * jnp.exp(-b)).astype(BF16)
        vb = v.astype(BF16)
        qd_s[bi] = qd.astype(BF16)
        ke_s[bi] = (kin * jnp.exp(tail)).astype(BF16)
        v_s[bi] = vb
        dec_s[bi] = jnp.exp(b + tail)

        for jb in range(tt // HG_BLOCK):
            sl = slice(jb * HG_BLOCK, (jb + 1) * HG_BLOCK)
            qb, kb, vv = qd[sl], kd[sl], vb[sl]
            acc = jnp.zeros((HG_BLOCK, W), F32)
            for hh in range(N_HEADS):
                hm = lane_head == hh
                qh = jnp.where(hm, qb, 0.0).astype(BF16)
                att = lax.dot_general(qh, kb, NT_DIMS, preferred_element_type=F32)
                att = jnp.where(amask, att, 0.0).astype(BF16)
                acc = jnp.where(hm, jnp.dot(att, vv, preferred_element_type=F32), acc)
            o_s[bi, sl, :] = acc

    def chunk(c, carry):
        sl = pl.ds(pl.multiple_of(c * C, C), C)
        for bi in range(nb):
            qc = qd_s[bi, sl, :]
            st = st_s[bi]
            qbd = jnp.concatenate([jnp.where(lane_head == hh, qc, jnp.zeros_like(qc))
                                   for hh in range(N_HEADS)], axis=0)
            res = lax.dot_general(qbd, st.astype(BF16), NT_DIMS, preferred_element_type=F32)
            oi = jnp.zeros((C, W), F32)
            for hh in range(N_HEADS):
                oi = jnp.where(lane_head == hh, res[hh * C:(hh + 1) * C, :], oi)
            o_s[bi, sl, :] = o_s[bi, sl, :] + oi
            upd = lax.dot_general(v_s[bi, sl, :], ke_s[bi, sl, :], TN_DIMS, preferred_element_type=F32)
            st_s[bi] = st * dec_s[bi, pl.ds(c * C, 1), :] + upd
        return carry

    lax.fori_loop(0, tt // C, chunk, 0, unroll=math.gcd(tt // C, 4))

    for bi in range(nb):
        g = z_ref[bi, :, 3 * W:4 * W]
        y = o_s[bi] * (g * _sigmoid(g))
        y_ref[bi] = (y * _rms(y, W) * gn_ref[...]).astype(BF16)


def _hgrn2(zh, lb, gn, B, tt):
    R = zh.shape[0]
    lp = R // B
    W = GROUP_WIDTH
    return pl.pallas_call(
        _hgrn2_kernel,
        grid=(lp // tt,),
        in_specs=[pl.BlockSpec((B, tt, 4 * W), lambda t: (0, t, 0)), _full((1, W)), _full((1, W))],
        out_specs=pl.BlockSpec((B, tt, W), lambda t: (0, t, 0)),
        out_shape=jax.ShapeDtypeStruct((B, lp, W), BF16),
        scratch_shapes=[pltpu.VMEM((B, tt, W), BF16), pltpu.VMEM((B, tt, W), BF16), pltpu.VMEM((B, tt, W), BF16),
                        pltpu.VMEM((B, tt, W), F32), pltpu.VMEM((B, tt, W), F32), pltpu.VMEM((B, W, W), F32)],
        compiler_params=_cparams("arbitrary"),
        name="hgrn2",
    )(zh.reshape(B, lp, 4 * W), lb, gn)


def _rope(x, cos, s_lo, s_hi):
    return x * cos + pltpu.roll(x, 16, 1) * s_hi + pltpu.roll(x, HEAD_PAD - 16, 1) * s_lo


STAT_ROWS = 8
STAT_C_FIRST, STAT_C_LAST, STAT_Q2, STAT_K2 = range(4)
SHIFT_LANE = 100


def _stat_block(rows):
    srow = lax.broadcasted_iota(jnp.int32, (STAT_ROWS, HEAD_PAD), 0)
    stat = jnp.zeros((STAT_ROWS, HEAD_PAD), F32)
    for rr, val in rows:
        stat = jnp.where(srow == rr, val, stat)
    return stat


def _mla_prep_kernel(z_ref, gq_ref, wuq_ref, gkv_ref, wuk_ref, wuv_ref, gqn_ref, gkn_ref,
                     cos_ref, slo_ref, shi_ref, q_ref, k_ref, v_ref, stat_ref):
    P = HEAD_PAD
    ckv = z_ref[:, 0:P]
    krb = z_ref[:, P:2 * P]
    cq = z_ref[:, 2 * P:4 * P]
    rq = _rms_tiles(cq, MLA_Q_RANK)
    qn = (cq * jnp.concatenate([rq, rq], axis=1) * gq_ref[...]).astype(BF16)
    kvn = (ckv * _rms_tiles(ckv, MLA_KV_RANK) * gkv_ref[...]).astype(BF16)
    q = jnp.dot(qn, wuq_ref[...], preferred_element_type=F32)
    kn = jnp.dot(kvn, wuk_ref[...], preferred_element_type=F32)
    vv = jnp.dot(kvn, wuv_ref[...], preferred_element_type=F32)
    cos, slo, shi = cos_ref[...], slo_ref[...], shi_ref[...]
    lane = lax.broadcasted_iota(jnp.int32, (1, P), 1)
    qscale = (MLA_QK ** -0.5) * LOG2E
    q2 = jnp.zeros((1, P), F32)
    k2 = jnp.zeros((1, P), F32)
    for hh in range(N_HEADS):
        sl = slice(hh * P, (hh + 1) * P)
        qh = q[:, sl]
        qh = qh * _rms_tiles(qh, MLA_QK) * gqn_ref[...]
        qh = _rope(qh, cos, slo, shi) * qscale
        q2 = jnp.where(lane == hh, jnp.max(_row_ssq(qh), axis=0, keepdims=True), q2)
        q_ref[:, sl] = qh.astype(BF16)
        kh = kn[:, sl] + krb
        kh = kh * _rms_tiles(kh, MLA_QK) * gkn_ref[...]
        kh = _rope(kh, cos, slo, shi)
        k2 = jnp.where(lane == hh, jnp.max(_row_ssq(kh), axis=0, keepdims=True), k2)
        k_ref[:, sl] = jnp.where(lane == SHIFT_LANE, 1.0, kh).astype(BF16)
        v_ref[:, sl] = jnp.where(lane == ONES_LANE, 1.0, vv[:, sl]).astype(BF16)
    stat_ref[...] = _stat_block(((STAT_Q2, q2), (STAT_K2, k2)))


def _mla_prep(zm, gq, wuq, gkv, wuk, wuv, gqn, gkn, cos, slo, shi, tm):
    R = zm.shape[0]
    P = HEAD_PAD
    nt = cos.shape[0] // tm
    row = lambda n: pl.BlockSpec((tm, n), lambda i: (i, 0))
    tab = pl.BlockSpec((tm, P), lambda i: (i % nt, 0))
    out = jax.ShapeDtypeStruct((R, N_HEADS * P), BF16)
    return pl.pallas_call(
        _mla_prep_kernel,
        grid=(R // tm,),
        in_specs=[row(4 * P), _full((1, 2 * P)), _full(wuq.shape), _full((1, P)), _full(wuk.shape),
                  _full(wuv.shape), _full((1, P)), _full((1, P)), tab, tab, tab],
        out_specs=[row(N_HEADS * P)] * 3 + [pl.BlockSpec((STAT_ROWS, P), lambda i: (i, 0))],
        out_shape=[out, out, out, jax.ShapeDtypeStruct((R // tm * STAT_ROWS, P), F32)],
        compiler_params=_cparams("parallel"),
        name="mla_prep",
    )(zm, gq, wuq, gkv, wuk, wuv, gqn, gkn, cos, slo, shi)


def _fox_prep_kernel(z_ref, bf_ref, gqn_ref, gkn_ref, hsum_ref, place_ref, q_ref, k_ref, v_ref, stat_ref, carry):
    P = HEAD_PAD
    W = GROUP_WIDTH
    tt = q_ref.shape[0]

    @pl.when(pl.program_id(1) == 0)
    def _():
        carry[...] = jnp.zeros((1, P), F32)

    c = _log_sigmoid(z_ref[:, 3 * W:] + bf_ref[...])
    row = lax.broadcasted_iota(jnp.int32, (tt, P), 0)
    s = 1
    while s < tt:
        c = c + jnp.where(row >= s, pltpu.roll(c, s, 0), 0.0)
        s *= 2
    c = c + carry[...]
    carry[...] = c[tt - 1:tt, :]
    c = c * LOG2E
    c1 = c.astype(BF16).astype(F32)
    c2 = (c - c1).astype(BF16).astype(F32)
    c3 = c - c1 - c2

    def norm_place(x, gain):
        ssq = jnp.dot((x * x).astype(BF16), hsum_ref[...], preferred_element_type=F32)
        xn = (x * lax.rsqrt(ssq * (1.0 / FOX_HD) + EPS) * gain).astype(BF16)
        return jnp.dot(xn, place_ref[...], preferred_element_type=F32)

    qscale = (FOX_HD ** -0.5) * LOG2E
    qp = norm_place(z_ref[:, 0:W], gqn_ref[...] * qscale)
    kp = norm_place(z_ref[:, W:2 * W], gkn_ref[...])
    vp = jnp.dot(z_ref[:, 2 * W:3 * W].astype(BF16), place_ref[...], preferred_element_type=F32)

    lane = lax.broadcasted_iota(jnp.int32, (1, P), 1)
    q2 = jnp.zeros((1, P), F32)
    k2 = jnp.zeros((1, P), F32)
    for hh in range(N_HEADS):
        hs = slice(hh * P, (hh + 1) * P)
        p1 = c1[:, hh:hh + 1]
        p2 = c2[:, hh:hh + 1]
        p3 = c3[:, hh:hh + 1]
        qh = qp[:, hs]
        q2 = jnp.where(lane == hh, jnp.max(_row_ssq(qh), axis=0, keepdims=True), q2)
        qh = jnp.where(lane == 64, p1, jnp.where(lane == 65, p2, jnp.where(lane == 66, p3, qh)))
        qh = jnp.where((lane >= 67) & (lane < 70), 1.0, qh)
        q_ref[:, hs] = qh.astype(BF16)
        kh = kp[:, hs]
        k2 = jnp.where(lane == hh, jnp.max(_row_ssq(kh), axis=0, keepdims=True), k2)
        kh = jnp.where(lane == 67, -p1, jnp.where(lane == 68, -p2, jnp.where(lane == 69, -p3, kh)))
        kh = jnp.where(((lane >= 64) & (lane < 67)) | (lane == SHIFT_LANE), 1.0, kh)
        k_ref[:, hs] = kh.astype(BF16)
        v_ref[:, hs] = jnp.where(lane == ONES_LANE, 1.0, vp[:, hs]).astype(BF16)

    stat_ref[...] = _stat_block(((STAT_C_FIRST, c[0:1, :]), (STAT_C_LAST, c[tt - 1:tt, :]),
                                 (STAT_Q2, q2), (STAT_K2, k2)))


def _fox_prep(zf, bf, gqn, gkn, B, tt):
    R = zf.shape[0]
    nt = R // B // tt
    P = HEAD_PAD
    W = GROUP_WIDTH
    blk = lambda n: pl.BlockSpec((tt, n), lambda b, t: (b * nt + t, 0))
    out = jax.ShapeDtypeStruct((R, N_HEADS * P), BF16)
    head_sum = jnp.kron(jnp.eye(N_HEADS, dtype=F32), jnp.ones((FOX_HD, FOX_HD), F32)).astype(BF16)
    place = _pad_heads(jnp.eye(W, dtype=F32), FOX_HD).astype(BF16)
    return pl.pallas_call(
        _fox_prep_kernel,
        grid=(B, nt),
        in_specs=[blk(zf.shape[1]), _full((1, P)), _full((1, W)), _full((1, W)), _full((W, W)),
                  _full((W, N_HEADS * P))],
        out_specs=[blk(N_HEADS * P)] * 3 + [pl.BlockSpec((STAT_ROWS, P), lambda b, t: (b * nt + t, 0))],
        out_shape=[out, out, out, jax.ShapeDtypeStruct((B * nt * STAT_ROWS, P), F32)],
        scratch_shapes=[pltpu.VMEM((1, P), F32)],
        compiler_params=_cparams("parallel", "arbitrary"),
        name="fox_prep",
    )(zf, bf, gqn, gkn, head_sum, place)


SKIP_LOG2_MARGIN = 40.0
NORM_SLACK = 1.02


def _fox_first_chunk(stats, B, nt):
    st = stats.reshape(B, nt, STAT_ROWS, HEAD_PAD)[..., :N_HEADS]
    c_first, c_last = st[:, :, STAT_C_FIRST], st[:, :, STAT_C_LAST]
    bound = jnp.sqrt(jnp.max(st[:, :, STAT_Q2], axis=1) * jnp.max(st[:, :, STAT_K2], axis=1)) * NORM_SLACK
    gap = 2.0 * bound[:, None, None, :] + c_first[:, :, None, :] - c_last[:, None, :, :]
    earlier = jnp.arange(nt)[None, :] < jnp.arange(nt)[:, None]
    skip = (gap < -SKIP_LOG2_MARGIN) & earlier[None, :, :, None]
    return jnp.min(jnp.sum(skip, axis=2), axis=-1).astype(jnp.int32).reshape(-1)


MAX_SCORE_SHIFT = 48.0


def _score_bound(stats, B, nt):
    st = stats.reshape(B, nt, STAT_ROWS, HEAD_PAD)[..., :N_HEADS]
    bound = jnp.sqrt(jnp.max(st[:, :, STAT_Q2], axis=1) * jnp.max(st[:, :, STAT_K2], axis=1)) * NORM_SLACK
    return bound.reshape(-1), jnp.all(bound <= MAX_SCORE_SHIFT)


def _attn_kernel(first_ref, shift_ref, q_ref, k_ref, v_ref, gn_ref, y_ref, m_s, acc_s, o_s, q_s, *,
                 chunk_causal, tk, fixed_shift):
    P = HEAD_PAD
    tq = q_ref.shape[0]
    lp = k_ref.shape[0]
    blk = pl.program_id(0) * pl.num_programs(1) + pl.program_id(1)
    q0 = pl.program_id(1) * tq

    qpos = q0 + lax.broadcasted_iota(jnp.int32, (tq, 1), 0)
    if chunk_causal:
        qlim = N_META + CHUNK * ((qpos + (CHUNK - N_META)) // CHUNK)
        reach = N_META
    else:
        qlim = qpos + 1
        reach = 0
    n_full = (q0 + reach) // tk
    n_diag = (jnp.minimum(q0 + tq, lp) + tk - 1) // tk

    if fixed_shift:
        lane = lax.broadcasted_iota(jnp.int32, (1, P), 1)
        for hh in range(N_HEADS):
            shift = jnp.full((1, P), -shift_ref[pl.program_id(0) * N_HEADS + hh], F32).astype(BF16)
            q_s[hh] = jnp.where(lane == SHIFT_LANE, shift, q_ref[:, hh * P:(hh + 1) * P])
    else:
        m_s[...] = jnp.full(m_s.shape, -jnp.inf, F32)
    acc_s[...] = jnp.zeros(acc_s.shape, F32)

    def visit(k0, width, masked):
        ks = pl.ds(pl.multiple_of(k0, SEQ_ALIGN), width)
        if masked:
            vis = (k0 + lax.broadcasted_iota(jnp.int32, (1, width), 1)) < qlim
        for hh in range(N_HEADS):
            hs = slice(hh * P, (hh + 1) * P)
            qh = q_s[hh] if fixed_shift else q_ref[:, hs]
            s = lax.dot_general(qh, k_ref[ks, hs], NT_DIMS, preferred_element_type=F32)
            if masked:
                s = jnp.where(vis, s, -jnp.inf)
            if fixed_shift:
                acc_s[hh] += jnp.dot(jnp.exp2(s).astype(BF16), v_ref[ks, hs], preferred_element_type=F32)
                continue
            tiles = [s[:, c * P:(c + 1) * P] for c in range(width // P)]
            mx = tiles[0]
            for t in tiles[1:]:
                mx = jnp.maximum(mx, t)
            m_old = m_s[hh]
            m_new = jnp.maximum(m_old, jnp.max(mx, axis=-1, keepdims=True))
            p = jnp.concatenate([jnp.exp2((t - m_new).astype(BF16)) for t in tiles], axis=1)
            acc_s[hh] = jnp.exp2(m_old - m_new) * acc_s[hh] + jnp.dot(
                p, v_ref[ks, hs], preferred_element_type=F32)
            m_s[hh] = m_new

    def full_body(j, carry):
        visit(j * tk, tk, False)
        return carry

    def masked_body(j, carry):
        visit(j * tk, tk, True)
        return carry

    lax.fori_loop(first_ref[blk], n_full, full_body, 0)
    lax.fori_loop(n_full, n_diag, masked_body, 0)
    if chunk_causal:
        @pl.when(q0 + tq < lp)
        def _():
            visit(q0 + tq, SEQ_ALIGN, True)

    hd = GROUP_WIDTH // N_HEADS
    for hh in range(N_HEADS):
        acc = acc_s[hh]
        o_s[:, hh * hd:(hh + 1) * hd] = acc[:, 0:hd] / acc[:, ONES_LANE:ONES_LANE + 1]
    y = o_s[...]
    y_ref[...] = (y * _rms(y, GROUP_WIDTH) * gn_ref[...]).astype(BF16)


def _attention(first_chunk, stats, q, k, v, gn, B, tq, tk, chunk_causal):
    R = q.shape[0]
    lp = R // B
    nq = lp // tq
    P = HEAD_PAD
    W = GROUP_WIDTH
    shift, shift_is_safe = _score_bound(stats, B, nq)
    kv_spec = pl.BlockSpec((lp, N_HEADS * P), lambda b, i, fc, sh: (b, 0))

    def run(fixed_shift):
        return pl.pallas_call(
            functools.partial(_attn_kernel, chunk_causal=chunk_causal, tk=tk, fixed_shift=fixed_shift),
            grid_spec=pltpu.PrefetchScalarGridSpec(
                num_scalar_prefetch=2,
                grid=(B, nq),
                in_specs=[pl.BlockSpec((tq, N_HEADS * P), lambda b, i, fc, sh: (b * nq + i, 0)),
                          kv_spec, kv_spec, pl.BlockSpec((1, W), lambda b, i, fc, sh: (0, 0))],
                out_specs=pl.BlockSpec((tq, W), lambda b, i, fc, sh: (b * nq + i, 0)),
                scratch_shapes=[pltpu.VMEM((N_HEADS, tq, P), F32), pltpu.VMEM((N_HEADS, tq, P), F32),
                                pltpu.VMEM((tq, W), F32), pltpu.VMEM((N_HEADS, tq, P), BF16)],
            ),
            out_shape=jax.ShapeDtypeStruct((R, W), BF16),
            compiler_params=_cparams("parallel", "arbitrary"),
            name=("mla_attn" if chunk_causal else "fox_attn") + ("_shift" if fixed_shift else ""),
        )(first_chunk, shift, q, k, v, gn)

    return lax.cond(shift_is_safe, lambda: run(True), lambda: run(False))


ROUTE_E1, ROUTE_E2, ROUTE_R1, ROUTE_R2, ROUTE_G1, ROUTE_G2 = range(6)
ROUTE_ROWS = 8


def _outproj_kernel(h_ref, ya_ref, yb_ref, yc_ref, yd_ref, w_ref, g_ref, *rest, with_router):
    if with_router:
        wr_ref, hn_ref, u_ref, route_ref, route_t_ref, cnt_ref = rest
    else:
        hn_ref, u_ref = rest
    y = jnp.concatenate([ya_ref[...], yb_ref[...], yc_ref[...], yd_ref[...]], axis=1)
    acc = h_ref[...] + jnp.dot(y, w_ref[...], preferred_element_type=F32)
    hn_ref[...] = acc
    u = acc * _rms(acc, D_MODEL) * g_ref[...]
    if not with_router:
        u_ref[...] = u.astype(BF16)
        return
    u_ref[...] = u
    tm = u.shape[0]

    @pl.when(pl.program_id(0) == 0)
    def _():
        cnt_ref[...] = jnp.zeros(cnt_ref.shape, F32)

    u_hi = u.astype(BF16)
    u_lo = (u - u_hi.astype(F32)).astype(BF16)
    hi_both = jnp.dot(u_hi, wr_ref[...], preferred_element_type=F32)
    logits = (hi_both[:, :HEAD_PAD] + hi_both[:, HEAD_PAD:]
              + jnp.dot(u_lo, wr_ref[:, :HEAD_PAD], preferred_element_type=F32))
    lane = lax.broadcasted_iota(jnp.int32, logits.shape, 1).astype(F32)
    lg = jnp.where(lane < N_EXPERTS, logits, -jnp.inf)
    m1 = jnp.max(lg, axis=-1, keepdims=True)
    i1 = jnp.min(jnp.where(lg == m1, lane, 1e9), axis=-1, keepdims=True)
    lg2 = jnp.where(lane == i1, -jnp.inf, lg)
    m2 = jnp.max(lg2, axis=-1, keepdims=True)
    i2 = jnp.min(jnp.where(lg2 == m2, lane, 1e9), axis=-1, keepdims=True)
    e = jnp.exp(m2 - m1)
    g1 = 1.0 / (1.0 + e)
    picks = jnp.where(lane == i1, 1.0, 0.0) + jnp.where(lane == i2, 1.0, 0.0)
    earlier = (lax.broadcasted_iota(jnp.int32, (tm, tm), 0) > lax.broadcasted_iota(jnp.int32, (tm, tm), 1))
    base = cnt_ref[...] + jnp.dot(earlier.astype(BF16), picks.astype(BF16), preferred_element_type=F32)
    r1 = jnp.sum(jnp.where(lane == i1, base, 0.0), axis=-1, keepdims=True)
    r2 = jnp.sum(jnp.where(lane == i2, base, 0.0), axis=-1, keepdims=True)
    cnt_ref[...] = cnt_ref[...] + jnp.sum(picks, axis=0, keepdims=True)
    rec = jnp.zeros(logits.shape, F32)
    for ln, val in ((ROUTE_E1, i1), (ROUTE_E2, i2), (ROUTE_R1, r1), (ROUTE_R2, r2),
                    (ROUTE_G1, g1), (ROUTE_G2, e * g1)):
        rec = jnp.where(lane == ln, val, rec)
    route_ref[...] = rec
    route_t_ref[...] = rec.T[0:ROUTE_ROWS, :]


def _outproj(h, ya, yb, yc, yd, w, g, wr, tm):
    R = h.shape[0]
    W = GROUP_WIDTH
    with_router = wr is not None
    row = lambda n: pl.BlockSpec((tm, n), lambda i: (i, 0))
    nt = yb.shape[1] // tm
    yb_spec = pl.BlockSpec((None, tm, W), lambda i: (i // nt, i % nt, 0))
    in_specs = [row(D_MODEL), row(W), yb_spec, row(W), row(W), _full(w.shape), _full((1, D_MODEL))]
    out_specs = [row(D_MODEL), row(D_MODEL)]
    out_shape = [jax.ShapeDtypeStruct((R, D_MODEL), F32),
                 jax.ShapeDtypeStruct((R, D_MODEL), F32 if with_router else BF16)]
    args = [h, ya, yb, yc, yd, w, g]
    if with_router:
        in_specs.append(_full(wr.shape))
        out_specs += [row(HEAD_PAD), pl.BlockSpec((ROUTE_ROWS, tm), lambda i: (0, i)), _full((1, HEAD_PAD))]
        out_shape += [jax.ShapeDtypeStruct((R, HEAD_PAD), F32), jax.ShapeDtypeStruct((ROUTE_ROWS, R), F32),
                      jax.ShapeDtypeStruct((1, HEAD_PAD), F32)]
        args.append(wr)
    return pl.pallas_call(
        functools.partial(_outproj_kernel, with_router=with_router),
        grid=(R // tm,),
        in_specs=in_specs, out_specs=out_specs, out_shape=out_shape,
        compiler_params=_cparams("arbitrary" if with_router else "parallel"),
        name="outproj_router" if with_router else "outproj",
    )(*args)


def _ffn_kernel(u_ref, h_ref, wg_ref, wu_ref, wd_ref, o_ref):
    @pl.when(pl.program_id(1) == 0)
    def _():
        o_ref[...] = h_ref[...]

    u = u_ref[...]
    a = jnp.dot(u, wg_ref[...], preferred_element_type=F32)
    b = jnp.dot(u, wu_ref[...], preferred_element_type=F32)
    hid = (a * _sigmoid(a) * b).astype(BF16)
    o_ref[...] += jnp.dot(hid, wd_ref[...], preferred_element_type=F32)


def _ffn(u, h, wg, wu, wd, tm, tf):
    R = h.shape[0]
    dff = wg.shape[1]
    return pl.pallas_call(
        _ffn_kernel,
        grid=(R // tm, dff // tf),
        in_specs=[pl.BlockSpec((tm, D_MODEL), lambda i, f: (i, 0)),
                  pl.BlockSpec((tm, D_MODEL), lambda i, f: (i, 0)),
                  pl.BlockSpec((D_MODEL, tf), lambda i, f: (0, f)),
                  pl.BlockSpec((D_MODEL, tf), lambda i, f: (0, f)),
                  pl.BlockSpec((tf, D_MODEL), lambda i, f: (f, 0))],
        out_specs=pl.BlockSpec((tm, D_MODEL), lambda i, f: (i, 0)),
        out_shape=jax.ShapeDtypeStruct((R, D_MODEL), F32),
        compiler_params=_cparams("parallel", "arbitrary"),
        name="ffn",
    )(u, h, wg, wu, wd)


def _row_copy(src_ref, src_row, dst_ref, dst_row, sem):
    return pltpu.make_async_copy(src_ref.at[pl.ds(src_row, 1)], dst_ref.at[pl.ds(dst_row, 1)], sem)


def _dispatch_kernel(pos1_ref, pos2_ref, empty_ref, u_ref, xs_ref, zero_s, sem):
    tm = u_ref.shape[0]
    t0 = pl.program_id(0) * tm
    n_empty = empty_ref.shape[0]

    @pl.when(pl.program_id(0) == 0)
    def _():
        zero_s[...] = jnp.zeros(zero_s.shape, F32)

        def fill(i, carry):
            _row_copy(zero_s, 0, xs_ref, empty_ref[i], sem).start()
            return carry

        lax.fori_loop(0, n_empty, fill, 0, unroll=8)

        def fill_drain(i, carry):
            _row_copy(zero_s, 0, xs_ref, 0, sem).wait()
            return carry

        lax.fori_loop(0, n_empty, fill_drain, 0, unroll=8)

    def issue(r, carry):
        _row_copy(u_ref, r, xs_ref, pos1_ref[t0 + r], sem).start()
        _row_copy(u_ref, r, xs_ref, pos2_ref[t0 + r], sem).start()
        return carry

    lax.fori_loop(0, tm, issue, 0, unroll=8)

    def drain(r, carry):
        _row_copy(u_ref, r, xs_ref, 0, sem).wait()
        _row_copy(u_ref, r, xs_ref, 0, sem).wait()
        return carry

    lax.fori_loop(0, tm, drain, 0, unroll=8)


def _dispatch(pos1, pos2, empty, u, n_slots, tm):
    R = u.shape[0]
    return pl.pallas_call(
        _dispatch_kernel,
        grid_spec=pltpu.PrefetchScalarGridSpec(
            num_scalar_prefetch=3,
            grid=(R // tm,),
            in_specs=[pl.BlockSpec((tm, D_MODEL), lambda i, p1, p2, em: (i, 0))],
            out_specs=pl.BlockSpec(memory_space=pl.ANY),
            scratch_shapes=[pltpu.VMEM((8, D_MODEL), F32), pltpu.SemaphoreType.DMA],
        ),
        out_shape=jax.ShapeDtypeStruct((n_slots, D_MODEL), F32),
        compiler_params=_cparams("arbitrary"),
        name="moe_dispatch",
    )(pos1, pos2, empty, u)


def _expert_ffn_kernel(te_ref, nt_ref, x_ref, wg_ref, wu_ref, wd_ref, y_ref, xb_s):
    f = pl.program_id(1)
    used = pl.program_id(0) < nt_ref[0]

    @pl.when(jnp.logical_not(used) & (f == 0))
    def _():
        y_ref[...] = jnp.zeros(y_ref.shape, F32)

    @pl.when(used)
    def _():
        @pl.when(f == 0)
        def _():
            xb_s[...] = x_ref[...].astype(BF16)

        x = xb_s[...]
        a = jnp.dot(x, wg_ref[...], preferred_element_type=F32)
        b = jnp.dot(x, wu_ref[...], preferred_element_type=F32)
        hid = (a * _sigmoid(a) * b).astype(BF16)
        out = jnp.dot(hid, wd_ref[...], preferred_element_type=F32)

        @pl.when(f == 0)
        def _():
            y_ref[...] = out

        @pl.when(f > 0)
        def _():
            y_ref[...] += out


def _expert_ffn(tile_expert, n_tiles, xs, wg, wu, wd, tm, tf):
    n_slots = xs.shape[0]
    dff = wg.shape[2]
    row_map = lambda i, f, te, nt: (jnp.minimum(i, nt[0] - 1), 0)
    out_map = lambda i, f, te, nt: (i, 0)
    return pl.pallas_call(
        _expert_ffn_kernel,
        grid_spec=pltpu.PrefetchScalarGridSpec(
            num_scalar_prefetch=2,
            grid=(n_slots // tm, dff // tf),
            in_specs=[pl.BlockSpec((tm, D_MODEL), row_map),
                      pl.BlockSpec((None, D_MODEL, tf), lambda i, f, te, nt: (te[i], 0, f)),
                      pl.BlockSpec((None, D_MODEL, tf), lambda i, f, te, nt: (te[i], 0, f)),
                      pl.BlockSpec((None, tf, D_MODEL), lambda i, f, te, nt: (te[i], f, 0))],
            out_specs=pl.BlockSpec((tm, D_MODEL), out_map),
            scratch_shapes=[pltpu.VMEM((tm, D_MODEL), BF16)],
        ),
        out_shape=jax.ShapeDtypeStruct((n_slots, D_MODEL), F32),
        compiler_params=_cparams("arbitrary", "arbitrary"),
        name="moe_expert_ffn",
    )(tile_expert, n_tiles, xs, wg, wu, wd)


def _combine_kernel(pos1_ref, pos2_ref, h_ref, route_ref, ys_ref, o_ref, y1_s, y2_s, sem, *, row0):
    tm = h_ref.shape[0]
    t0 = row0(pl.program_id(0), pl.program_id(1))

    def issue(r, carry):
        _row_copy(ys_ref, pos1_ref[t0 + r], y1_s, r, sem).start()
        _row_copy(ys_ref, pos2_ref[t0 + r], y2_s, r, sem).start()
        return carry

    lax.fori_loop(0, tm, issue, 0, unroll=8)

    def drain(r, carry):
        _row_copy(ys_ref, 0, y1_s, r, sem).wait()
        _row_copy(ys_ref, 0, y2_s, r, sem).wait()
        return carry

    lax.fori_loop(0, tm, drain, 0, unroll=8)
    rec = route_ref[...]
    g1 = rec[:, ROUTE_G1:ROUTE_G1 + 1]
    g2 = rec[:, ROUTE_G2:ROUTE_G2 + 1]
    o_ref[...] = h_ref[...] + g1 * y1_s[...] + g2 * y2_s[...]


def _combine(pos1, pos2, h, route, ys, tm, frames=None):
    R = h.shape[0]
    if frames is None:
        grid = (R // tm, 1)
        row0 = lambda i, j: i * tm
        in_rows = lambda n: pl.BlockSpec((tm, n), lambda i, j, p1, p2: (i, 0))
        out_rows = R
        out_spec = pl.BlockSpec((tm, D_MODEL), lambda i, j, p1, p2: (i, 0))
    else:
        B, S, lp = frames
        grid = (B, S // tm)
        row0 = lambda b, w: b * lp + N_META + w * tm
        in_rows = lambda n: pl.BlockSpec((pl.Element(tm), pl.Element(n)),
                                         lambda b, w, p1, p2: (pl.multiple_of(row0(b, w), 8), 0))
        out_rows = B * S
        out_spec = pl.BlockSpec((tm, D_MODEL), lambda b, w, p1, p2: (b * (S // tm) + w, 0))
    return pl.pallas_call(
        functools.partial(_combine_kernel, row0=row0),
        grid_spec=pltpu.PrefetchScalarGridSpec(
            num_scalar_prefetch=2,
            grid=grid,
            in_specs=[in_rows(D_MODEL), in_rows(route.shape[1]), pl.BlockSpec(memory_space=pl.ANY)],
            out_specs=out_spec,
            scratch_shapes=[pltpu.VMEM((tm, D_MODEL), F32), pltpu.VMEM((tm, D_MODEL), F32),
                            pltpu.SemaphoreType.DMA],
        ),
        out_shape=jax.ShapeDtypeStruct((out_rows, D_MODEL), F32),
        compiler_params=_cparams("arbitrary", "arbitrary"),
        name="moe_combine",
    )(pos1, pos2, h, route, ys)


def _moe(u, h, route, route_t, counts, wg, wu, wd, tm_rows, tm_expert, tf, frames=None):
    R = h.shape[0]
    ne = wg.shape[0]
    cnt = counts[0, :ne].astype(jnp.int32)
    padded = -(-cnt // tm_expert) * tm_expert
    ends = jnp.cumsum(padded)
    offs = ends - padded
    col = lambda ln: route_t[ln].astype(jnp.int32)
    pos1 = jnp.take(offs, col(ROUTE_E1)) + col(ROUTE_R1)
    pos2 = jnp.take(offs, col(ROUTE_E2)) + col(ROUTE_R2)
    n_slots = (2 * R // tm_expert + ne) * tm_expert
    n_tiles = (ends[-1] // tm_expert).reshape(1)
    tile_start = jnp.arange(n_slots // tm_expert, dtype=jnp.int32) * tm_expert
    tile_expert = jnp.sum(tile_start[:, None] >= ends[None, :], axis=1).astype(jnp.int32)
    last_expert = jnp.sum((ends[-1] - 1) >= ends).astype(jnp.int32)
    tile_expert = jnp.minimum(tile_expert, last_expert)

    n_empty = n_slots - 2 * R
    starts = jnp.concatenate([offs + cnt, ends[-1:]])
    sizes = jnp.concatenate([padded - cnt, n_slots - ends[-1:]])
    cum = jnp.cumsum(sizes)
    k = jnp.arange(n_empty, dtype=jnp.int32)
    rng = jnp.sum(k[:, None] >= cum[None, :], axis=1)
    empty = (jnp.take(starts, rng) + k - jnp.take(cum - sizes, rng)).astype(jnp.int32)

    xs = _dispatch(pos1, pos2, empty, u, n_slots, tm_rows)
    ys = _expert_ffn(tile_expert, n_tiles, xs, wg, wu, wd, tm_expert, tf)
    if frames is None:
        return _combine(pos1, pos2, h, route, ys, tm_rows)
    return _combine(pos1, pos2, h, route, ys, _tile(frames[1], tm_rows), frames)


def _pad_heads(w, real):
    rows = w.shape[0]
    w = w.reshape(rows, N_HEADS, real)
    return jnp.pad(w, ((0, 0), (0, 0), (0, HEAD_PAD - real))).reshape(rows, N_HEADS * HEAD_PAD)


def _pad_cols(w, n):
    return jnp.pad(w, ((0, 0), (0, n - w.shape[1])))


def _row(v, n=None):
    v = v.reshape(1, -1).astype(F32)
    return v if n is None else _pad_cols(v, n)


def _block_diag(w):
    nb, c, d = w.shape
    eye = jnp.eye(nb, dtype=w.dtype)
    return (eye[:, None, :, None] * w[:, :, None, :]).reshape(nb * c, nb * d)


def _rope_tables(lp):
    half = MLA_ROPE // 2
    inv = ROPE_THETA ** (-jnp.arange(half, dtype=F32) / half)
    ang = jnp.arange(lp, dtype=jnp.int32).astype(F32)[:, None] * inv[None, :]
    cos, sin = jnp.cos(ang), jnp.sin(ang)
    one = jnp.ones((lp, MLA_NOPE), F32)
    zero = jnp.zeros((lp, MLA_NOPE), F32)
    zh = jnp.zeros((lp, half), F32)
    tail1 = jnp.ones((lp, HEAD_PAD - MLA_QK), F32)
    tail0 = jnp.zeros((lp, HEAD_PAD - MLA_QK), F32)
    cos_t = jnp.concatenate([one, cos, cos, tail1], axis=1)
    s_lo = jnp.concatenate([zero, -sin, zh, tail0], axis=1)
    s_hi = jnp.concatenate([zero, zh, sin, tail0], axis=1)
    return cos_t, s_lo, s_hi


def kernel(x, meta, norm1_g, norm2_g, w_in, w_out, out_norm_g, lru_conv_w, lru_conv_b, lru_wa, lru_ba, lru_wx, lru_bx, lru_lambda, hg_lb_logits, mla_gq, mla_w_uq, mla_gkv, mla_w_ukv, mla_gqn, mla_gkn, fox_gqn, fox_gkn, fox_bf, ffn_w_gate, ffn_w_up, ffn_w_down, moe_w_router, moe_w_gate, moe_w_up, moe_w_down):
    B, S, _ = x.shape
    depth = w_in.shape[0]
    L = N_META + S
    lp = -(-L // SEQ_ALIGN) * SEQ_ALIGN
    R = B * lp
    W = GROUP_WIDTH
    P = HEAD_PAD

    tt = _tile(lp, 640)
    tq = tk = tt

    h = jnp.concatenate([jnp.broadcast_to(meta[None].astype(x.dtype), (B, N_META, D_MODEL)), x,
                         jnp.zeros((B, lp - L, D_MODEL), x.dtype)], axis=1).reshape(R, D_MODEL)
    cos_t, s_lo, s_hi = _rope_tables(lp)
    lb_cum = jnp.cumsum(jax.nn.softmax(hg_lb_logits.astype(F32), axis=0), axis=0)

    o = 0
    offs = []
    for n in (W, W, W, W, W, W, MLA_Q_RANK, MLA_KV_RANK, MLA_ROPE, W, W, W, N_HEADS):
        offs.append(o)
        o += n
    (o_xa, _, o_hq, _, _, _, o_cq, o_ckv, o_kr, o_fq, o_fk, o_fv, o_ff) = offs

    for l in range(depth):
        wcols = lambda start, n: lax.slice(w_in, (l, 0, start), (l + 1, D_MODEL, start + n))[0].astype(BF16)
        wl = wcols(o_xa, 2 * W)
        wh = wcols(o_hq, 4 * W)
        wm = jnp.concatenate([
            wcols(o_ckv, MLA_KV_RANK),
            jnp.zeros((D_MODEL, MLA_NOPE), BF16), wcols(o_kr, MLA_ROPE),
            jnp.zeros((D_MODEL, P - MLA_QK), BF16),
            _pad_cols(wcols(o_cq, MLA_Q_RANK), 2 * P)], axis=1)
        wf = jnp.concatenate([wcols(o_fq, 3 * W), _pad_cols(wcols(o_ff, N_HEADS), P)], axis=1)
        zl, zh, zm, zf = _inproj(h, _row(norm1_g[l]), wl, wh, wm, wf, tt)

        gn = out_norm_g[l].astype(F32)
        ya = _rglru(zl, lru_conv_w[l].astype(F32), _row(lru_conv_b[l]),
                    _block_diag(lru_wa[l]).astype(BF16), _row(lru_ba[l]),
                    _block_diag(lru_wx[l]).astype(BF16), _row(lru_bx[l]),
                    _row(lru_lambda[l]), _row(gn[0:W]), B, tt)
        yb = _hgrn2(zh, _row(lb_cum[l] - lb_cum[0]), _row(gn[W:2 * W]), B, tt)

        wuq = jnp.pad(_pad_heads(mla_w_uq[l], MLA_QK), ((0, 2 * P - MLA_Q_RANK), (0, 0))).astype(BF16)
        wukv = mla_w_ukv[l].reshape(MLA_KV_RANK, N_HEADS, MLA_NOPE + MLA_V)
        wuk = _pad_heads(wukv[:, :, :MLA_NOPE].reshape(MLA_KV_RANK, -1), MLA_NOPE).astype(BF16)
        wuv = _pad_heads(wukv[:, :, MLA_NOPE:].reshape(MLA_KV_RANK, -1), MLA_V).astype(BF16)
        q, k, v, stats = _mla_prep(zm, _row(mla_gq[l], 2 * P), wuq, _row(mla_gkv[l]), wuk, wuv,
                                   _row(mla_gqn[l], P), _row(mla_gkn[l], P), cos_t, s_lo, s_hi, tt)
        visit_all = jnp.zeros((B * (lp // tq),), jnp.int32)
        yc = _attention(visit_all, stats, q, k, v, _row(gn[2 * W:3 * W]), B, tq, tk, True)

        q, k, v, stats = _fox_prep(zf, _row(fox_bf[l], P), _row(jnp.tile(fox_gqn[l], N_HEADS)),
                                   _row(jnp.tile(fox_gkn[l], N_HEADS)), B, tt)
        yd = _attention(_fox_first_chunk(stats, B, lp // tt), stats, q, k, v, _row(gn[3 * W:4 * W]),
                        B, tq, tk, False)

        wo = w_out[l].astype(BF16)
        if l % 2 == 0:
            hn, u2 = _outproj(h, ya, yb, yc, yd, wo, _row(norm2_g[l]), None, tt)
            j = l // 2
            h = _ffn(u2, hn, ffn_w_gate[j].astype(BF16), ffn_w_up[j].astype(BF16),
                     ffn_w_down[j].astype(BF16), tt, 1408)
        else:
            j = l // 2
            wr = _pad_cols(moe_w_router[j].astype(F32), P)
            wr_hi = wr.astype(BF16)
            wr = jnp.concatenate([wr_hi, (wr - wr_hi.astype(F32)).astype(BF16)], axis=1)
            hn, u2, route, route_t, counts = _outproj(h, ya, yb, yc, yd, wo, _row(norm2_g[l]), wr, tt)
            last = l == depth - 1
            h = _moe(u2, hn, route, route_t, counts, moe_w_gate[j].astype(BF16), moe_w_up[j].astype(BF16),
                     moe_w_down[j].astype(BF16), tt, 512, 1792, (B, S, lp) if last else None)
            if last:
                return h.reshape(B, S, D_MODEL)
    return h.reshape(B, lp, D_MODEL)[:, N_META:L]
```

```python
import functools
import math

import jax
import jax.numpy as jnp
from jax import lax
from jax.experimental import pallas as pl
from jax.experimental.pallas import tpu as pltpu

F32 = jnp.float32
BF16 = jnp.bfloat16

D_MODEL = 1024
N_META = 16
CHUNK = 64
SEQ_ALIGN = 128
EPS = 1e-6
GROUP_WIDTH = 256
N_HEADS = 4
HEAD_PAD = 128
LRU_C = 8.0
HG_CHUNK = 16
HG_BLOCK = 128
MLA_NOPE, MLA_ROPE, MLA_V = 64, 32, 64
MLA_QK = MLA_NOPE + MLA_ROPE
MLA_Q_RANK, MLA_KV_RANK = 192, 128
ROPE_THETA = 10000.0
FOX_HD = 64
N_EXPERTS = 8
LOG2E = 1.4426950408889634
ONES_LANE = 64
VMEM_LIMIT = 56 * 1024 * 1024

NT_DIMS = (((1,), (1,)), ((), ()))
TN_DIMS = (((0,), (0,)), ((), ()))


def _cparams(*sem):
    return pltpu.CompilerParams(dimension_semantics=sem, vmem_limit_bytes=VMEM_LIMIT)


def _tile(n, pref, align=SEQ_ALIGN):
    best = None
    for t in range(align, min(n, pref) + 1, align):
        if n % t == 0:
            best = t
    assert best is not None, (n, pref, align)
    return best


def _rms(x, width):
    return lax.rsqrt(jnp.sum(x * x, axis=-1, keepdims=True) * (1.0 / width) + EPS)


def _row_ssq(x):
    ones = jnp.ones((x.shape[1], HEAD_PAD), BF16)
    return jnp.dot((x * x).astype(BF16), ones, preferred_element_type=F32)


def _rms_tiles(x, width):
    return lax.rsqrt(_row_ssq(x) * (1.0 / width) + EPS)


def _sigmoid(x):
    return 0.5 * jnp.tanh(0.5 * x) + 0.5


def _log_sigmoid(x):
    return jnp.minimum(x, 0.0) - jnp.log(1.0 + jnp.exp(-jnp.abs(x)))


def _full(shape):
    return pl.BlockSpec(shape, lambda *_: (0,) * len(shape))


def _inproj_kernel(h_ref, g_ref, wl_ref, wh_ref, wm_ref, wf_ref, zl_ref, zh_ref, zm_ref, zf_ref):
    x = h_ref[...]
    u = (x * _rms(x, D_MODEL) * g_ref[...]).astype(BF16)
    zl_ref[...] = jnp.dot(u, wl_ref[...], preferred_element_type=F32)
    zh_ref[...] = jnp.dot(u, wh_ref[...], preferred_element_type=F32)
    zm_ref[...] = jnp.dot(u, wm_ref[...], preferred_element_type=F32)
    zf_ref[...] = jnp.dot(u, wf_ref[...], preferred_element_type=F32)


def _inproj(h, g, wl, wh, wm, wf, tm):
    R = h.shape[0]
    row = lambda n: pl.BlockSpec((tm, n), lambda i: (i, 0))
    return pl.pallas_call(
        _inproj_kernel,
        grid=(R // tm,),
        in_specs=[row(D_MODEL), _full((1, D_MODEL)), _full(wl.shape), _full(wh.shape),
                  _full(wm.shape), _full(wf.shape)],
        out_specs=[row(wl.shape[1]), row(wh.shape[1]), row(wm.shape[1]), row(wf.shape[1])],
        out_shape=[jax.ShapeDtypeStruct((R, w.shape[1]), F32) for w in (wl, wh, wm, wf)],
        compiler_params=_cparams("parallel"),
        name="inproj",
    )(h, g, wl, wh, wm, wf)


def _rglru_kernel(z_ref, cw_ref, cb_ref, wa_ref, ba_ref, wx_ref, bx_ref, lam_ref, gn_ref, y_ref,
                  xbuf, a_s, b_s, h_s, hst):
    W = GROUP_WIDTH
    tt = y_ref.shape[0]

    @pl.when(pl.program_id(1) == 0)
    def _():
        xbuf[0:8, :] = jnp.zeros((8, W), F32)
        hst[...] = jnp.zeros((1, W), F32)

    xa = z_ref[:, 0:W]
    ga = z_ref[:, W:2 * W]
    xbuf[8:8 + tt, :] = xa
    u = (cb_ref[...] + xbuf[5:5 + tt, :] * cw_ref[0:1, :] + xbuf[6:6 + tt, :] * cw_ref[1:2, :]
         + xbuf[7:7 + tt, :] * cw_ref[2:3, :] + xa * cw_ref[3:4, :])
    xbuf[0:8, :] = xbuf[tt:tt + 8, :]

    ub = u.astype(BF16)
    r = _sigmoid(jnp.dot(ub, wa_ref[...], preferred_element_type=F32) + ba_ref[...])
    i = _sigmoid(jnp.dot(ub, wx_ref[...], preferred_element_type=F32) + bx_ref[...])
    lam = lam_ref[...]
    softplus_neg_lam = jnp.maximum(-lam, 0.0) + jnp.log(1.0 + jnp.exp(-jnp.abs(lam)))
    a = jnp.exp((-LRU_C) * r * softplus_neg_lam)
    b = jnp.sqrt(1.0 - a * a) * (i * u)

    row = lax.broadcasted_iota(jnp.int32, (tt, W), 0) & 7
    for s in (1, 2, 4):
        ok = row >= s
        b = jnp.where(ok, a * pltpu.roll(b, s, 0) + b, b)
        a = jnp.where(ok, a * pltpu.roll(a, s, 0), a)
    a_s[...] = a
    b_s[...] = b

    def group(gi, hprev):
        sl = pl.ds(pl.multiple_of(gi * 8, 8), 8)
        hg = b_s[sl, :] + a_s[sl, :] * hprev
        h_s[sl, :] = hg
        return hg[7:8, :]

    hst[...] = lax.fori_loop(0, tt // 8, group, hst[...], unroll=8)

    c0 = math.sqrt(2.0 / math.pi)
    gelu = 0.5 * ga * (1.0 + jnp.tanh(c0 * (ga + 0.044715 * (ga * ga * ga))))
    y = h_s[...] * gelu
    y_ref[...] = (y * _rms(y, W) * gn_ref[...]).astype(BF16)


def _rglru(zl, cw, cb, wa, ba, wx, bx, lam, gn, B, tt):
    R = zl.shape[0]
    nt = R // B // tt
    W = GROUP_WIDTH
    return pl.pallas_call(
        _rglru_kernel,
        grid=(B, nt),
        in_specs=[pl.BlockSpec((tt, 2 * W), lambda b, t: (b * nt + t, 0)),
                  _full((4, W)), _full((1, W)), _full((W, W)), _full((1, W)), _full((W, W)),
                  _full((1, W)), _full((1, W)), _full((1, W))],
        out_specs=pl.BlockSpec((tt, W), lambda b, t: (b * nt + t, 0)),
        out_shape=jax.ShapeDtypeStruct((R, W), BF16),
        scratch_shapes=[pltpu.VMEM((tt + 8, W), F32), pltpu.VMEM((tt, W), F32),
                        pltpu.VMEM((tt, W), F32), pltpu.VMEM((tt, W), F32), pltpu.VMEM((1, W), F32)],
        compiler_params=_cparams("parallel", "arbitrary"),
        name="rglru",
    )(zl, cw, cb, wa, ba, wx, bx, lam, gn)


def _hgrn2_kernel(z_ref, lb_ref, gn_ref, y_ref, qd_s, ke_s, v_s, dec_s, o_s, st_s):
    W = GROUP_WIDTH
    nb, tt = y_ref.shape[0], y_ref.shape[1]
    C = HG_CHUNK

    @pl.when(pl.program_id(0) == 0)
    def _():
        st_s[...] = jnp.zeros(st_s.shape, F32)

    lb = lb_ref[...]
    la = jnp.log(lb)
    l1 = jnp.log(1.0 - lb)
    rowc = lax.broadcasted_iota(jnp.int32, (tt, W), 0) & (C - 1)
    lane_head = lax.broadcasted_iota(jnp.int32, (1, W), 1) // (W // N_HEADS)
    rr = lax.broadcasted_iota(jnp.int32, (HG_BLOCK, HG_BLOCK), 0)
    cc = lax.broadcasted_iota(jnp.int32, (HG_BLOCK, HG_BLOCK), 1)
    amask = (rr // C == cc // C) & (cc <= rr)

    for bi in range(nb):
        q = z_ref[bi, :, 0:W]
        fz = z_ref[bi, :, W:2 * W]
        v = z_ref[bi, :, 2 * W:3 * W]
        lq = l1 + _log_sigmoid(fz)
        logf = jnp.maximum(la, lq) + jnp.log(1.0 + jnp.exp(-jnp.abs(la - lq)))
        kin = 1.0 - jnp.exp(logf)

        b = logf
        s = 1
        while s < C:
            b = b + jnp.where(rowc >= s, pltpu.roll(b, s, 0), 0.0)
            s *= 2
        b3 = b.reshape(tt // C, C, W)
        tail = (jnp.broadcast_to(b3[:, C - 1:C, :], b3.shape) - b3).reshape(tt, W)
        qd = q * jnp.exp(b)
        kd = (kin * jnp.exp(-b)).astype(BF16)
        vb = v.astype(BF16)
        qd_s[bi] = qd.astype(BF16)
        ke_s[bi] = (kin * jnp.exp(tail)).astype(BF16)
        v_s[bi] = vb
        dec_s[bi] = jnp.exp(b + tail)

        for jb in range(tt // HG_BLOCK):
            sl = slice(jb * HG_BLOCK, (jb + 1) * HG_BLOCK)
            qb, kb, vv = qd[sl], kd[sl], vb[sl]
            acc = jnp.zeros((HG_BLOCK, W), F32)
            for hh in range(N_HEADS):
                hm = lane_head == hh
                qh = jnp.where(hm, qb, 0.0).astype(BF16)
                att = lax.dot_general(qh, kb, NT_DIMS, preferred_element_type=F32)
                att = jnp.where(amask, att, 0.0).astype(BF16)
                acc = jnp.where(hm, jnp.dot(att, vv, preferred_element_type=F32), acc)
            o_s[bi, sl, :] = acc

    def chunk(c, carry):
        sl = pl.ds(pl.multiple_of(c * C, C), C)
        for bi in range(nb):
            qc = qd_s[bi, sl, :]
            st = st_s[bi]
            qbd = jnp.concatenate([jnp.where(lane_head == hh, qc, jnp.zeros_like(qc))
                                   for hh in range(N_HEADS)], axis=0)
            res = lax.dot_general(qbd, st.astype(BF16), NT_DIMS, preferred_element_type=F32)
            oi = jnp.zeros((C, W), F32)
            for hh in range(N_HEADS):
                oi = jnp.where(lane_head == hh, res[hh * C:(hh + 1) * C, :], oi)
            o_s[bi, sl, :] = o_s[bi, sl, :] + oi
            upd = lax.dot_general(v_s[bi, sl, :], ke_s[bi, sl, :], TN_DIMS, preferred_element_type=F32)
            st_s[bi] = st * dec_s[bi, pl.ds(c * C, 1), :] + upd
        return carry

    lax.fori_loop(0, tt // C, chunk, 0, unroll=math.gcd(tt // C, 4))

    for bi in range(nb):
        g = z_ref[bi, :, 3 * W:4 * W]
        y = o_s[bi] * (g * _sigmoid(g))
        y_ref[bi] = (y * _rms(y, W) * gn_ref[...]).astype(BF16)


def _hgrn2(zh, lb, gn, B, tt):
    R = zh.shape[0]
    lp = R // B
    W = GROUP_WIDTH
    return pl.pallas_call(
        _hgrn2_kernel,
        grid=(lp // tt,),
        in_specs=[pl.BlockSpec((B, tt, 4 * W), lambda t: (0, t, 0)), _full((1, W)), _full((1, W))],
        out_specs=pl.BlockSpec((B, tt, W), lambda t: (0, t, 0)),
        out_shape=jax.ShapeDtypeStruct((B, lp, W), BF16),
        scratch_shapes=[pltpu.VMEM((B, tt, W), BF16), pltpu.VMEM((B, tt, W), BF16), pltpu.VMEM((B, tt, W), BF16),
                        pltpu.VMEM((B, tt, W), F32), pltpu.VMEM((B, tt, W), F32), pltpu.VMEM((B, W, W), F32)],
        compiler_params=_cparams("arbitrary"),
        name="hgrn2",
    )(zh.reshape(B, lp, 4 * W), lb, gn)


def _rope(x, cos, s_lo, s_hi):
    return x * cos + pltpu.roll(x, 16, 1) * s_hi + pltpu.roll(x, HEAD_PAD - 16, 1) * s_lo


STAT_ROWS = 8
STAT_C_FIRST, STAT_C_LAST, STAT_Q2, STAT_K2 = range(4)
SHIFT_LANE = 100


def _stat_block(rows):
    srow = lax.broadcasted_iota(jnp.int32, (STAT_ROWS, HEAD_PAD), 0)
    stat = jnp.zeros((STAT_ROWS, HEAD_PAD), F32)
    for rr, val in rows:
        stat = jnp.where(srow == rr, val, stat)
    return stat


def _mla_prep_kernel(z_ref, gq_ref, wuq_ref, gkv_ref, wuk_ref, wuv_ref, gqn_ref, gkn_ref,
                     cos_ref, slo_ref, shi_ref, q_ref, k_ref, v_ref, stat_ref):
    P = HEAD_PAD
    ckv = z_ref[:, 0:P]
    krb = z_ref[:, P:2 * P]
    cq = z_ref[:, 2 * P:4 * P]
    rq = _rms_tiles(cq, MLA_Q_RANK)
    qn = (cq * jnp.concatenate([rq, rq], axis=1) * gq_ref[...]).astype(BF16)
    kvn = (ckv * _rms_tiles(ckv, MLA_KV_RANK) * gkv_ref[...]).astype(BF16)
    q = jnp.dot(qn, wuq_ref[...], preferred_element_type=F32)
    kn = jnp.dot(kvn, wuk_ref[...], preferred_element_type=F32)
    vv = jnp.dot(kvn, wuv_ref[...], preferred_element_type=F32)
    cos, slo, shi = cos_ref[...], slo_ref[...], shi_ref[...]
    lane = lax.broadcasted_iota(jnp.int32, (1, P), 1)
    qscale = (MLA_QK ** -0.5) * LOG2E
    q2 = jnp.zeros((1, P), F32)
    k2 = jnp.zeros((1, P), F32)
    for hh in range(N_HEADS):
        sl = slice(hh * P, (hh + 1) * P)
        qh = q[:, sl]
        qh = qh * _rms_tiles(qh, MLA_QK) * gqn_ref[...]
        qh = _rope(qh, cos, slo, shi) * qscale
        q2 = jnp.where(lane == hh, jnp.max(_row_ssq(qh), axis=0, keepdims=True), q2)
        q_ref[:, sl] = qh.astype(BF16)
        kh = kn[:, sl] + krb
        kh = kh * _rms_tiles(kh, MLA_QK) * gkn_ref[...]
        kh = _rope(kh, cos, slo, shi)
        k2 = jnp.where(lane == hh, jnp.max(_row_ssq(kh), axis=0, keepdims=True), k2)
        k_ref[:, sl] = jnp.where(lane == SHIFT_LANE, 1.0, kh).astype(BF16)
        v_ref[:, sl] = jnp.where(lane == ONES_LANE, 1.0, vv[:, sl]).astype(BF16)
    stat_ref[...] = _stat_block(((STAT_Q2, q2), (STAT_K2, k2)))


def _mla_prep(zm, gq, wuq, gkv, wuk, wuv, gqn, gkn, cos, slo, shi, tm):
    R = zm.shape[0]
    P = HEAD_PAD
    nt = cos.shape[0] // tm
    row = lambda n: pl.BlockSpec((tm, n), lambda i: (i, 0))
    tab = pl.BlockSpec((tm, P), lambda i: (i % nt, 0))
    out = jax.ShapeDtypeStruct((R, N_HEADS * P), BF16)
    return pl.pallas_call(
        _mla_prep_kernel,
        grid=(R // tm,),
        in_specs=[row(4 * P), _full((1, 2 * P)), _full(wuq.shape), _full((1, P)), _full(wuk.shape),
                  _full(wuv.shape), _full((1, P)), _full((1, P)), tab, tab, tab],
        out_specs=[row(N_HEADS * P)] * 3 + [pl.BlockSpec((STAT_ROWS, P), lambda i: (i, 0))],
        out_shape=[out, out, out, jax.ShapeDtypeStruct((R // tm * STAT_ROWS, P), F32)],
        compiler_params=_cparams("parallel"),
        name="mla_prep",
    )(zm, gq, wuq, gkv, wuk, wuv, gqn, gkn, cos, slo, shi)


def _fox_prep_kernel(z_ref, bf_ref, gqn_ref, gkn_ref, hsum_ref, place_ref, q_ref, k_ref, v_ref, stat_ref, carry):
    P = HEAD_PAD
    W = GROUP_WIDTH
    tt = q_ref.shape[0]

    @pl.when(pl.program_id(1) == 0)
    def _():
        carry[...] = jnp.zeros((1, P), F32)

    c = _log_sigmoid(z_ref[:, 3 * W:] + bf_ref[...])
    row = lax.broadcasted_iota(jnp.int32, (tt, P), 0)
    s = 1
    while s < tt:
        c = c + jnp.where(row >= s, pltpu.roll(c, s, 0), 0.0)
        s *= 2
    c = c + carry[...]
    carry[...] = c[tt - 1:tt, :]
    c = c * LOG2E
    c1 = c.astype(BF16).astype(F32)
    c2 = (c - c1).astype(BF16).astype(F32)
    c3 = c - c1 - c2

    def norm_place(x, gain):
        ssq = jnp.dot((x * x).astype(BF16), hsum_ref[...], preferred_element_type=F32)
        xn = (x * lax.rsqrt(ssq * (1.0 / FOX_HD) + EPS) * gain).astype(BF16)
        return jnp.dot(xn, place_ref[...], preferred_element_type=F32)

    qscale = (FOX_HD ** -0.5) * LOG2E
    qp = norm_place(z_ref[:, 0:W], gqn_ref[...] * qscale)
    kp = norm_place(z_ref[:, W:2 * W], gkn_ref[...])
    vp = jnp.dot(z_ref[:, 2 * W:3 * W].astype(BF16), place_ref[...], preferred_element_type=F32)

    lane = lax.broadcasted_iota(jnp.int32, (1, P), 1)
    q2 = jnp.zeros((1, P), F32)
    k2 = jnp.zeros((1, P), F32)
    for hh in range(N_HEADS):
        hs = slice(hh * P, (hh + 1) * P)
        p1 = c1[:, hh:hh + 1]
        p2 = c2[:, hh:hh + 1]
        p3 = c3[:, hh:hh + 1]
        qh = qp[:, hs]
        q2 = jnp.where(lane == hh, jnp.max(_row_ssq(qh), axis=0, keepdims=True), q2)
        qh = jnp.where(lane == 64, p1, jnp.where(lane == 65, p2, jnp.where(lane == 66, p3, qh)))
        qh = jnp.where((lane >= 67) & (lane < 70), 1.0, qh)
        q_ref[:, hs] = qh.astype(BF16)
        kh = kp[:, hs]
        k2 = jnp.where(lane == hh, jnp.max(_row_ssq(kh), axis=0, keepdims=True), k2)
        kh = jnp.where(lane == 67, -p1, jnp.where(lane == 68, -p2, jnp.where(lane == 69, -p3, kh)))
        kh = jnp.where(((lane >= 64) & (lane < 67)) | (lane == SHIFT_LANE), 1.0, kh)
        k_ref[:, hs] = kh.astype(BF16)
        v_ref[:, hs] = jnp.where(lane == ONES_LANE, 1.0, vp[:, hs]).astype(BF16)

    stat_ref[...] = _stat_block(((STAT_C_FIRST, c[0:1, :]), (STAT_C_LAST, c[tt - 1:tt, :]),
                                 (STAT_Q2, q2), (STAT_K2, k2)))


def _fox_prep(zf, bf, gqn, gkn, B, tt):
    R = zf.shape[0]
    nt = R // B // tt
    P = HEAD_PAD
    W = GROUP_WIDTH
    blk = lambda n: pl.BlockSpec((tt, n), lambda b, t: (b * nt + t, 0))
    out = jax.ShapeDtypeStruct((R, N_HEADS * P), BF16)
    head_sum = jnp.kron(jnp.eye(N_HEADS, dtype=F32), jnp.ones((FOX_HD, FOX_HD), F32)).astype(BF16)
    place = _pad_heads(jnp.eye(W, dtype=F32), FOX_HD).astype(BF16)
    return pl.pallas_call(
        _fox_prep_kernel,
        grid=(B, nt),
        in_specs=[blk(zf.shape[1]), _full((1, P)), _full((1, W)), _full((1, W)), _full((W, W)),
                  _full((W, N_HEADS * P))],
        out_specs=[blk(N_HEADS * P)] * 3 + [pl.BlockSpec((STAT_ROWS, P), lambda b, t: (b * nt + t, 0))],
        out_shape=[out, out, out, jax.ShapeDtypeStruct((B * nt * STAT_ROWS, P), F32)],
        scratch_shapes=[pltpu.VMEM((1, P), F32)],
        compiler_params=_cparams("parallel", "arbitrary"),
        name="fox_prep",
    )(zf, bf, gqn, gkn, head_sum, place)


SKIP_LOG2_MARGIN = 40.0
NORM_SLACK = 1.02


def _fox_first_chunk(stats, B, nt):
    st = stats.reshape(B, nt, STAT_ROWS, HEAD_PAD)[..., :N_HEADS]
    c_first, c_last = st[:, :, STAT_C_FIRST], st[:, :, STAT_C_LAST]
    bound = jnp.sqrt(jnp.max(st[:, :, STAT_Q2], axis=1) * jnp.max(st[:, :, STAT_K2], axis=1)) * NORM_SLACK
    gap = 2.0 * bound[:, None, None, :] + c_first[:, :, None, :] - c_last[:, None, :, :]
    earlier = jnp.arange(nt)[None, :] < jnp.arange(nt)[:, None]
    skip = (gap < -SKIP_LOG2_MARGIN) & earlier[None, :, :, None]
    return jnp.min(jnp.sum(skip, axis=2), axis=-1).astype(jnp.int32).reshape(-1)


MAX_SCORE_SHIFT = 48.0


def _score_bound(stats, B, nt):
    st = stats.reshape(B, nt, STAT_ROWS, HEAD_PAD)[..., :N_HEADS]
    bound = jnp.sqrt(jnp.max(st[:, :, STAT_Q2], axis=1) * jnp.max(st[:, :, STAT_K2], axis=1)) * NORM_SLACK
    return bound.reshape(-1), jnp.all(bound <= MAX_SCORE_SHIFT)


def _attn_kernel(first_ref, shift_ref, q_ref, k_ref, v_ref, gn_ref, y_ref, m_s, acc_s, o_s, q_s, *,
                 chunk_causal, tk, fixed_shift):
    P = HEAD_PAD
    tq = q_ref.shape[0]
    lp = k_ref.shape[0]
    blk = pl.program_id(0) * pl.num_programs(1) + pl.program_id(1)
    q0 = pl.program_id(1) * tq

    qpos = q0 + lax.broadcasted_iota(jnp.int32, (tq, 1), 0)
    if chunk_causal:
        qlim = N_META + CHUNK * ((qpos + (CHUNK - N_META)) // CHUNK)
        reach = N_META
    else:
        qlim = qpos + 1
        reach = 0
    n_full = (q0 + reach) // tk
    n_diag = (jnp.minimum(q0 + tq, lp) + tk - 1) // tk

    if fixed_shift:
        lane = lax.broadcasted_iota(jnp.int32, (1, P), 1)
        for hh in range(N_HEADS):
            shift = jnp.full((1, P), -shift_ref[pl.program_id(0) * N_HEADS + hh], F32).astype(BF16)
            q_s[hh] = jnp.where(lane == SHIFT_LANE, shift, q_ref[:, hh * P:(hh + 1) * P])
    else:
        m_s[...] = jnp.full(m_s.shape, -jnp.inf, F32)
    acc_s[...] = jnp.zeros(acc_s.shape, F32)

    def visit(k0, width, masked):
        ks = pl.ds(pl.multiple_of(k0, SEQ_ALIGN), width)
        if masked:
            vis = (k0 + lax.broadcasted_iota(jnp.int32, (1, width), 1)) < qlim
        for hh in range(N_HEADS):
            hs = slice(hh * P, (hh + 1) * P)
            qh = q_s[hh] if fixed_shift else q_ref[:, hs]
            s = lax.dot_general(qh, k_ref[ks, hs], NT_DIMS, preferred_element_type=F32)
            if masked:
                s = jnp.where(vis, s, -jnp.inf)
            if fixed_shift:
                acc_s[hh] += jnp.dot(jnp.exp2(s).astype(BF16), v_ref[ks, hs], preferred_element_type=F32)
                continue
            tiles = [s[:, c * P:(c + 1) * P] for c in range(width // P)]
            mx = tiles[0]
            for t in tiles[1:]:
                mx = jnp.maximum(mx, t)
            m_old = m_s[hh]
            m_new = jnp.maximum(m_old, jnp.max(mx, axis=-1, keepdims=True))
            p = jnp.concatenate([jnp.exp2((t - m_new).astype(BF16)) for t in tiles], axis=1)
            acc_s[hh] = jnp.exp2(m_old - m_new) * acc_s[hh] + jnp.dot(
                p, v_ref[ks, hs], preferred_element_type=F32)
            m_s[hh] = m_new

    def full_body(j, carry):
        visit(j * tk, tk, False)
        return carry

    def masked_body(j, carry):
        visit(j * tk, tk, True)
        return carry

    lax.fori_loop(first_ref[blk], n_full, full_body, 0)
    if chunk_causal and tq == tk:
        @pl.when(q0 + tq < lp)
        def _():
            visit(q0, tk + SEQ_ALIGN, True)

        @pl.when(q0 + tq >= lp)
        def _():
            visit(q0, tk, True)
    else:
        lax.fori_loop(n_full, n_diag, masked_body, 0)
        if chunk_causal:
            @pl.when(q0 + tq < lp)
            def _():
                visit(q0 + tq, SEQ_ALIGN, True)

    hd = GROUP_WIDTH // N_HEADS
    for hh in range(N_HEADS):
        acc = acc_s[hh]
        o_s[:, hh * hd:(hh + 1) * hd] = acc[:, 0:hd] / acc[:, ONES_LANE:ONES_LANE + 1]
    y = o_s[...]
    y_ref[...] = (y * _rms(y, GROUP_WIDTH) * gn_ref[...]).astype(BF16)


def _attention(first_chunk, stats, q, k, v, gn, B, tq, tk, chunk_causal):
    R = q.shape[0]
    lp = R // B
    nq = lp // tq
    P = HEAD_PAD
    W = GROUP_WIDTH
    shift, shift_is_safe = _score_bound(stats, B, nq)
    kv_spec = pl.BlockSpec((lp, N_HEADS * P), lambda b, i, fc, sh: (b, 0))

    def run(fixed_shift):
        return pl.pallas_call(
            functools.partial(_attn_kernel, chunk_causal=chunk_causal, tk=tk, fixed_shift=fixed_shift),
            grid_spec=pltpu.PrefetchScalarGridSpec(
                num_scalar_prefetch=2,
                grid=(B, nq),
                in_specs=[pl.BlockSpec((tq, N_HEADS * P), lambda b, i, fc, sh: (b * nq + i, 0)),
                          kv_spec, kv_spec, pl.BlockSpec((1, W), lambda b, i, fc, sh: (0, 0))],
                out_specs=pl.BlockSpec((tq, W), lambda b, i, fc, sh: (b * nq + i, 0)),
                scratch_shapes=[pltpu.VMEM((N_HEADS, tq, P), F32), pltpu.VMEM((N_HEADS, tq, P), F32),
                                pltpu.VMEM((tq, W), F32), pltpu.VMEM((N_HEADS, tq, P), BF16)],
            ),
            out_shape=jax.ShapeDtypeStruct((R, W), BF16),
            compiler_params=_cparams("parallel", "arbitrary"),
            name=("mla_attn" if chunk_causal else "fox_attn") + ("_shift" if fixed_shift else ""),
        )(first_chunk, shift, q, k, v, gn)

    return lax.cond(shift_is_safe, lambda: run(True), lambda: run(False))


ROUTE_E1, ROUTE_E2, ROUTE_R1, ROUTE_R2, ROUTE_G1, ROUTE_G2 = range(6)
ROUTE_ROWS = 8


def _outproj_kernel(h_ref, ya_ref, yb_ref, yc_ref, yd_ref, w_ref, g_ref, *rest, with_router):
    if with_router:
        wr_ref, hn_ref, u_ref, route_ref, route_t_ref, cnt_ref = rest
    else:
        hn_ref, u_ref = rest
    y = jnp.concatenate([ya_ref[...], yb_ref[...], yc_ref[...], yd_ref[...]], axis=1)
    acc = h_ref[...] + jnp.dot(y, w_ref[...], preferred_element_type=F32)
    hn_ref[...] = acc
    u = acc * _rms(acc, D_MODEL) * g_ref[...]
    if not with_router:
        u_ref[...] = u.astype(BF16)
        return
    u_ref[...] = u
    tm = u.shape[0]

    @pl.when(pl.program_id(0) == 0)
    def _():
        cnt_ref[...] = jnp.zeros(cnt_ref.shape, F32)

    u_hi = u.astype(BF16)
    u_lo = (u - u_hi.astype(F32)).astype(BF16)
    hi_both = jnp.dot(u_hi, wr_ref[...], preferred_element_type=F32)
    logits = (hi_both[:, :HEAD_PAD] + hi_both[:, HEAD_PAD:]
              + jnp.dot(u_lo, wr_ref[:, :HEAD_PAD], preferred_element_type=F32))
    lane = lax.broadcasted_iota(jnp.int32, logits.shape, 1).astype(F32)
    lg = jnp.where(lane < N_EXPERTS, logits, -jnp.inf)
    m1 = jnp.max(lg, axis=-1, keepdims=True)
    i1 = jnp.min(jnp.where(lg == m1, lane, 1e9), axis=-1, keepdims=True)
    lg2 = jnp.where(lane == i1, -jnp.inf, lg)
    m2 = jnp.max(lg2, axis=-1, keepdims=True)
    i2 = jnp.min(jnp.where(lg2 == m2, lane, 1e9), axis=-1, keepdims=True)
    e = jnp.exp(m2 - m1)
    g1 = 1.0 / (1.0 + e)
    picks = jnp.where(lane == i1, 1.0, 0.0) + jnp.where(lane == i2, 1.0, 0.0)
    earlier = (lax.broadcasted_iota(jnp.int32, (tm, tm), 0) > lax.broadcasted_iota(jnp.int32, (tm, tm), 1))
    base = cnt_ref[...] + jnp.dot(earlier.astype(BF16), picks.astype(BF16), preferred_element_type=F32)
    r1 = jnp.sum(jnp.where(lane == i1, base, 0.0), axis=-1, keepdims=True)
    r2 = jnp.sum(jnp.where(lane == i2, base, 0.0), axis=-1, keepdims=True)
    cnt_ref[...] = cnt_ref[...] + jnp.sum(picks, axis=0, keepdims=True)
    rec = jnp.zeros(logits.shape, F32)
    for ln, val in ((ROUTE_E1, i1), (ROUTE_E2, i2), (ROUTE_R1, r1), (ROUTE_R2, r2),
                    (ROUTE_G1, g1), (ROUTE_G2, e * g1)):
        rec = jnp.where(lane == ln, val, rec)
    route_ref[...] = rec
    route_t_ref[...] = rec.T[0:ROUTE_ROWS, :]


def _outproj(h, ya, yb, yc, yd, w, g, wr, tm):
    R = h.shape[0]
    W = GROUP_WIDTH
    with_router = wr is not None
    row = lambda n: pl.BlockSpec((tm, n), lambda i: (i, 0))
    nt = yb.shape[1] // tm
    yb_spec = pl.BlockSpec((None, tm, W), lambda i: (i // nt, i % nt, 0))
    in_specs = [row(D_MODEL), row(W), yb_spec, row(W), row(W), _full(w.shape), _full((1, D_MODEL))]
    out_specs = [row(D_MODEL), row(D_MODEL)]
    out_shape = [jax.ShapeDtypeStruct((R, D_MODEL), F32),
                 jax.ShapeDtypeStruct((R, D_MODEL), F32 if with_router else BF16)]
    args = [h, ya, yb, yc, yd, w, g]
    if with_router:
        in_specs.append(_full(wr.shape))
        out_specs += [row(HEAD_PAD), pl.BlockSpec((ROUTE_ROWS, tm), lambda i: (0, i)), _full((1, HEAD_PAD))]
        out_shape += [jax.ShapeDtypeStruct((R, HEAD_PAD), F32), jax.ShapeDtypeStruct((ROUTE_ROWS, R), F32),
                      jax.ShapeDtypeStruct((1, HEAD_PAD), F32)]
        args.append(wr)
    return pl.pallas_call(
        functools.partial(_outproj_kernel, with_router=with_router),
        grid=(R // tm,),
        in_specs=in_specs, out_specs=out_specs, out_shape=out_shape,
        compiler_params=_cparams("arbitrary" if with_router else "parallel"),
        name="outproj_router" if with_router else "outproj",
    )(*args)


def _ffn_kernel(u_ref, h_ref, wg_ref, wu_ref, wd_ref, o_ref):
    @pl.when(pl.program_id(1) == 0)
    def _():
        o_ref[...] = h_ref[...]

    u = u_ref[...]
    a = jnp.dot(u, wg_ref[...], preferred_element_type=F32)
    b = jnp.dot(u, wu_ref[...], preferred_element_type=F32)
    hid = (a * _sigmoid(a) * b).astype(BF16)
    o_ref[...] += jnp.dot(hid, wd_ref[...], preferred_element_type=F32)


def _ffn(u, h, wg, wu, wd, tm, tf):
    R = h.shape[0]
    dff = wg.shape[1]
    return pl.pallas_call(
        _ffn_kernel,
        grid=(R // tm, dff // tf),
        in_specs=[pl.BlockSpec((tm, D_MODEL), lambda i, f: (i, 0)),
                  pl.BlockSpec((tm, D_MODEL), lambda i, f: (i, 0)),
                  pl.BlockSpec((D_MODEL, tf), lambda i, f: (0, f)),
                  pl.BlockSpec((D_MODEL, tf), lambda i, f: (0, f)),
                  pl.BlockSpec((tf, D_MODEL), lambda i, f: (f, 0))],
        out_specs=pl.BlockSpec((tm, D_MODEL), lambda i, f: (i, 0)),
        out_shape=jax.ShapeDtypeStruct((R, D_MODEL), F32),
        compiler_params=_cparams("parallel", "arbitrary"),
        name="ffn",
    )(u, h, wg, wu, wd)


def _row_copy(src_ref, src_row, dst_ref, dst_row, sem):
    return pltpu.make_async_copy(src_ref.at[pl.ds(src_row, 1)], dst_ref.at[pl.ds(dst_row, 1)], sem)


def _dispatch_kernel(pos1_ref, pos2_ref, empty_ref, u_ref, xs_ref, zero_s, sem):
    tm = u_ref.shape[0]
    t0 = pl.program_id(0) * tm
    n_empty = empty_ref.shape[0]

    @pl.when(pl.program_id(0) == 0)
    def _():
        zero_s[...] = jnp.zeros(zero_s.shape, F32)

        def fill(i, carry):
            _row_copy(zero_s, 0, xs_ref, empty_ref[i], sem).start()
            return carry

        lax.fori_loop(0, n_empty, fill, 0, unroll=8)

        def fill_drain(i, carry):
            _row_copy(zero_s, 0, xs_ref, 0, sem).wait()
            return carry

        lax.fori_loop(0, n_empty, fill_drain, 0, unroll=8)

    def issue(r, carry):
        _row_copy(u_ref, r, xs_ref, pos1_ref[t0 + r], sem).start()
        _row_copy(u_ref, r, xs_ref, pos2_ref[t0 + r], sem).start()
        return carry

    lax.fori_loop(0, tm, issue, 0, unroll=8)

    def drain(r, carry):
        _row_copy(u_ref, r, xs_ref, 0, sem).wait()
        _row_copy(u_ref, r, xs_ref, 0, sem).wait()
        return carry

    lax.fori_loop(0, tm, drain, 0, unroll=8)


def _dispatch(pos1, pos2, empty, u, n_slots, tm):
    R = u.shape[0]
    return pl.pallas_call(
        _dispatch_kernel,
        grid_spec=pltpu.PrefetchScalarGridSpec(
            num_scalar_prefetch=3,
            grid=(R // tm,),
            in_specs=[pl.BlockSpec((tm, D_MODEL), lambda i, p1, p2, em: (i, 0))],
            out_specs=pl.BlockSpec(memory_space=pl.ANY),
            scratch_shapes=[pltpu.VMEM((8, D_MODEL), F32), pltpu.SemaphoreType.DMA],
        ),
        out_shape=jax.ShapeDtypeStruct((n_slots, D_MODEL), F32),
        compiler_params=_cparams("arbitrary"),
        name="moe_dispatch",
    )(pos1, pos2, empty, u)


def _expert_ffn_kernel(te_ref, nt_ref, x_ref, wg_ref, wu_ref, wd_ref, y_ref, xb_s):
    f = pl.program_id(1)
    used = pl.program_id(0) < nt_ref[0]

    @pl.when(jnp.logical_not(used) & (f == 0))
    def _():
        y_ref[...] = jnp.zeros(y_ref.shape, F32)

    @pl.when(used)
    def _():
        @pl.when(f == 0)
        def _():
            xb_s[...] = x_ref[...].astype(BF16)

        x = xb_s[...]
        a = jnp.dot(x, wg_ref[...], preferred_element_type=F32)
        b = jnp.dot(x, wu_ref[...], preferred_element_type=F32)
        hid = (a * _sigmoid(a) * b).astype(BF16)
        out = jnp.dot(hid, wd_ref[...], preferred_element_type=F32)

        @pl.when(f == 0)
        def _():
            y_ref[...] = out

        @pl.when(f > 0)
        def _():
            y_ref[...] += out


def _expert_ffn(tile_expert, n_tiles, xs, wg, wu, wd, tm, tf):
    n_slots = xs.shape[0]
    dff = wg.shape[2]
    row_map = lambda i, f, te, nt: (jnp.minimum(i, nt[0] - 1), 0)
    out_map = lambda i, f, te, nt: (i, 0)
    return pl.pallas_call(
        _expert_ffn_kernel,
        grid_spec=pltpu.PrefetchScalarGridSpec(
            num_scalar_prefetch=2,
            grid=(n_slots // tm, dff // tf),
            in_specs=[pl.BlockSpec((tm, D_MODEL), row_map),
                      pl.BlockSpec((None, D_MODEL, tf), lambda i, f, te, nt: (te[i], 0, f)),
                      pl.BlockSpec((None, D_MODEL, tf), lambda i, f, te, nt: (te[i], 0, f)),
                      pl.BlockSpec((None, tf, D_MODEL), lambda i, f, te, nt: (te[i], f, 0))],
            out_specs=pl.BlockSpec((tm, D_MODEL), out_map),
            scratch_shapes=[pltpu.VMEM((tm, D_MODEL), BF16)],
        ),
        out_shape=jax.ShapeDtypeStruct((n_slots, D_MODEL), F32),
        compiler_params=_cparams("arbitrary", "arbitrary"),
        name="moe_expert_ffn",
    )(tile_expert, n_tiles, xs, wg, wu, wd)


def _combine_kernel(pos1_ref, pos2_ref, h_ref, route_ref, ys_ref, o_ref, y1_s, y2_s, sem, *, row0):
    tm = h_ref.shape[0]
    t0 = row0(pl.program_id(0), pl.program_id(1))

    def issue(r, carry):
        _row_copy(ys_ref, pos1_ref[t0 + r], y1_s, r, sem).start()
        _row_copy(ys_ref, pos2_ref[t0 + r], y2_s, r, sem).start()
        return carry

    lax.fori_loop(0, tm, issue, 0, unroll=8)

    def drain(r, carry):
        _row_copy(ys_ref, 0, y1_s, r, sem).wait()
        _row_copy(ys_ref, 0, y2_s, r, sem).wait()
        return carry

    lax.fori_loop(0, tm, drain, 0, unroll=8)
    rec = route_ref[...]
    g1 = rec[:, ROUTE_G1:ROUTE_G1 + 1]
    g2 = rec[:, ROUTE_G2:ROUTE_G2 + 1]
    o_ref[...] = h_ref[...] + g1 * y1_s[...] + g2 * y2_s[...]


def _combine(pos1, pos2, h, route, ys, tm, frames=None):
    R = h.shape[0]
    if frames is None:
        grid = (R // tm, 1)
        row0 = lambda i, j: i * tm
        in_rows = lambda n: pl.BlockSpec((tm, n), lambda i, j, p1, p2: (i, 0))
        out_rows = R
        out_spec = pl.BlockSpec((tm, D_MODEL), lambda i, j, p1, p2: (i, 0))
    else:
        B, S, lp = frames
        grid = (B, S // tm)
        row0 = lambda b, w: b * lp + N_META + w * tm
        in_rows = lambda n: pl.BlockSpec((pl.Element(tm), pl.Element(n)),
                                         lambda b, w, p1, p2: (pl.multiple_of(row0(b, w), 8), 0))
        out_rows = B * S
        out_spec = pl.BlockSpec((tm, D_MODEL), lambda b, w, p1, p2: (b * (S // tm) + w, 0))
    return pl.pallas_call(
        functools.partial(_combine_kernel, row0=row0),
        grid_spec=pltpu.PrefetchScalarGridSpec(
            num_scalar_prefetch=2,
            grid=grid,
            in_specs=[in_rows(D_MODEL), in_rows(route.shape[1]), pl.BlockSpec(memory_space=pl.ANY)],
            out_specs=out_spec,
            scratch_shapes=[pltpu.VMEM((tm, D_MODEL), F32), pltpu.VMEM((tm, D_MODEL), F32),
                            pltpu.SemaphoreType.DMA],
        ),
        out_shape=jax.ShapeDtypeStruct((out_rows, D_MODEL), F32),
        compiler_params=_cparams("arbitrary", "arbitrary"),
        name="moe_combine",
    )(pos1, pos2, h, route, ys)


def _moe(u, h, route, route_t, counts, wg, wu, wd, tm_rows, tm_expert, tf, frames=None):
    R = h.shape[0]
    ne = wg.shape[0]
    cnt = counts[0, :ne].astype(jnp.int32)
    padded = -(-cnt // tm_expert) * tm_expert
    ends = jnp.cumsum(padded)
    offs = ends - padded
    col = lambda ln: route_t[ln].astype(jnp.int32)
    pos1 = jnp.take(offs, col(ROUTE_E1)) + col(ROUTE_R1)
    pos2 = jnp.take(offs, col(ROUTE_E2)) + col(ROUTE_R2)
    n_slots = (2 * R // tm_expert + ne) * tm_expert
    n_tiles = (ends[-1] // tm_expert).reshape(1)
    tile_start = jnp.arange(n_slots // tm_expert, dtype=jnp.int32) * tm_expert
    tile_expert = jnp.sum(tile_start[:, None] >= ends[None, :], axis=1).astype(jnp.int32)
    last_expert = jnp.sum((ends[-1] - 1) >= ends).astype(jnp.int32)
    tile_expert = jnp.minimum(tile_expert, last_expert)

    n_empty = n_slots - 2 * R
    starts = jnp.concatenate([offs + cnt, ends[-1:]])
    sizes = jnp.concatenate([padded - cnt, n_slots - ends[-1:]])
    cum = jnp.cumsum(sizes)
    k = jnp.arange(n_empty, dtype=jnp.int32)
    rng = jnp.sum(k[:, None] >= cum[None, :], axis=1)
    empty = (jnp.take(starts, rng) + k - jnp.take(cum - sizes, rng)).astype(jnp.int32)

    xs = _dispatch(pos1, pos2, empty, u, n_slots, tm_rows)
    ys = _expert_ffn(tile_expert, n_tiles, xs, wg, wu, wd, tm_expert, tf)
    if frames is None:
        return _combine(pos1, pos2, h, route, ys, tm_rows)
    return _combine(pos1, pos2, h, route, ys, _tile(frames[1], tm_rows), frames)


def _pad_heads(w, real):
    rows = w.shape[0]
    w = w.reshape(rows, N_HEADS, real)
    return jnp.pad(w, ((0, 0), (0, 0), (0, HEAD_PAD - real))).reshape(rows, N_HEADS * HEAD_PAD)


def _pad_cols(w, n):
    return jnp.pad(w, ((0, 0), (0, n - w.shape[1])))


def _row(v, n=None):
    v = v.reshape(1, -1).astype(F32)
    return v if n is None else _pad_cols(v, n)


def _block_diag(w):
    nb, c, d = w.shape
    eye = jnp.eye(nb, dtype=w.dtype)
    return (eye[:, None, :, None] * w[:, :, None, :]).reshape(nb * c, nb * d)


def _rope_tables(lp):
    half = MLA_ROPE // 2
    inv = ROPE_THETA ** (-jnp.arange(half, dtype=F32) / half)
    ang = jnp.arange(lp, dtype=jnp.int32).astype(F32)[:, None] * inv[None, :]
    cos, sin = jnp.cos(ang), jnp.sin(ang)
    one = jnp.ones((lp, MLA_NOPE), F32)
    zero = jnp.zeros((lp, MLA_NOPE), F32)
    zh = jnp.zeros((lp, half), F32)
    tail1 = jnp.ones((lp, HEAD_PAD - MLA_QK), F32)
    tail0 = jnp.zeros((lp, HEAD_PAD - MLA_QK), F32)
    cos_t = jnp.concatenate([one, cos, cos, tail1], axis=1)
    s_lo = jnp.concatenate([zero, -sin, zh, tail0], axis=1)
    s_hi = jnp.concatenate([zero, zh, sin, tail0], axis=1)
    return cos_t, s_lo, s_hi


def kernel(x, meta, norm1_g, norm2_g, w_in, w_out, out_norm_g, lru_conv_w, lru_conv_b, lru_wa, lru_ba, lru_wx, lru_bx, lru_lambda, hg_lb_logits, mla_gq, mla_w_uq, mla_gkv, mla_w_ukv, mla_gqn, mla_gkn, fox_gqn, fox_gkn, fox_bf, ffn_w_gate, ffn_w_up, ffn_w_down, moe_w_router, moe_w_gate, moe_w_up, moe_w_down):
    B, S, _ = x.shape
    depth = w_in.shape[0]
    L = N_META + S
    lp = -(-L // SEQ_ALIGN) * SEQ_ALIGN
    R = B * lp
    W = GROUP_WIDTH
    P = HEAD_PAD

    tt = _tile(lp, 640)
    tq = tk = tt

    h = jnp.concatenate([jnp.broadcast_to(meta[None].astype(x.dtype), (B, N_META, D_MODEL)), x,
                         jnp.zeros((B, lp - L, D_MODEL), x.dtype)], axis=1).reshape(R, D_MODEL)
    cos_t, s_lo, s_hi = _rope_tables(lp)
    lb_cum = jnp.cumsum(jax.nn.softmax(hg_lb_logits.astype(F32), axis=0), axis=0)

    o = 0
    offs = []
    for n in (W, W, W, W, W, W, MLA_Q_RANK, MLA_KV_RANK, MLA_ROPE, W, W, W, N_HEADS):
        offs.append(o)
        o += n
    (o_xa, _, o_hq, _, _, _, o_cq, o_ckv, o_kr, o_fq, o_fk, o_fv, o_ff) = offs

    for l in range(depth):
        wcols = lambda start, n: lax.slice(w_in, (l, 0, start), (l + 1, D_MODEL, start + n))[0].astype(BF16)
        wl = wcols(o_xa, 2 * W)
        wh = wcols(o_hq, 4 * W)
        wm = jnp.concatenate([
            wcols(o_ckv, MLA_KV_RANK),
            jnp.zeros((D_MODEL, MLA_NOPE), BF16), wcols(o_kr, MLA_ROPE),
            jnp.zeros((D_MODEL, P - MLA_QK), BF16),
            _pad_cols(wcols(o_cq, MLA_Q_RANK), 2 * P)], axis=1)
        wf = jnp.concatenate([wcols(o_fq, 3 * W), _pad_cols(wcols(o_ff, N_HEADS), P)], axis=1)
        zl, zh, zm, zf = _inproj(h, _row(norm1_g[l]), wl, wh, wm, wf, tt)

        gn = out_norm_g[l].astype(F32)
        ya = _rglru(zl, lru_conv_w[l].astype(F32), _row(lru_conv_b[l]),
                    _block_diag(lru_wa[l]).astype(BF16), _row(lru_ba[l]),
                    _block_diag(lru_wx[l]).astype(BF16), _row(lru_bx[l]),
                    _row(lru_lambda[l]), _row(gn[0:W]), B, tt)
        yb = _hgrn2(zh, _row(lb_cum[l] - lb_cum[0]), _row(gn[W:2 * W]), B, tt)

        wuq = jnp.pad(_pad_heads(mla_w_uq[l], MLA_QK), ((0, 2 * P - MLA_Q_RANK), (0, 0))).astype(BF16)
        wukv = mla_w_ukv[l].reshape(MLA_KV_RANK, N_HEADS, MLA_NOPE + MLA_V)
        wuk = _pad_heads(wukv[:, :, :MLA_NOPE].reshape(MLA_KV_RANK, -1), MLA_NOPE).astype(BF16)
        wuv = _pad_heads(wukv[:, :, MLA_NOPE:].reshape(MLA_KV_RANK, -1), MLA_V).astype(BF16)
        q, k, v, stats = _mla_prep(zm, _row(mla_gq[l], 2 * P), wuq, _row(mla_gkv[l]), wuk, wuv,
                                   _row(mla_gqn[l], P), _row(mla_gkn[l], P), cos_t, s_lo, s_hi, tt)
        visit_all = jnp.zeros((B * (lp // tq),), jnp.int32)
        yc = _attention(visit_all, stats, q, k, v, _row(gn[2 * W:3 * W]), B, tq, tk, True)

        q, k, v, stats = _fox_prep(zf, _row(fox_bf[l], P), _row(jnp.tile(fox_gqn[l], N_HEADS)),
                                   _row(jnp.tile(fox_gkn[l], N_HEADS)), B, tt)
        yd = _attention(_fox_first_chunk(stats, B, lp // tt), stats, q, k, v, _row(gn[3 * W:4 * W]),
                        B, tq, tk, False)

        wo = w_out[l].astype(BF16)
        if l % 2 == 0:
            hn, u2 = _outproj(h, ya, yb, yc, yd, wo, _row(norm2_g[l]), None, tt)
            j = l // 2
            h = _ffn(u2, hn, ffn_w_gate[j].astype(BF16), ffn_w_up[j].astype(BF16),
                     ffn_w_down[j].astype(BF16), tt, 1408)
        else:
            j = l // 2
            wr = _pad_cols(moe_w_router[j].astype(F32), P)
            wr_hi = wr.astype(BF16)
            wr = jnp.concatenate([wr_hi, (wr - wr_hi.astype(F32)).astype(BF16)], axis=1)
            hn, u2, route, route_t, counts = _outproj(h, ya, yb, yc, yd, wo, _row(norm2_g[l]), wr, tt)
            last = l == depth - 1
            h = _moe(u2, hn, route, route_t, counts, moe_w_gate[j].astype(BF16), moe_w_up[j].astype(BF16),
                     moe_w_down[j].astype(BF16), tt, 512, 1792, (B, S, lp) if last else None)
            if last:
                return h.reshape(B, S, D_MODEL)
    return h.reshape(B, lp, D_MODEL)[:, N_META:L]
```
